```python
import math
import jax, jax.numpy as jnp
from jax import lax
import numpy as np

D_MODEL = 1024
BATCH = 1
SEQ = 16384
DEPTH = 1

HEAD_DIM = 64
SWA_HEADS = 8
SWA_KV_HEADS = 2
SWA_GROUP = SWA_HEADS // SWA_KV_HEADS
WINDOW = 128
FOX_HEADS = 8
BLOCK = 128
REL_BUCKETS = 32
REL_MAX_DIST = WINDOW
N_EXPERTS = 32
TOP_K = 4
D_FF = D_MODEL
SWIGLU_LIMIT = 7.0
SWIGLU_ALPHA = 1.702
RMS_EPS = 1e-5
N_MOD = 6

SWA_Q = SWA_HEADS * HEAD_DIM
SWA_KV = SWA_KV_HEADS * HEAD_DIM
FOX_W = FOX_HEADS * HEAD_DIM
IN_SPLITS = (SWA_Q, SWA_KV, SWA_KV, FOX_W, FOX_W, FOX_W, FOX_HEADS, D_MODEL, D_MODEL)
D_IN = sum(IN_SPLITS)

kernel_name = "gated_hybrid_swa_fox_moe_adaln"


def rms_norm(x, g):
    xf = x.astype(jnp.float32)
    y = xf * lax.rsqrt(jnp.mean(xf * xf, axis=-1, keepdims=True) + RMS_EPS)
    return (y * g.astype(jnp.float32)).astype(x.dtype)


def modulate(h, shift, scale):
    return h * (1.0 + scale[:, None, :]) + shift[:, None, :]


def t5_causal_buckets(dist):
    n = jnp.maximum(dist, 0)
    max_exact = REL_BUCKETS // 2
    nf = jnp.maximum(n, 1).astype(jnp.float32)
    large = max_exact + (jnp.log(nf / max_exact) / math.log(REL_MAX_DIST / max_exact)
                         * (REL_BUCKETS - max_exact)).astype(jnp.int32)
    large = jnp.minimum(large, REL_BUCKETS - 1)
    return jnp.where(n < max_exact, n, large)


def sliding_window_gqa(q, k, v, sinks, rel_table):
    B, S = q.shape[0], q.shape[1]
    nb = S // BLOCK
    qb = q.reshape(B, nb, BLOCK, SWA_KV_HEADS, SWA_GROUP, HEAD_DIM)
    kb = k.reshape(B, nb, BLOCK, SWA_KV_HEADS, HEAD_DIM)
    vb = v.reshape(B, nb, BLOCK, SWA_KV_HEADS, HEAD_DIM)
    shift = lambda t: jnp.concatenate([jnp.zeros_like(t[:, :1]), t[:, :-1]], axis=1)
    kk = jnp.concatenate([shift(kb), kb], axis=2)
    vv = jnp.concatenate([shift(vb), vb], axis=2)
    scores = jnp.einsum('bnqhgd,bnkhd->bnhgqk', qb, kk).astype(jnp.float32) * (HEAD_DIM ** -0.5)

    qi = jnp.arange(BLOCK)[:, None]
    kj = jnp.arange(2 * BLOCK)[None, :]
    dist = BLOCK + qi - kj
    bias = rel_table.astype(jnp.float32)[t5_causal_buckets(dist)]
    bias = bias.transpose(2, 0, 1).reshape(SWA_KV_HEADS, SWA_GROUP, BLOCK, 2 * BLOCK)
    band = (dist >= 0) & (dist < WINDOW)
    blk = jnp.arange(nb)[:, None, None]
    key_ok = (blk * BLOCK - BLOCK + kj[None]) >= 0
    mask = band[None] & key_ok
    scores = jnp.where(mask[None, :, None, None], scores + bias, -jnp.inf)

    sink = jnp.broadcast_to(sinks.astype(jnp.float32).reshape(SWA_KV_HEADS, SWA_GROUP)[None, None, :, :, None, None],
                            scores.shape[:-1] + (1,))
    p = jax.nn.softmax(jnp.concatenate([scores, sink], axis=-1), axis=-1)[..., :-1]
    out = jnp.einsum('bnhgqk,bnkhd->bnqhgd', p.astype(v.dtype), vv)
    return out.reshape(B, S, SWA_Q)


def forgetting_attention(q, k, v, logf):
    B, S = q.shape[0], q.shape[1]
    nb = S // BLOCK
    cum_t = lax.cumsum(logf.astype(jnp.float32), axis=1).transpose(0, 2, 1)
    kpos = jnp.arange(S)

    def one_block(i):
        start = i * BLOCK
        qi = lax.dynamic_slice_in_dim(q, start, BLOCK, axis=1)
        ci = lax.dynamic_slice_in_dim(cum_t, start, BLOCK, axis=2)
        s = (jnp.einsum('bqhd,bkhd->bhqk', qi, k).astype(jnp.float32) * (HEAD_DIM ** -0.5)
             + ci[..., None] - cum_t[:, :, None, :])
        qpos = start + jnp.arange(BLOCK)
        s = jnp.where(qpos[:, None] >= kpos[None, :], s, -jnp.inf)
        p = jax.nn.softmax(s, axis=-1).astype(v.dtype)
        return jnp.einsum('bhqk,bkhd->bqhd', p, v)

    out = lax.map(one_block, jnp.arange(nb))
    return out.transpose(1, 0, 2, 3, 4).reshape(B, S, FOX_W)


def moe_ffn(u, w_router, b_router, w_e1, b_e1, w_e2, b_e2):
    logits = (u @ w_router).astype(jnp.float32) + b_router.astype(jnp.float32)
    top_v, top_i = lax.top_k(logits, TOP_K)
    top_w = jax.nn.softmax(top_v, axis=-1)
    combine = jnp.sum(jax.nn.one_hot(top_i, N_EXPERTS, dtype=jnp.float32) * top_w[..., None], axis=-2)
    combine = combine.astype(u.dtype)
    out = jnp.zeros_like(u)
    for e in range(N_EXPERTS):
        h = u @ w_e1[e] + b_e1[e]
        glu = jnp.minimum(h[..., ::2], SWIGLU_LIMIT)
        lin = jnp.clip(h[..., 1::2], -SWIGLU_LIMIT, SWIGLU_LIMIT)
        a = glu * jax.nn.sigmoid(SWIGLU_ALPHA * glu) * (lin + 1.0)
        out = out + combine[..., e:e + 1] * (a @ w_e2[e] + b_e2[e])
    return out


def setup_inputs(seed: int = 0) -> dict:
    key = jax.random.key(seed)
    ks = jax.random.split(key, 20)
    f32 = jnp.float32
    nrm = lambda k, shape, s: jax.random.normal(k, shape, f32) * s
    return {
        "x": nrm(ks[0], (BATCH, SEQ, D_MODEL), 1.0),
        "c": nrm(ks[1], (BATCH, D_MODEL), 1.0),
        "w_ada": nrm(ks[2], (DEPTH, D_MODEL, N_MOD * D_MODEL), D_MODEL ** -0.5),
        "b_ada": nrm(ks[3], (DEPTH, N_MOD * D_MODEL), 0.02),
        "g_mix": 1.0 + nrm(ks[4], (DEPTH, D_MODEL), 0.02),
        "w_in": nrm(ks[5], (DEPTH, D_MODEL, D_IN), D_MODEL ** -0.5),
        "b_forget": 2.0 + nrm(ks[6], (DEPTH, FOX_HEADS), 0.5),
        "sinks": nrm(ks[7], (DEPTH, SWA_HEADS), 0.5),
        "rel_bias": nrm(ks[8], (REL_BUCKETS, SWA_HEADS), 0.1),
        "w_proj_a": nrm(ks[9], (DEPTH, SWA_Q, D_MODEL), SWA_Q ** -0.5),
        "w_proj_b": nrm(ks[10], (DEPTH, FOX_W, D_MODEL), FOX_W ** -0.5),
        "w_out": nrm(ks[11], (DEPTH, D_MODEL, D_MODEL), D_MODEL ** -0.5),
        "g_ffn": 1.0 + nrm(ks[12], (DEPTH, D_MODEL), 0.02),
        "w_router": nrm(ks[13], (DEPTH, D_MODEL, N_EXPERTS), D_MODEL ** -0.5),
        "b_router": nrm(ks[14], (DEPTH, N_EXPERTS), 0.01),
        "w_e1": nrm(ks[15], (DEPTH, N_EXPERTS, D_MODEL, 2 * D_FF), D_MODEL ** -0.5),
        "b_e1": nrm(ks[16], (DEPTH, N_EXPERTS, 2 * D_FF), 0.02),
        "w_e2": nrm(ks[17], (DEPTH, N_EXPERTS, D_FF, D_MODEL), D_FF ** -0.5),
        "b_e2": nrm(ks[18], (DEPTH, N_EXPERTS, D_MODEL), 0.02),
        "g_final": 1.0 + nrm(ks[19], (D_MODEL,), 0.02),
    }


def reference(x, c, w_ada, b_ada, g_mix, w_in, b_forget, sinks, rel_bias, w_proj_a, w_proj_b, w_out,
              g_ffn, w_router, b_router, w_e1, b_e1, w_e2, b_e2, g_final):
    B, S = x.shape[0], x.shape[1]
    split_idx = [int(v) for v in np.cumsum(IN_SPLITS)[:-1]]
    c_act = jax.nn.silu(c)
    for l in range(DEPTH):
        mod = c_act @ w_ada[l] + b_ada[l]
        sh_m, sc_m, gt_m, sh_f, sc_f, gt_f = jnp.split(mod, N_MOD, axis=-1)

        u = modulate(rms_norm(x, g_mix[l]), sh_m, sc_m)
        z = u @ w_in[l]
        qa, ka, va, qb, kb, vb, fb, ga, gb = jnp.split(z, split_idx, axis=-1)
        ya = sliding_window_gqa(qa.reshape(B, S, SWA_HEADS, HEAD_DIM),
                                ka.reshape(B, S, SWA_KV_HEADS, HEAD_DIM),
                                va.reshape(B, S, SWA_KV_HEADS, HEAD_DIM),
                                sinks[l], rel_bias)
        logf = jax.nn.log_sigmoid(fb.astype(jnp.float32) + b_forget[l].astype(jnp.float32))
        yb = forgetting_attention(qb.reshape(B, S, FOX_HEADS, HEAD_DIM),
                                  kb.reshape(B, S, FOX_HEADS, HEAD_DIM),
                                  vb.reshape(B, S, FOX_HEADS, HEAD_DIM), logf)
        merged = jax.nn.sigmoid(ga) * (ya @ w_proj_a[l]) + jax.nn.sigmoid(gb) * (yb @ w_proj_b[l])
        x = x + gt_m[:, None, :] * (merged @ w_out[l])

        u = modulate(rms_norm(x, g_ffn[l]), sh_f, sc_f)
        x = x + gt_f[:, None, :] * moe_ffn(u, w_router[l], b_router[l], w_e1[l], b_e1[l], w_e2[l], b_e2[l])
    return rms_norm(x, g_final)
```

```python
import functools
import math

import numpy as np
import jax
import jax.numpy as jnp
from jax import lax
from jax.experimental import pallas as pl
from jax.experimental.pallas import tpu as pltpu

D_MODEL = 1024
HEAD_DIM = 64
SWA_HEADS = 8
SWA_KV_HEADS = 2
WINDOW = 128
FOX_HEADS = 8
BLOCK = 128
REL_BUCKETS = 32
REL_MAX_DIST = WINDOW
N_EXPERTS = 32
TOP_K = 4
D_FF = D_MODEL
SWIGLU_LIMIT = 7.0
SWIGLU_ALPHA = 1.702
RMS_EPS = 1e-5
N_MOD = 6

SWA_Q = SWA_HEADS * HEAD_DIM
SWA_KV = SWA_KV_HEADS * HEAD_DIM
FOX_W = FOX_HEADS * HEAD_DIM
LANES = 128
N_PAIRS = FOX_HEADS // 2
NEG_BIG = -1e30
VMEM_LIMIT = 56 * 1024 * 1024

F32 = jnp.float32
BF16 = jnp.bfloat16
HIGHEST = lax.Precision.HIGHEST


def _dot(a, b):
    return jnp.dot(a, b, preferred_element_type=F32)


def _dot_nt(a, b, precision=None):
    return lax.dot_general(a, b, (((1,), (1,)), ((), ())), preferred_element_type=F32, precision=precision)


def _const_spec(shape):
    nd = len(shape)
    return pl.BlockSpec(shape, lambda *_: (0,) * nd)


def _params(sem):
    return pltpu.CompilerParams(dimension_semantics=sem, vmem_limit_bytes=VMEM_LIMIT)


def _ada_kernel(c_ref, w_ref, b_ref, o_ref):
    c = c_ref[...]
    act = c * jax.nn.sigmoid(c)
    o_ref[...] = jnp.dot(act, w_ref[...], preferred_element_type=F32, precision=HIGHEST) + b_ref[...]


def _ada(c8, w_ada, b_ada):
    n = w_ada.shape[1]
    tn = 1024
    return pl.pallas_call(
        _ada_kernel,
        grid=(n // tn,),
        in_specs=[_const_spec((8, D_MODEL)),
                  pl.BlockSpec((D_MODEL, tn), lambda j: (0, j)),
                  pl.BlockSpec((1, tn), lambda j: (0, j))],
        out_specs=pl.BlockSpec((8, tn), lambda j: (0, j)),
        out_shape=jax.ShapeDtypeStruct((8, n), F32),
        compiler_params=_params(("arbitrary",)),
        name="ada_mod",
    )(c8, w_ada, b_ada)


def _inproj_kernel(x_ref, g_ref, sh_ref, sc_ref, wa_ref, wb_ref, wg_ref, wf_ref, wft_ref, bf_ref, bft_ref,
                   qa_ref, kva_ref, qb_ref, kb_ref, vb_ref, sga_ref, sgb_ref, c_ref, ct_ref,
                   carry_ref, carryt_ref):
    i = pl.program_id(0)
    tm = x_ref.shape[0]

    @pl.when(i == 0)
    def _():
        carry_ref[...] = jnp.zeros_like(carry_ref)
        carryt_ref[...] = jnp.zeros_like(carryt_ref)

    xf = x_ref[...]
    ms = jnp.mean(xf * xf, axis=-1, keepdims=True)
    y = xf * lax.rsqrt(ms + RMS_EPS) * g_ref[...]
    u = y * (1.0 + sc_ref[...]) + sh_ref[...]
    ub = u.astype(BF16)

    za = _dot(ub, wa_ref[...])
    qa_ref[...] = (za[:, :SWA_Q] * (HEAD_DIM ** -0.5)).astype(BF16)
    kva_ref[...] = za[:, SWA_Q:].astype(BF16)

    zb = _dot(ub, wb_ref[...])
    qb_ref[...] = (zb[:, :FOX_W] * (HEAD_DIM ** -0.5)).astype(BF16)
    kb_ref[...] = zb[:, FOX_W:2 * FOX_W].astype(BF16)
    vb_ref[...] = zb[:, 2 * FOX_W:].astype(BF16)

    zg = _dot(ub, wg_ref[...])
    sg = jax.nn.sigmoid(zg)
    sga_ref[...] = sg[:, :D_MODEL].astype(BF16)
    sgb_ref[...] = sg[:, D_MODEL:].astype(BF16)

    def log_sigmoid(z):
        return jnp.minimum(z, 0.0) - jnp.log1p(jnp.exp(-jnp.abs(z)))

    fb = jnp.dot(u, wf_ref[...], preferred_element_type=F32, precision=HIGHEST) + bf_ref[...]
    lf = log_sigmoid(fb)
    r = lax.broadcasted_iota(jnp.int32, (tm, tm), 0)
    cc = lax.broadcasted_iota(jnp.int32, (tm, tm), 1)
    lower = (cc <= r).astype(F32)
    cl = jnp.dot(lower, lf, preferred_element_type=F32, precision=HIGHEST) + carry_ref[...]
    c_ref[...] = cl
    carry_ref[...] = cl[tm - 1:tm, :]

    fbt = _dot_nt(wft_ref[...], u, precision=HIGHEST) + bft_ref[...]
    lft = log_sigmoid(fbt)
    upper = (r <= cc).astype(F32)
    clt = jnp.dot(lft, upper, preferred_element_type=F32, precision=HIGHEST) + carryt_ref[...]
    ct_ref[...] = clt
    carryt_ref[...] = clt[:, tm - 1:tm]


def _inproj(x2, g, sh, sc, wa, wb, wg, wf, wft, bfor, bfort, tm):
    s = x2.shape[0]
    row = lambda n: pl.BlockSpec((tm, n), lambda i: (i, 0))
    out_shape = [
        jax.ShapeDtypeStruct((s, SWA_Q), BF16),
        jax.ShapeDtypeStruct((s, 4 * LANES), BF16),
        jax.ShapeDtypeStruct((s, FOX_W), BF16),
        jax.ShapeDtypeStruct((s, FOX_W), BF16),
        jax.ShapeDtypeStruct((s, FOX_W), BF16),
        jax.ShapeDtypeStruct((s, D_MODEL), BF16),
        jax.ShapeDtypeStruct((s, D_MODEL), BF16),
        jax.ShapeDtypeStruct((s, FOX_HEADS), F32),
        jax.ShapeDtypeStruct((FOX_HEADS, s), F32),
    ]
    out_specs = [row(SWA_Q), row(4 * LANES), row(FOX_W), row(FOX_W), row(FOX_W), row(D_MODEL), row(D_MODEL),
                 row(FOX_HEADS), pl.BlockSpec((FOX_HEADS, tm), lambda i: (0, i))]
    return pl.pallas_call(
        _inproj_kernel,
        grid=(s // tm,),
        in_specs=[row(D_MODEL), _const_spec(g.shape), _const_spec(sh.shape), _const_spec(sc.shape),
                  _const_spec(wa.shape), _const_spec(wb.shape), _const_spec(wg.shape),
                  _const_spec(wf.shape), _const_spec(wft.shape), _const_spec(bfor.shape), _const_spec(bfort.shape)],
        out_specs=out_specs,
        out_shape=out_shape,
        scratch_shapes=[pltpu.VMEM((1, FOX_HEADS), F32), pltpu.VMEM((FOX_HEADS, 1), F32)],
        compiler_params=_params(("arbitrary",)),
        name="in_proj",
    )(x2, g, sh, sc, wa, wb, wg, wf, wft, bfor, bfort)


def _t5_buckets_np():
    qi = np.arange(BLOCK)[:, None]
    kj = np.arange(2 * BLOCK)[None, :]
    dist = BLOCK + qi - kj
    n = np.maximum(dist, 0)
    max_exact = REL_BUCKETS // 2
    nf = np.maximum(n, 1).astype(np.float32)
    large = max_exact + (np.log(nf / np.float32(max_exact)) / np.float32(math.log(REL_MAX_DIST / max_exact))
                         * np.float32(REL_BUCKETS - max_exact)).astype(np.int32)
    large = np.minimum(large, REL_BUCKETS - 1)
    bucket = np.where(n < max_exact, n, large).astype(np.int32)
    band = (dist >= 0) & (dist < WINDOW)
    return np.where(band, bucket, -1).astype(np.int32)


def _swa_kernel(rel_ref, sink_ref, bkt_ref, q_ref, kvc_ref, kvp_ref, o_ref, bias_ref):
    n = pl.program_id(0)

    @pl.when(n == 0)
    def _():
        bkt = bkt_ref[...]
        for h in range(SWA_HEADS):
            acc = jnp.full(bkt.shape, NEG_BIG, F32)
            for b in range(REL_BUCKETS):
                acc = jnp.where(bkt == b, rel_ref[b * SWA_HEADS + h], acc)
            bias_ref[h // 2, (h % 2) * BLOCK:(h % 2 + 1) * BLOCK, :] = acc

    lane = lax.broadcasted_iota(jnp.int32, (BLOCK, LANES), 1)
    row2 = lax.broadcasted_iota(jnp.int32, (2 * BLOCK, 1), 0)
    col = lax.broadcasted_iota(jnp.int32, (2 * BLOCK, 2 * BLOCK), 1)
    key_ok = col >= jnp.where(n > 0, 0, BLOCK)
    kv = jnp.concatenate([kvp_ref[...], kvc_ref[...]], axis=0)
    for p in range(SWA_HEADS // 2):
        g = p // 2
        qp = q_ref[:, p * LANES:(p + 1) * LANES]
        zero = jnp.zeros_like(qp)
        qs = jnp.concatenate([jnp.where(lane < HEAD_DIM, qp, zero), jnp.where(lane >= HEAD_DIM, qp, zero)], axis=0)
        kd = kv[:, g * LANES:(g + 1) * LANES]
        vd = kv[:, (2 + g) * LANES:(3 + g) * LANES]
        s = _dot_nt(qs, kd) + bias_ref[p]
        s = jnp.where(key_ok, s, NEG_BIG)
        sink = jnp.where(row2 < BLOCK, sink_ref[2 * p], sink_ref[2 * p + 1])
        m = jnp.maximum(jnp.max(s, axis=-1, keepdims=True), sink)
        e = jnp.exp(s - m)
        denom = jnp.sum(e, axis=-1, keepdims=True) + jnp.exp(sink - m)
        o = _dot(e.astype(BF16), vd) / denom
        o_ref[:, p * LANES:(p + 1) * LANES] = jnp.where(lane < HEAD_DIM, o[:BLOCK], o[BLOCK:]).astype(BF16)


def _swa(rel_flat, sinks, bkt, qa, kva):
    s = qa.shape[0]
    nb = s // BLOCK
    smem = pl.BlockSpec(memory_space=pltpu.SMEM)
    return pl.pallas_call(
        _swa_kernel,
        grid=(nb,),
        in_specs=[smem, smem, _const_spec(bkt.shape),
                  pl.BlockSpec((BLOCK, SWA_Q), lambda n: (n, 0)),
                  pl.BlockSpec((BLOCK, 4 * LANES), lambda n: (n, 0)),
                  pl.BlockSpec((BLOCK, 4 * LANES), lambda n: (jnp.maximum(n - 1, 0), 0))],
        out_specs=pl.BlockSpec((BLOCK, SWA_Q), lambda n: (n, 0)),
        out_shape=jax.ShapeDtypeStruct((s, SWA_Q), BF16),
        scratch_shapes=[pltpu.VMEM((SWA_HEADS // 2, 2 * BLOCK, 2 * BLOCK), F32)],
        compiler_params=_params(("arbitrary",)),
        name="swa_attn",
    )(rel_flat, sinks, bkt, qa, kva, kva)


def _fox_kernel(q_ref, k_ref, v_ref, c_ref, ct_ref, o_ref, *, tq):
    p = pl.program_id(0)
    i = pl.program_id(1)
    lane = lax.broadcasted_iota(jnp.int32, (tq, LANES), 1)
    lane8 = lax.broadcasted_iota(jnp.int32, (tq, FOX_HEADS), 1)
    q = q_ref[...]
    cblk = c_ref[...]
    outs = []
    for hh in range(2):
        h = 2 * p + hh
        in_head = (lane >= hh * HEAD_DIM) & (lane < (hh + 1) * HEAD_DIM)
        qm = jnp.where(in_head, q, jnp.zeros_like(q))
        ci = jnp.sum(jnp.where(lane8 == h, cblk, 0.0), axis=1, keepdims=True)

        def scores(j):
            kj = k_ref[pl.ds(pl.multiple_of(j * tq, tq), tq), :]
            cj = ct_ref[pl.ds(h, 1), pl.ds(pl.multiple_of(j * tq, tq), tq)]
            return _dot_nt(qm, kj) + (ci - cj)

        def update(j, s, carry):
            m, l, acc = carry
            vj = v_ref[pl.ds(pl.multiple_of(j * tq, tq), tq), :]
            m_new = jnp.maximum(m, jnp.max(s, axis=1, keepdims=True))
            alpha = jnp.exp(m - m_new)
            e = jnp.exp(s - m_new)
            l = alpha * l + jnp.sum(e, axis=1, keepdims=True)
            acc = alpha * acc + _dot(e.astype(BF16), vj)
            return m_new, l, acc

        def body(j, carry):
            return update(j, scores(j), carry)

        init = (jnp.full((tq, 1), NEG_BIG, F32), jnp.zeros((tq, 1), F32), jnp.zeros((tq, LANES), F32))
        carry = lax.fori_loop(0, i, body, init)
        r = lax.broadcasted_iota(jnp.int32, (tq, tq), 0)
        cc = lax.broadcasted_iota(jnp.int32, (tq, tq), 1)
        s_diag = jnp.where(cc <= r, scores(i), NEG_BIG)
        m, l, acc = update(i, s_diag, carry)
        outs.append(acc / l)
    o_ref[...] = jnp.where(lane < HEAD_DIM, outs[0], outs[1]).astype(BF16)


def _fox(qb, kb, vb, c, ct, tq):
    s = qb.shape[0]
    return pl.pallas_call(
        functools.partial(_fox_kernel, tq=tq),
        grid=(N_PAIRS, s // tq),
        in_specs=[pl.BlockSpec((tq, LANES), lambda p, i: (i, p)),
                  pl.BlockSpec((s, LANES), lambda p, i: (0, p)),
                  pl.BlockSpec((s, LANES), lambda p, i: (0, p)),
                  pl.BlockSpec((tq, FOX_HEADS), lambda p, i: (i, 0)),
                  pl.BlockSpec((FOX_HEADS, s), lambda p, i: (0, 0))],
        out_specs=pl.BlockSpec((tq, LANES), lambda p, i: (i, p)),
        out_shape=jax.ShapeDtypeStruct((s, FOX_W), BF16),
        compiler_params=_params(("arbitrary", "arbitrary")),
        name="fox_attn",
    )(qb, kb, vb, c, ct)


def _rms(x, g):
    return x * lax.rsqrt(jnp.mean(x * x, axis=-1, keepdims=True) + RMS_EPS) * g


def _outproj_kernel(x_ref, ya_ref, yb_ref, sga_ref, sgb_ref, pa_ref, pb_ref, wo_ref, gt_ref, g_ref, sh_ref, sc_ref,
                    wr_ref, br_ref, x1_ref, u2_ref, comb_ref):
    merged = (sga_ref[...].astype(F32) * _dot(ya_ref[...], pa_ref[...])
              + sgb_ref[...].astype(F32) * _dot(yb_ref[...], pb_ref[...]))
    x1 = x_ref[...] + gt_ref[...] * _dot(merged.astype(BF16), wo_ref[...])
    x1_ref[...] = x1
    u2 = _rms(x1, g_ref[...]) * (1.0 + sc_ref[...]) + sh_ref[...]
    u2_ref[...] = u2.astype(BF16)
    logits = jnp.dot(u2, wr_ref[...], preferred_element_type=F32, precision=HIGHEST) + br_ref[...]
    eidx = lax.broadcasted_iota(jnp.int32, logits.shape, 1)
    work = logits
    sel = jnp.zeros(logits.shape, jnp.bool_)
    denom = jnp.zeros((logits.shape[0], 1), F32)
    top = None
    for k in range(TOP_K):
        m = jnp.max(work, axis=-1, keepdims=True)
        if k == 0:
            top = m
        first = jnp.min(jnp.where(work == m, eidx, N_EXPERTS), axis=-1, keepdims=True)
        hit = eidx == first
        sel = sel | hit
        denom = denom + jnp.exp(m - top)
        work = jnp.where(hit, -jnp.inf, work)
    comb_ref[...] = jnp.where(sel, jnp.exp(logits - top) / denom, 0.0)


def _outproj(x2, ya, yb, sga, sgb, pa, pb, wo, gt, g, sh, sc, wr, br, tm):
    s = x2.shape[0]
    row = lambda n: pl.BlockSpec((tm, n), lambda i: (i, 0))
    consts = [pa, pb, wo, gt, g, sh, sc, wr, br]
    return pl.pallas_call(
        _outproj_kernel,
        grid=(s // tm,),
        in_specs=[row(D_MODEL), row(SWA_Q), row(FOX_W), row(D_MODEL), row(D_MODEL)] + [_const_spec(a.shape) for a in consts],
        out_specs=[row(D_MODEL), row(D_MODEL), row(N_EXPERTS)],
        out_shape=[jax.ShapeDtypeStruct((s, D_MODEL), F32), jax.ShapeDtypeStruct((s, D_MODEL), BF16),
                   jax.ShapeDtypeStruct((s, N_EXPERTS), F32)],
        compiler_params=_params(("arbitrary",)),
        name="out_proj_router",
    )(x2, ya, yb, sga, sgb, *consts)


def _moe_kernel(u_ref, comb_ref, x1_ref, w1g_ref, w1l_ref, b1g_ref, b1l_ref, w2_ref, b2_ref, gt_ref, gf_ref,
                o_ref, acc_ref):
    e = pl.program_id(1)

    @pl.when(e == 0)
    def _():
        acc_ref[...] = jnp.zeros_like(acc_ref)

    u = u_ref[...]
    hg = _dot(u, w1g_ref[0]) + b1g_ref[0]
    hl = _dot(u, w1l_ref[0]) + b1l_ref[0]
    glu = jnp.minimum(hg, SWIGLU_LIMIT)
    lin = jnp.clip(hl, -SWIGLU_LIMIT, SWIGLU_LIMIT)
    a = glu * jax.nn.sigmoid(SWIGLU_ALPHA * glu) * (lin + 1.0)
    y = _dot(a.astype(BF16), w2_ref[0]) + b2_ref[0]
    comb = comb_ref[...]
    eidx = lax.broadcasted_iota(jnp.int32, comb.shape, 1)
    ce = jnp.sum(jnp.where(eidx == e, comb, 0.0), axis=1, keepdims=True)
    acc_ref[...] += ce * y

    @pl.when(e == N_EXPERTS - 1)
    def _():
        x2 = x1_ref[...] + gt_ref[...] * acc_ref[...]
        o_ref[...] = _rms(x2, gf_ref[...])


def _moe_dense(u2, comb, x1, w1g, w1l, b1g, b1l, w2, b2, gt, gf, tm):
    s = u2.shape[0]
    row = lambda n: pl.BlockSpec((tm, n), lambda i, e: (i, 0))
    wsp = lambda a: pl.BlockSpec((1,) + a.shape[1:], lambda i, e: (e, 0, 0))
    return pl.pallas_call(
        _moe_kernel,
        grid=(s // tm, N_EXPERTS),
        in_specs=[row(D_MODEL), row(N_EXPERTS), row(D_MODEL), wsp(w1g), wsp(w1l), wsp(b1g), wsp(b1l), wsp(w2), wsp(b2),
                  _const_spec(gt.shape), _const_spec(gf.shape)],
        out_specs=row(D_MODEL),
        out_shape=jax.ShapeDtypeStruct((s, D_MODEL), F32),
        scratch_shapes=[pltpu.VMEM((tm, D_MODEL), F32)],
        compiler_params=_params(("arbitrary", "arbitrary")),
        name="moe_dense",
    )(u2, comb, x1, w1g, w1l, b1g, b1l, w2, b2, gt, gf)


def kernel(x, c, w_ada, b_ada, g_mix, w_in, b_forget, sinks, rel_bias, w_proj_a, w_proj_b, w_out, g_ffn,
           w_router, b_router, w_e1, b_e1, w_e2, b_e2, g_final):
    b, s, d = x.shape
    assert b == 1 and d == D_MODEL and w_ada.shape[0] == 1
    x2 = x.reshape(s, d)
    tm = min(512, s)

    mod = _ada(jnp.broadcast_to(c, (8, d)), w_ada[0], b_ada)[:1]
    sh_m, sc_m, gt_m, sh_f, sc_f, gt_f = [mod[:, k * d:(k + 1) * d] for k in range(N_MOD)]

    w = w_in[0]
    o_ka, o_va, o_b = SWA_Q, SWA_Q + SWA_KV, SWA_Q + 2 * SWA_KV
    o_f = o_b + 3 * FOX_W
    o_g = o_f + FOX_HEADS
    dup = lambda m: jnp.concatenate([m[:, :HEAD_DIM], m[:, :HEAD_DIM], m[:, HEAD_DIM:], m[:, HEAD_DIM:]], axis=1)
    wa = jnp.concatenate([w[:, :SWA_Q], dup(w[:, o_ka:o_va]), dup(w[:, o_va:o_b])], axis=1).astype(BF16)
    wb = w[:, o_b:o_f].astype(BF16)
    wf = w[:, o_f:o_g]
    wg = w[:, o_g:].astype(BF16)
    qa, kva, qb, kb, vb, sga, sgb, cum, cum_t = _inproj(
        x2, g_mix, sh_m, sc_m, wa, wb, wg, wf, wf.T, b_forget, b_forget.reshape(FOX_HEADS, 1), tm)

    ya = _swa(rel_bias.reshape(-1), sinks[0], jnp.asarray(_t5_buckets_np()), qa, kva)
    yb = _fox(qb, kb, vb, cum, cum_t, tm)

    x1, u2, comb = _outproj(x2, ya, yb, sga, sgb, w_proj_a[0].astype(BF16), w_proj_b[0].astype(BF16),
                            w_out[0].astype(BF16), gt_m, g_ffn, sh_f, sc_f, w_router[0], b_router, tm)

    w1 = w_e1[0]
    w1g = w1[:, :, 0::2].astype(BF16)
    w1l = w1[:, :, 1::2].astype(BF16)
    b1g = b_e1[0][:, None, 0::2]
    b1l = b_e1[0][:, None, 1::2]
    out = _moe_dense(u2, comb, x1, w1g, w1l, b1g, b1l, w_e2[0].astype(BF16), b_e2[0][:, None, :],
                     gt_f, g_final.reshape(1, d), tm)
    return out.reshape(b, s, d)
```

```python
import functools
import math

import numpy as np
import jax
import jax.numpy as jnp
from jax import lax
from jax.experimental import pallas as pl
from jax.experimental.pallas import tpu as pltpu

D_MODEL = 1024
HEAD_DIM = 64
SWA_HEADS = 8
SWA_KV_HEADS = 2
WINDOW = 128
FOX_HEADS = 8
BLOCK = 128
REL_BUCKETS = 32
REL_MAX_DIST = WINDOW
N_EXPERTS = 32
TOP_K = 4
D_FF = D_MODEL
SWIGLU_LIMIT = 7.0
SWIGLU_ALPHA = 1.702
RMS_EPS = 1e-5
N_MOD = 6

SWA_Q = SWA_HEADS * HEAD_DIM
SWA_KV = SWA_KV_HEADS * HEAD_DIM
FOX_W = FOX_HEADS * HEAD_DIM
LANES = 128
N_PAIRS = FOX_HEADS // 2
NEG_BIG = -1e30
LOG2E = math.log2(math.e)
FOX_TILE = 256
N_SPLIT = 3
SKIP_LOG2 = 127.0
NORM_SLACK = 1.01
VMEM_LIMIT = 56 * 1024 * 1024

F32 = jnp.float32
BF16 = jnp.bfloat16
HIGHEST = lax.Precision.HIGHEST


def _dot(a, b):
    return jnp.dot(a, b, preferred_element_type=F32)


def _dot_nt(a, b, precision=None):
    return lax.dot_general(a, b, (((1,), (1,)), ((), ())), preferred_element_type=F32, precision=precision)


def _const_spec(shape):
    nd = len(shape)
    return pl.BlockSpec(shape, lambda *_: (0,) * nd)


def _params(sem):
    return pltpu.CompilerParams(dimension_semantics=sem, vmem_limit_bytes=VMEM_LIMIT)


def _ada_kernel(c_ref, w_ref, b_ref, o_ref):
    c = c_ref[...]
    act = c * jax.nn.sigmoid(c)
    o_ref[...] = jnp.dot(act, w_ref[...], preferred_element_type=F32, precision=HIGHEST) + b_ref[...]


def _ada(c8, w_ada, b_ada):
    n = w_ada.shape[1]
    tn = 1024
    return pl.pallas_call(
        _ada_kernel,
        grid=(n // tn,),
        in_specs=[_const_spec((8, D_MODEL)),
                  pl.BlockSpec((D_MODEL, tn), lambda j: (0, j)),
                  pl.BlockSpec((1, tn), lambda j: (0, j))],
        out_specs=pl.BlockSpec((8, tn), lambda j: (0, j)),
        out_shape=jax.ShapeDtypeStruct((8, n), F32),
        compiler_params=_params(("arbitrary",)),
        name="ada_mod",
    )(c8, w_ada, b_ada)


def _inproj_kernel(x_ref, g_ref, sh_ref, sc_ref, wa_ref, wb_ref, wvt_ref, wg_ref, wf_ref, bf_ref, hind_ref, place_ref,
                   qa_ref, kva_ref, qb_ref, kb_ref, vbt_ref, sga_ref, sgb_ref, c_ref, ca_ref, qn_ref, kn_ref,
                   carry_ref):
    i = pl.program_id(0)
    tm = x_ref.shape[0]
    nsub = tm // FOX_TILE

    @pl.when(i == 0)
    def _():
        carry_ref[...] = jnp.zeros_like(carry_ref)

    xf = x_ref[...]
    ms = jnp.mean(xf * xf, axis=-1, keepdims=True)
    y = xf * lax.rsqrt(ms + RMS_EPS) * g_ref[...]
    u = y * (1.0 + sc_ref[...]) + sh_ref[...]
    ub = u.astype(BF16)

    za = _dot(ub, wa_ref[...])
    qa_ref[...] = (za[:, :SWA_Q] * (HEAD_DIM ** -0.5)).astype(BF16)
    kva_ref[...] = za[:, SWA_Q:].astype(BF16)

    zb = _dot(ub, wb_ref[...])
    qb = (zb[:, :FOX_W] * (HEAD_DIM ** -0.5 * LOG2E)).astype(BF16)
    kb = zb[:, FOX_W:].astype(BF16)
    qb_ref[...] = qb
    kb_ref[...] = kb
    vbt_ref[...] = _dot_nt(wvt_ref[...], ub).astype(BF16)

    def tile_norm_max(z, o_ref):
        zf = z.astype(F32)
        n2 = jnp.dot(zf * zf, hind_ref[...], preferred_element_type=F32, precision=HIGHEST)
        for sb in range(nsub):
            o_ref[sb] = jnp.max(n2[sb * FOX_TILE:(sb + 1) * FOX_TILE], axis=0, keepdims=True)

    tile_norm_max(qb, qn_ref)
    tile_norm_max(kb, kn_ref)

    zg = _dot(ub, wg_ref[...])
    sg = jax.nn.sigmoid(zg)
    sga_ref[...] = sg[:, :D_MODEL].astype(BF16)
    sgb_ref[...] = sg[:, D_MODEL:].astype(BF16)

    fb = jnp.dot(u, wf_ref[...], preferred_element_type=F32, precision=HIGHEST) + bf_ref[...]
    lf = jnp.minimum(fb, 0.0) - jnp.log1p(jnp.exp(-jnp.abs(fb)))
    r = lax.broadcasted_iota(jnp.int32, (tm, tm), 0)
    cc = lax.broadcasted_iota(jnp.int32, (tm, tm), 1)
    lower = (cc <= r).astype(F32)
    cg = jnp.dot(lower, lf, preferred_element_type=F32, precision=HIGHEST) + carry_ref[...]
    c_ref[...] = cg
    carry_ref[...] = cg[tm - 1:tm, :]
    same_tile = jnp.bitwise_xor(cc, r) < FOX_TILE
    lower_local = ((cc <= r) & same_tile).astype(F32)
    x3 = jnp.dot(lower_local, lf, preferred_element_type=F32, precision=HIGHEST) * LOG2E
    aug = jnp.zeros((tm, N_PAIRS * LANES), F32)
    for k in range(N_SPLIT):
        part = x3.astype(BF16)
        x3 = x3 - part.astype(F32)
        aug = aug + _dot(part, place_ref[k])
    ca_ref[...] = aug.astype(BF16)


def _inproj(x2, g, sh, sc, wa, wb, wvt, wg, wf, bfor, tm):
    s = x2.shape[0]
    nsub = tm // FOX_TILE
    row = lambda n: pl.BlockSpec((tm, n), lambda i: (i, 0))
    hind = np.zeros((FOX_W, FOX_HEADS), np.float32)
    hind[np.arange(FOX_W), np.arange(FOX_W) // HEAD_DIM] = 1.0
    place = np.zeros((N_SPLIT, FOX_HEADS, N_PAIRS * LANES), np.float32)
    for k in range(N_SPLIT):
        for h in range(FOX_HEADS):
            place[k, h, (h // 2) * LANES + N_SPLIT * (h % 2) + k] = 1.0
    hind = jnp.asarray(hind)
    place = jnp.asarray(place, BF16)
    out_shape = [
        jax.ShapeDtypeStruct((s, SWA_Q), BF16),
        jax.ShapeDtypeStruct((s, 4 * LANES), BF16),
        jax.ShapeDtypeStruct((s, FOX_W), BF16),
        jax.ShapeDtypeStruct((s, FOX_W), BF16),
        jax.ShapeDtypeStruct((FOX_W, s), BF16),
        jax.ShapeDtypeStruct((s, D_MODEL), BF16),
        jax.ShapeDtypeStruct((s, D_MODEL), BF16),
        jax.ShapeDtypeStruct((s, FOX_HEADS), F32),
        jax.ShapeDtypeStruct((s, N_PAIRS * LANES), BF16),
        jax.ShapeDtypeStruct((s // FOX_TILE, 1, FOX_HEADS), F32),
        jax.ShapeDtypeStruct((s // FOX_TILE, 1, FOX_HEADS), F32),
    ]
    stat = pl.BlockSpec((nsub, 1, FOX_HEADS), lambda i: (i, 0, 0))
    out_specs = [row(SWA_Q), row(4 * LANES), row(FOX_W), row(FOX_W), pl.BlockSpec((FOX_W, tm), lambda i: (0, i)),
                 row(D_MODEL), row(D_MODEL), row(FOX_HEADS), row(N_PAIRS * LANES), stat, stat]
    consts = [g, sh, sc, wa, wb, wvt, wg, wf, bfor, hind, place]
    return pl.pallas_call(
        _inproj_kernel,
        grid=(s // tm,),
        in_specs=[row(D_MODEL)] + [_const_spec(a.shape) for a in consts],
        out_specs=out_specs,
        out_shape=out_shape,
        scratch_shapes=[pltpu.VMEM((1, FOX_HEADS), F32)],
        compiler_params=_params(("arbitrary",)),
        name="in_proj",
    )(x2, *consts)


def _t5_buckets_np():
    qi = np.arange(BLOCK)[:, None]
    kj = np.arange(2 * BLOCK)[None, :]
    dist = BLOCK + qi - kj
    n = np.maximum(dist, 0)
    max_exact = REL_BUCKETS // 2
    nf = np.maximum(n, 1).astype(np.float32)
    large = max_exact + (np.log(nf / np.float32(max_exact)) / np.float32(math.log(REL_MAX_DIST / max_exact))
                         * np.float32(REL_BUCKETS - max_exact)).astype(np.int32)
    large = np.minimum(large, REL_BUCKETS - 1)
    bucket = np.where(n < max_exact, n, large).astype(np.int32)
    band = (dist >= 0) & (dist < WINDOW)
    return np.where(band, bucket, -1).astype(np.int32)


def _swa_kernel(rel_ref, sink_ref, bkt_ref, q_ref, kvc_ref, kvp_ref, o_ref, bias_ref):
    n = pl.program_id(0)

    @pl.when(n == 0)
    def _():
        bkt = bkt_ref[...]
        for h in range(SWA_HEADS):
            acc = jnp.full(bkt.shape, NEG_BIG, F32)
            for b in range(REL_BUCKETS):
                acc = jnp.where(bkt == b, rel_ref[b * SWA_HEADS + h], acc)
            bias_ref[h // 2, (h % 2) * BLOCK:(h % 2 + 1) * BLOCK, :] = acc

    lane = lax.broadcasted_iota(jnp.int32, (BLOCK, LANES), 1)
    row2 = lax.broadcasted_iota(jnp.int32, (2 * BLOCK, 1), 0)
    col = lax.broadcasted_iota(jnp.int32, (2 * BLOCK, 2 * BLOCK), 1)
    key_ok = col >= jnp.where(n > 0, 0, BLOCK)
    kv = jnp.concatenate([kvp_ref[...], kvc_ref[...]], axis=0)
    for p in range(SWA_HEADS // 2):
        g = p // 2
        qp = q_ref[:, p * LANES:(p + 1) * LANES]
        zero = jnp.zeros_like(qp)
        qs = jnp.concatenate([jnp.where(lane < HEAD_DIM, qp, zero), jnp.where(lane >= HEAD_DIM, qp, zero)], axis=0)
        kd = kv[:, g * LANES:(g + 1) * LANES]
        vd = kv[:, (2 + g) * LANES:(3 + g) * LANES]
        s = _dot_nt(qs, kd) + bias_ref[p]
        s = jnp.where(key_ok, s, NEG_BIG)
        sink = jnp.where(row2 < BLOCK, sink_ref[2 * p], sink_ref[2 * p + 1])
        m = jnp.maximum(jnp.max(s, axis=-1, keepdims=True), sink)
        e = jnp.exp(s - m)
        denom = jnp.sum(e, axis=-1, keepdims=True) + jnp.exp(sink - m)
        o = _dot(e.astype(BF16), vd) / denom
        o_ref[:, p * LANES:(p + 1) * LANES] = jnp.where(lane < HEAD_DIM, o[:BLOCK], o[BLOCK:]).astype(BF16)


def _swa(rel_flat, sinks, bkt, qa, kva):
    s = qa.shape[0]
    nb = s // BLOCK
    smem = pl.BlockSpec(memory_space=pltpu.SMEM)
    return pl.pallas_call(
        _swa_kernel,
        grid=(nb,),
        in_specs=[smem, smem, _const_spec(bkt.shape),
                  pl.BlockSpec((BLOCK, SWA_Q), lambda n: (n, 0)),
                  pl.BlockSpec((BLOCK, 4 * LANES), lambda n: (n, 0)),
                  pl.BlockSpec((BLOCK, 4 * LANES), lambda n: (jnp.maximum(n - 1, 0), 0))],
        out_specs=pl.BlockSpec((BLOCK, SWA_Q), lambda n: (n, 0)),
        out_shape=jax.ShapeDtypeStruct((s, SWA_Q), BF16),
        scratch_shapes=[pltpu.VMEM((SWA_HEADS // 2, 2 * BLOCK, 2 * BLOCK), F32)],
        compiler_params=_params(("arbitrary",)),
        name="swa_attn",
    )(rel_flat, sinks, bkt, qa, kva, kva)


def _fox_kernel(jlo_ref, base_ref, q_ref, k_ref, ca_ref, vt_ref, o_ref):
    p = pl.program_id(0)
    i = pl.program_id(1)
    t = FOX_TILE
    lane = lax.broadcasted_iota(jnp.int32, (t, LANES), 1)
    q = q_ref[...]
    wq = []
    for hh in range(2):
        in_head = (lane >= hh * HEAD_DIM) & (lane < (hh + 1) * HEAD_DIM)
        qm = jnp.where(in_head, q, jnp.zeros_like(q))
        sel = (lane >= N_SPLIT * hh) & (lane < N_SPLIT * (hh + 1))
        aug = jnp.where(sel, -1.0, 0.0).astype(BF16)
        wq.append(jnp.concatenate([qm, aug], axis=1))
    key = lax.broadcasted_iota(jnp.int32, (t, t), 0)
    qry = lax.broadcasted_iota(jnp.int32, (t, t), 1)
    causal = key <= qry

    def step(j, carry, heads, nsub, diagonal=False):
        start = pl.multiple_of(j * t, t)
        rows = nsub * t
        lhs = jnp.concatenate([k_ref[pl.ds(start, rows), :], ca_ref[pl.ds(start, rows), :]], axis=1)
        scores = [_dot_nt(lhs, wq[hh]) for hh in heads]
        mid = []
        for hh, s, (m, l, acc) in zip(heads, scores, carry):
            h = 2 * p + hh
            parts = [s[k * t:(k + 1) * t] for k in range(nsub)]
            if diagonal:
                parts[-1] = jnp.where(causal, parts[-1], NEG_BIG)
            djs = [base_ref[h, i] - base_ref[h, j + k] for k in range(nsub)]
            m_new = m
            for part, dj in zip(parts, djs):
                m_new = jnp.maximum(m_new, jnp.max(part, axis=0, keepdims=True) + dj)
            alpha = jnp.exp2(m - m_new)
            es = [jnp.exp2(part + (dj - m_new)) for part, dj in zip(parts, djs)]
            l = alpha * l
            for e in es:
                l = l + jnp.sum(e, axis=0, keepdims=True)
            e_all = es[0] if nsub == 1 else jnp.concatenate(es, axis=0)
            mid.append((m_new, l, alpha, acc, e_all.astype(BF16)))
        new = []
        for hh, (m_new, l, alpha, acc, e_all) in zip(heads, mid):
            vt = vt_ref[hh * HEAD_DIM:(hh + 1) * HEAD_DIM, pl.ds(start, rows)]
            new.append((m_new, l, alpha * acc + _dot(vt, e_all)))
        return tuple(new)

    def run(lo, hi, carry, heads):
        odd = (hi - lo) & 1
        carry = lax.fori_loop(0, odd, lambda _, c: step(lo, c, heads, 1), carry)
        lo2 = lo + odd
        return lax.fori_loop(0, (hi - lo2) // 2, lambda n, c: step(lo2 + 2 * n, c, heads, 2), carry)

    lo0 = jlo_ref[2 * p, i]
    lo1 = jlo_ref[2 * p + 1, i]
    lo_both = jnp.maximum(lo0, lo1)
    init = (jnp.full((1, t), NEG_BIG, F32), jnp.zeros((1, t), F32), jnp.zeros((HEAD_DIM, t), F32))
    (c0,) = run(lo0, lo_both, (init,), (0,))
    (c1,) = run(lo1, lo_both, (init,), (1,))
    carry = run(lo_both, i, (c0, c1), (0, 1))
    carry = step(i, carry, (0, 1), 1, diagonal=True)
    ot = jnp.concatenate([carry[0][2] / carry[0][1], carry[1][2] / carry[1][1]], axis=0)
    o_ref[...] = ot.T.astype(BF16)


def _fox_schedule(cum, qn2, kn2):
    nt = qn2.shape[0]
    t = FOX_TILE
    bq = jnp.sqrt(qn2.reshape(nt, FOX_HEADS)) * NORM_SLACK
    bk = jnp.sqrt(kn2.reshape(nt, FOX_HEADS)) * NORM_SLACK
    c2 = cum.reshape(nt, t, FOX_HEADS) * LOG2E
    c_first, c_last = c2[:, 0, :], c2[:, t - 1, :]
    upper = (bq[:, None, :] * bk[None, :, :] + (bq * bk)[:, None, :] + c_first[:, None, :] - c_last[None, :, :])
    ii = jnp.arange(nt)[:, None, None]
    jj = jnp.arange(nt)[None, :, None]
    needed = (jj <= ii) & ((upper >= -SKIP_LOG2) | (jj == ii))
    jlo = jnp.min(jnp.where(needed, jj, nt), axis=1).T.astype(jnp.int32)
    base = jnp.concatenate([jnp.zeros((1, FOX_HEADS), F32), c_last[:-1]], axis=0).T
    return jlo, base


def _fox(jlo, base, qb, kb, ca, vbt):
    s = qb.shape[0]
    t = FOX_TILE
    grid_spec = pltpu.PrefetchScalarGridSpec(
        num_scalar_prefetch=2,
        grid=(N_PAIRS, s // t),
        in_specs=[pl.BlockSpec((t, LANES), lambda p, i, *_: (i, p)),
                  pl.BlockSpec((s, LANES), lambda p, i, *_: (0, p)),
                  pl.BlockSpec((s, LANES), lambda p, i, *_: (0, p)),
                  pl.BlockSpec((LANES, s), lambda p, i, *_: (p, 0))],
        out_specs=pl.BlockSpec((t, LANES), lambda p, i, *_: (i, p)),
    )
    return pl.pallas_call(
        _fox_kernel,
        grid_spec=grid_spec,
        out_shape=jax.ShapeDtypeStruct((s, FOX_W), BF16),
        compiler_params=_params(("arbitrary", "arbitrary")),
        name="fox_attn",
    )(jlo, base, qb, kb, ca, vbt)


def _rms(x, g):
    return x * lax.rsqrt(jnp.mean(x * x, axis=-1, keepdims=True) + RMS_EPS) * g


def _outproj_kernel(x_ref, ya_ref, yb_ref, sga_ref, sgb_ref, pa_ref, pb_ref, wo_ref, gt_ref, g_ref, sh_ref, sc_ref,
                    wr_ref, br_ref, x1_ref, u2_ref, comb_ref):
    merged = (sga_ref[...].astype(F32) * _dot(ya_ref[...], pa_ref[...])
              + sgb_ref[...].astype(F32) * _dot(yb_ref[...], pb_ref[...]))
    x1 = x_ref[...] + gt_ref[...] * _dot(merged.astype(BF16), wo_ref[...])
    x1_ref[...] = x1
    u2 = _rms(x1, g_ref[...]) * (1.0 + sc_ref[...]) + sh_ref[...]
    u2_ref[...] = u2.astype(BF16)
    logits = jnp.dot(u2, wr_ref[...], preferred_element_type=F32, precision=HIGHEST) + br_ref[...]
    eidx = lax.broadcasted_iota(jnp.int32, logits.shape, 1)
    work = logits
    sel = jnp.zeros(logits.shape, jnp.bool_)
    denom = jnp.zeros((logits.shape[0], 1), F32)
    top = None
    for k in range(TOP_K):
        m = jnp.max(work, axis=-1, keepdims=True)
        if k == 0:
            top = m
        first = jnp.min(jnp.where(work == m, eidx, N_EXPERTS), axis=-1, keepdims=True)
        hit = eidx == first
        sel = sel | hit
        denom = denom + jnp.exp(m - top)
        work = jnp.where(hit, -jnp.inf, work)
    comb_ref[...] = jnp.where(sel, jnp.exp(logits - top) / denom, 0.0)


def _outproj(x2, ya, yb, sga, sgb, pa, pb, wo, gt, g, sh, sc, wr, br, tm):
    s = x2.shape[0]
    row = lambda n: pl.BlockSpec((tm, n), lambda i: (i, 0))
    consts = [pa, pb, wo, gt, g, sh, sc, wr, br]
    return pl.pallas_call(
        _outproj_kernel,
        grid=(s // tm,),
        in_specs=[row(D_MODEL), row(SWA_Q), row(FOX_W), row(D_MODEL), row(D_MODEL)] + [_const_spec(a.shape) for a in consts],
        out_specs=[row(D_MODEL), row(D_MODEL), row(N_EXPERTS)],
        out_shape=[jax.ShapeDtypeStruct((s, D_MODEL), F32), jax.ShapeDtypeStruct((s, D_MODEL), BF16),
                   jax.ShapeDtypeStruct((s, N_EXPERTS), F32)],
        compiler_params=_params(("arbitrary",)),
        name="out_proj_router",
    )(x2, ya, yb, sga, sgb, *consts)


def _moe_kernel(u_ref, comb_ref, x1_ref, w1g_ref, w1l_ref, b1g_ref, b1l_ref, w2_ref, b2_ref, gt_ref, gf_ref,
                o_ref, acc_ref):
    e = pl.program_id(1)

    @pl.when(e == 0)
    def _():
        acc_ref[...] = jnp.zeros_like(acc_ref)

    u = u_ref[...]
    hg = _dot_nt(u, w1g_ref[0]) + b1g_ref[0]
    hl = _dot_nt(u, w1l_ref[0]) + b1l_ref[0]
    glu = jnp.minimum(hg, SWIGLU_LIMIT)
    lin = jnp.clip(hl, -SWIGLU_LIMIT, SWIGLU_LIMIT)
    a = glu * jax.nn.sigmoid(SWIGLU_ALPHA * glu) * (lin + 1.0)
    y = _dot(a.astype(BF16), w2_ref[0]) + b2_ref[0]
    comb = comb_ref[...]
    eidx = lax.broadcasted_iota(jnp.int32, comb.shape, 1)
    ce = jnp.sum(jnp.where(eidx == e, comb, 0.0), axis=1, keepdims=True)
    acc_ref[...] += ce * y

    @pl.when(e == N_EXPERTS - 1)
    def _():
        x2 = x1_ref[...] + gt_ref[...] * acc_ref[...]
        o_ref[...] = _rms(x2, gf_ref[...])


def _moe_dense(u2, comb, x1, w1t, b1g, b1l, w2, b2, gt, gf, tm):
    s = u2.shape[0]
    row = lambda n: pl.BlockSpec((tm, n), lambda i, e: (i, 0))
    wsp = lambda a: pl.BlockSpec((1,) + a.shape[1:], lambda i, e: (e, 0, 0))
    w1g_spec = pl.BlockSpec((1, D_FF, D_MODEL), lambda i, e: (e, 0, 0))
    w1l_spec = pl.BlockSpec((1, D_FF, D_MODEL), lambda i, e: (e, 0, 1))
    return pl.pallas_call(
        _moe_kernel,
        grid=(s // tm, N_EXPERTS),
        in_specs=[row(D_MODEL), row(N_EXPERTS), row(D_MODEL), w1g_spec, w1l_spec, wsp(b1g), wsp(b1l), wsp(w2), wsp(b2),
                  _const_spec(gt.shape), _const_spec(gf.shape)],
        out_specs=row(D_MODEL),
        out_shape=jax.ShapeDtypeStruct((s, D_MODEL), F32),
        scratch_shapes=[pltpu.VMEM((tm, D_MODEL), F32)],
        compiler_params=_params(("arbitrary", "arbitrary")),
        name="moe_dense",
    )(u2, comb, x1, w1t, w1t, b1g, b1l, w2, b2, gt, gf)


def kernel(x, c, w_ada, b_ada, g_mix, w_in, b_forget, sinks, rel_bias, w_proj_a, w_proj_b, w_out, g_ffn,
           w_router, b_router, w_e1, b_e1, w_e2, b_e2, g_final):
    b, s, d = x.shape
    assert b == 1 and d == D_MODEL and w_ada.shape[0] == 1
    x2 = x.reshape(s, d)
    tm = min(512, s)

    mod = _ada(jnp.broadcast_to(c, (8, d)), w_ada[0], b_ada)[:1]
    sh_m, sc_m, gt_m, sh_f, sc_f, gt_f = [mod[:, k * d:(k + 1) * d] for k in range(N_MOD)]

    w = w_in[0]
    o_ka, o_va, o_b = SWA_Q, SWA_Q + SWA_KV, SWA_Q + 2 * SWA_KV
    o_f = o_b + 3 * FOX_W
    o_g = o_f + FOX_HEADS
    dup = lambda m: jnp.concatenate([m[:, :HEAD_DIM], m[:, :HEAD_DIM], m[:, HEAD_DIM:], m[:, HEAD_DIM:]], axis=1)
    wa = jnp.concatenate([w[:, :SWA_Q], dup(w[:, o_ka:o_va]), dup(w[:, o_va:o_b])], axis=1).astype(BF16)
    wb = w[:, o_b:o_b + 2 * FOX_W].astype(BF16)
    wvt = w[:, o_b + 2 * FOX_W:o_f].T.astype(BF16)
    wf = w[:, o_f:o_g]
    wg = w[:, o_g:].astype(BF16)
    qa, kva, qb, kb, vbt, sga, sgb, cum, ca, qn2, kn2 = _inproj(
        x2, g_mix, sh_m, sc_m, wa, wb, wvt, wg, wf, b_forget, tm)

    ya = _swa(rel_bias.reshape(-1), sinks[0], jnp.asarray(_t5_buckets_np()), qa, kva)
    jlo, base = _fox_schedule(cum, qn2, kn2)
    yb = _fox(jlo, base, qb, kb, ca, vbt)

    x1, u2, comb = _outproj(x2, ya, yb, sga, sgb, w_proj_a[0].astype(BF16), w_proj_b[0].astype(BF16),
                            w_out[0].astype(BF16), gt_m, g_ffn, sh_f, sc_f, w_router[0], b_router, tm)

    w1t = jnp.swapaxes(w_e1[0], 1, 2).astype(BF16).reshape(N_EXPERTS, D_FF, 2 * D_MODEL)
    b1 = b_e1[0].reshape(N_EXPERTS, D_FF, 2)
    b1g = b1[:, None, :, 0]
    b1l = b1[:, None, :, 1]
    out = _moe_dense(u2, comb, x1, w1t, b1g, b1l, w_e2[0].astype(BF16), b_e2[0][:, None, :],
                     gt_f, g_final.reshape(1, d), tm)
    return out.reshape(b, s, d)
```

```python
import functools
import math

import numpy as np
import jax
import jax.numpy as jnp
from jax import lax
from jax.experimental import pallas as pl
from jax.experimental.pallas import tpu as pltpu
from jax.experimental.pallas import tpu_sc as plsc

D_MODEL = 1024
HEAD_DIM = 64
SWA_HEADS = 8
SWA_KV_HEADS = 2
WINDOW = 128
FOX_HEADS = 8
BLOCK = 128
REL_BUCKETS = 32
REL_MAX_DIST = WINDOW
N_EXPERTS = 32
TOP_K = 4
D_FF = D_MODEL
SWIGLU_LIMIT = 7.0
SWIGLU_ALPHA = 1.702
RMS_EPS = 1e-5
N_MOD = 6

SWA_Q = SWA_HEADS * HEAD_DIM
SWA_KV = SWA_KV_HEADS * HEAD_DIM
FOX_W = FOX_HEADS * HEAD_DIM
LANES = 128
N_PAIRS = FOX_HEADS // 2
NEG_BIG = -1e30
LOG2E = math.log2(math.e)
FOX_TILE = 256
N_SPLIT = 3
SKIP_LOG2 = 127.0
NORM_SLACK = 1.01
PACK_W = 256
MOE_TILE = 512
SC_WINDOW = 128
VMEM_LIMIT = 56 * 1024 * 1024

F32 = jnp.float32
BF16 = jnp.bfloat16
HIGHEST = lax.Precision.HIGHEST


def _dot(a, b):
    return jnp.dot(a, b, preferred_element_type=F32)


def _dot_nt(a, b, precision=None):
    return lax.dot_general(a, b, (((1,), (1,)), ((), ())), preferred_element_type=F32, precision=precision)


def _const_spec(shape):
    nd = len(shape)
    return pl.BlockSpec(shape, lambda *_: (0,) * nd)


def _params(sem):
    return pltpu.CompilerParams(dimension_semantics=sem, vmem_limit_bytes=VMEM_LIMIT)


def _ada_kernel(c_ref, w_ref, b_ref, o_ref):
    c = c_ref[...]
    act = c * jax.nn.sigmoid(c)
    o_ref[...] = jnp.dot(act, w_ref[...], preferred_element_type=F32, precision=HIGHEST) + b_ref[...]


def _ada(c8, w_ada, b_ada):
    n = w_ada.shape[1]
    tn = 1024
    return pl.pallas_call(
        _ada_kernel,
        grid=(n // tn,),
        in_specs=[_const_spec((8, D_MODEL)),
                  pl.BlockSpec((D_MODEL, tn), lambda j: (0, j)),
                  pl.BlockSpec((1, tn), lambda j: (0, j))],
        out_specs=pl.BlockSpec((8, tn), lambda j: (0, j)),
        out_shape=jax.ShapeDtypeStruct((8, n), F32),
        compiler_params=_params(("arbitrary",)),
        name="ada_mod",
    )(c8, w_ada, b_ada)


def _inproj_kernel(x_ref, g_ref, sh_ref, sc_ref, wa_ref, wb_ref, wvt_ref, wg_ref, wf_ref, bf_ref, hind_ref, place_ref,
                   qa_ref, kva_ref, qb_ref, kb_ref, vbt_ref, sga_ref, sgb_ref, c_ref, ca_ref, qn_ref, kn_ref,
                   carry_ref):
    i = pl.program_id(0)
    tm = x_ref.shape[0]
    nsub = tm // FOX_TILE

    @pl.when(i == 0)
    def _():
        carry_ref[...] = jnp.zeros_like(carry_ref)

    xf = x_ref[...]
    ms = jnp.mean(xf * xf, axis=-1, keepdims=True)
    y = xf * lax.rsqrt(ms + RMS_EPS) * g_ref[...]
    u = y * (1.0 + sc_ref[...]) + sh_ref[...]
    ub = u.astype(BF16)

    za = _dot(ub, wa_ref[...])
    qa_ref[...] = (za[:, :SWA_Q] * (HEAD_DIM ** -0.5)).astype(BF16)
    kva_ref[...] = za[:, SWA_Q:].astype(BF16)

    zb = _dot(ub, wb_ref[...])
    qb = (zb[:, :FOX_W] * (HEAD_DIM ** -0.5 * LOG2E)).astype(BF16)
    kb = zb[:, FOX_W:].astype(BF16)
    qb_ref[...] = qb
    kb_ref[...] = kb
    vbt_ref[...] = _dot_nt(wvt_ref[...], ub).astype(BF16)

    def tile_norm_max(z, o_ref):
        zf = z.astype(F32)
        n2 = jnp.dot(zf * zf, hind_ref[...], preferred_element_type=F32, precision=HIGHEST)
        for sb in range(nsub):
            o_ref[sb] = jnp.max(n2[sb * FOX_TILE:(sb + 1) * FOX_TILE], axis=0, keepdims=True)

    tile_norm_max(qb, qn_ref)
    tile_norm_max(kb, kn_ref)

    zg = _dot(ub, wg_ref[...])
    sg = jax.nn.sigmoid(zg)
    sga_ref[...] = sg[:, :D_MODEL].astype(BF16)
    sgb_ref[...] = sg[:, D_MODEL:].astype(BF16)

    fb = jnp.dot(u, wf_ref[...], preferred_element_type=F32, precision=HIGHEST) + bf_ref[...]
    lf = jnp.minimum(fb, 0.0) - jnp.log1p(jnp.exp(-jnp.abs(fb)))
    r = lax.broadcasted_iota(jnp.int32, (tm, tm), 0)
    cc = lax.broadcasted_iota(jnp.int32, (tm, tm), 1)
    lower = (cc <= r).astype(F32)
    cg = jnp.dot(lower, lf, preferred_element_type=F32, precision=HIGHEST) + carry_ref[...]
    c_ref[...] = cg
    carry_ref[...] = cg[tm - 1:tm, :]
    same_tile = jnp.bitwise_xor(cc, r) < FOX_TILE
    lower_local = ((cc <= r) & same_tile).astype(F32)
    x3 = jnp.dot(lower_local, lf, preferred_element_type=F32, precision=HIGHEST) * LOG2E
    aug = jnp.zeros((tm, N_PAIRS * LANES), F32)
    for k in range(N_SPLIT):
        part = x3.astype(BF16)
        x3 = x3 - part.astype(F32)
        aug = aug + _dot(part, place_ref[k])
    ca_ref[...] = aug.astype(BF16)


def _inproj(x2, g, sh, sc, wa, wb, wvt, wg, wf, bfor, tm):
    s = x2.shape[0]
    nsub = tm // FOX_TILE
    row = lambda n: pl.BlockSpec((tm, n), lambda i: (i, 0))
    hind = np.zeros((FOX_W, FOX_HEADS), np.float32)
    hind[np.arange(FOX_W), np.arange(FOX_W) // HEAD_DIM] = 1.0
    place = np.zeros((N_SPLIT, FOX_HEADS, N_PAIRS * LANES), np.float32)
    for k in range(N_SPLIT):
        for h in range(FOX_HEADS):
            place[k, h, (h // 2) * LANES + N_SPLIT * (h % 2) + k] = 1.0
    hind = jnp.asarray(hind)
    place = jnp.asarray(place, BF16)
    out_shape = [
        jax.ShapeDtypeStruct((s, SWA_Q), BF16),
        jax.ShapeDtypeStruct((s, 4 * LANES), BF16),
        jax.ShapeDtypeStruct((s, FOX_W), BF16),
        jax.ShapeDtypeStruct((s, FOX_W), BF16),
        jax.ShapeDtypeStruct((FOX_W, s), BF16),
        jax.ShapeDtypeStruct((s, D_MODEL), BF16),
        jax.ShapeDtypeStruct((s, D_MODEL), BF16),
        jax.ShapeDtypeStruct((s, FOX_HEADS), F32),
        jax.ShapeDtypeStruct((s, N_PAIRS * LANES), BF16),
        jax.ShapeDtypeStruct((s // FOX_TILE, 1, FOX_HEADS), F32),
        jax.ShapeDtypeStruct((s // FOX_TILE, 1, FOX_HEADS), F32),
    ]
    stat = pl.BlockSpec((nsub, 1, FOX_HEADS), lambda i: (i, 0, 0))
    out_specs = [row(SWA_Q), row(4 * LANES), row(FOX_W), row(FOX_W), pl.BlockSpec((FOX_W, tm), lambda i: (0, i)),
                 row(D_MODEL), row(D_MODEL), row(FOX_HEADS), row(N_PAIRS * LANES), stat, stat]
    consts = [g, sh, sc, wa, wb, wvt, wg, wf, bfor, hind, place]
    return pl.pallas_call(
        _inproj_kernel,
        grid=(s // tm,),
        in_specs=[row(D_MODEL)] + [_const_spec(a.shape) for a in consts],
        out_specs=out_specs,
        out_shape=out_shape,
        scratch_shapes=[pltpu.VMEM((1, FOX_HEADS), F32)],
        compiler_params=_params(("arbitrary",)),
        name="in_proj",
    )(x2, *consts)


def _t5_buckets_np():
    qi = np.arange(BLOCK)[:, None]
    kj = np.arange(2 * BLOCK)[None, :]
    dist = BLOCK + qi - kj
    n = np.maximum(dist, 0)
    max_exact = REL_BUCKETS // 2
    nf = np.maximum(n, 1).astype(np.float32)
    large = max_exact + (np.log(nf / np.float32(max_exact)) / np.float32(math.log(REL_MAX_DIST / max_exact))
                         * np.float32(REL_BUCKETS - max_exact)).astype(np.int32)
    large = np.minimum(large, REL_BUCKETS - 1)
    bucket = np.where(n < max_exact, n, large).astype(np.int32)
    band = (dist >= 0) & (dist < WINDOW)
    return np.where(band, bucket, -1).astype(np.int32)


def _swa_kernel(rel_ref, sink_ref, bkt_ref, q_ref, kvc_ref, kvp_ref, o_ref, bias_ref):
    n = pl.program_id(0)

    @pl.when(n == 0)
    def _():
        bkt = bkt_ref[...]
        for h in range(SWA_HEADS):
            acc = jnp.full(bkt.shape, NEG_BIG, F32)
            for b in range(REL_BUCKETS):
                acc = jnp.where(bkt == b, rel_ref[b * SWA_HEADS + h], acc)
            bias_ref[h // 2, (h % 2) * BLOCK:(h % 2 + 1) * BLOCK, :] = acc

    lane = lax.broadcasted_iota(jnp.int32, (BLOCK, LANES), 1)
    row2 = lax.broadcasted_iota(jnp.int32, (2 * BLOCK, 1), 0)
    col = lax.broadcasted_iota(jnp.int32, (2 * BLOCK, 2 * BLOCK), 1)
    key_ok = col >= jnp.where(n > 0, 0, BLOCK)
    kv = jnp.concatenate([kvp_ref[...], kvc_ref[...]], axis=0)
    for p in range(SWA_HEADS // 2):
        g = p // 2
        qp = q_ref[:, p * LANES:(p + 1) * LANES]
        zero = jnp.zeros_like(qp)
        qs = jnp.concatenate([jnp.where(lane < HEAD_DIM, qp, zero), jnp.where(lane >= HEAD_DIM, qp, zero)], axis=0)
        kd = kv[:, g * LANES:(g + 1) * LANES]
        vd = kv[:, (2 + g) * LANES:(3 + g) * LANES]
        s = _dot_nt(qs, kd) + bias_ref[p]
        s = jnp.where(key_ok, s, NEG_BIG)
        sink = jnp.where(row2 < BLOCK, sink_ref[2 * p], sink_ref[2 * p + 1])
        m = jnp.maximum(jnp.max(s, axis=-1, keepdims=True), sink)
        e = jnp.exp(s - m)
        denom = jnp.sum(e, axis=-1, keepdims=True) + jnp.exp(sink - m)
        o = _dot(e.astype(BF16), vd) / denom
        o_ref[:, p * LANES:(p + 1) * LANES] = jnp.where(lane < HEAD_DIM, o[:BLOCK], o[BLOCK:]).astype(BF16)


def _swa(rel_flat, sinks, bkt, qa, kva):
    s = qa.shape[0]
    nb = s // BLOCK
    smem = pl.BlockSpec(memory_space=pltpu.SMEM)
    return pl.pallas_call(
        _swa_kernel,
        grid=(nb,),
        in_specs=[smem, smem, _const_spec(bkt.shape),
                  pl.BlockSpec((BLOCK, SWA_Q), lambda n: (n, 0)),
                  pl.BlockSpec((BLOCK, 4 * LANES), lambda n: (n, 0)),
                  pl.BlockSpec((BLOCK, 4 * LANES), lambda n: (jnp.maximum(n - 1, 0), 0))],
        out_specs=pl.BlockSpec((BLOCK, SWA_Q), lambda n: (n, 0)),
        out_shape=jax.ShapeDtypeStruct((s, SWA_Q), BF16),
        scratch_shapes=[pltpu.VMEM((SWA_HEADS // 2, 2 * BLOCK, 2 * BLOCK), F32)],
        compiler_params=_params(("arbitrary",)),
        name="swa_attn",
    )(rel_flat, sinks, bkt, qa, kva, kva)


def _fox_kernel(jlo_ref, base_ref, q_ref, k_ref, ca_ref, vt_ref, o_ref):
    p = pl.program_id(0)
    i = pl.program_id(1)
    t = FOX_TILE
    lane = lax.broadcasted_iota(jnp.int32, (t, LANES), 1)
    q = q_ref[...]
    wq = []
    for hh in range(2):
        in_head = (lane >= hh * HEAD_DIM) & (lane < (hh + 1) * HEAD_DIM)
        qm = jnp.where(in_head, q, jnp.zeros_like(q))
        sel = (lane >= N_SPLIT * hh) & (lane < N_SPLIT * (hh + 1))
        aug = jnp.where(sel, -1.0, 0.0).astype(BF16)
        wq.append(jnp.concatenate([qm, aug], axis=1))
    key = lax.broadcasted_iota(jnp.int32, (t, t), 0)
    qry = lax.broadcasted_iota(jnp.int32, (t, t), 1)
    causal = key <= qry

    def step(j, carry, heads, nsub, diagonal=False):
        start = pl.multiple_of(j * t, t)
        rows = nsub * t
        lhs = jnp.concatenate([k_ref[pl.ds(start, rows), :], ca_ref[pl.ds(start, rows), :]], axis=1)
        scores = [_dot_nt(lhs, wq[hh]) for hh in heads]
        mid = []
        for hh, s, (m, l, acc) in zip(heads, scores, carry):
            h = 2 * p + hh
            parts = [s[k * t:(k + 1) * t] for k in range(nsub)]
            if diagonal:
                parts[-1] = jnp.where(causal, parts[-1], NEG_BIG)
            djs = [base_ref[h, i] - base_ref[h, j + k] for k in range(nsub)]
            m_new = m
            for part, dj in zip(parts, djs):
                m_new = jnp.maximum(m_new, jnp.max(part, axis=0, keepdims=True) + dj)
            alpha = jnp.exp2(m - m_new)
            es = [jnp.exp2(part + (dj - m_new)) for part, dj in zip(parts, djs)]
            l = alpha * l
            for e in es:
                l = l + jnp.sum(e, axis=0, keepdims=True)
            e_all = es[0] if nsub == 1 else jnp.concatenate(es, axis=0)
            mid.append((m_new, l, alpha, acc, e_all.astype(BF16)))
        new = []
        for hh, (m_new, l, alpha, acc, e_all) in zip(heads, mid):
            vt = vt_ref[hh * HEAD_DIM:(hh + 1) * HEAD_DIM, pl.ds(start, rows)]
            new.append((m_new, l, alpha * acc + _dot(vt, e_all)))
        return tuple(new)

    def run(lo, hi, carry, heads):
        odd = (hi - lo) & 1
        carry = lax.fori_loop(0, odd, lambda _, c: step(lo, c, heads, 1), carry)
        lo2 = lo + odd
        return lax.fori_loop(0, (hi - lo2) // 2, lambda n, c: step(lo2 + 2 * n, c, heads, 2), carry)

    lo0 = jlo_ref[2 * p, i]
    lo1 = jlo_ref[2 * p + 1, i]
    lo_both = jnp.maximum(lo0, lo1)
    init = (jnp.full((1, t), NEG_BIG, F32), jnp.zeros((1, t), F32), jnp.zeros((HEAD_DIM, t), F32))
    (c0,) = run(lo0, lo_both, (init,), (0,))
    (c1,) = run(lo1, lo_both, (init,), (1,))
    carry = run(lo_both, i, (c0, c1), (0, 1))
    carry = step(i, carry, (0, 1), 1, diagonal=True)
    ot = jnp.concatenate([carry[0][2] / carry[0][1], carry[1][2] / carry[1][1]], axis=0)
    o_ref[...] = ot.T.astype(BF16)


def _fox_schedule(cum, qn2, kn2):
    nt = qn2.shape[0]
    t = FOX_TILE
    bq = jnp.sqrt(qn2.reshape(nt, FOX_HEADS)) * NORM_SLACK
    bk = jnp.sqrt(kn2.reshape(nt, FOX_HEADS)) * NORM_SLACK
    c2 = cum.reshape(nt, t, FOX_HEADS) * LOG2E
    c_first, c_last = c2[:, 0, :], c2[:, t - 1, :]
    upper = (bq[:, None, :] * bk[None, :, :] + (bq * bk)[:, None, :] + c_first[:, None, :] - c_last[None, :, :])
    ii = jnp.arange(nt)[:, None, None]
    jj = jnp.arange(nt)[None, :, None]
    needed = (jj <= ii) & ((upper >= -SKIP_LOG2) | (jj == ii))
    jlo = jnp.min(jnp.where(needed, jj, nt), axis=1).T.astype(jnp.int32)
    base = jnp.concatenate([jnp.zeros((1, FOX_HEADS), F32), c_last[:-1]], axis=0).T
    return jlo, base


def _fox(jlo, base, qb, kb, ca, vbt):
    s = qb.shape[0]
    t = FOX_TILE
    grid_spec = pltpu.PrefetchScalarGridSpec(
        num_scalar_prefetch=2,
        grid=(N_PAIRS, s // t),
        in_specs=[pl.BlockSpec((t, LANES), lambda p, i, *_: (i, p)),
                  pl.BlockSpec((s, LANES), lambda p, i, *_: (0, p)),
                  pl.BlockSpec((s, LANES), lambda p, i, *_: (0, p)),
                  pl.BlockSpec((LANES, s), lambda p, i, *_: (p, 0))],
        out_specs=pl.BlockSpec((t, LANES), lambda p, i, *_: (i, p)),
    )
    return pl.pallas_call(
        _fox_kernel,
        grid_spec=grid_spec,
        out_shape=jax.ShapeDtypeStruct((s, FOX_W), BF16),
        compiler_params=_params(("arbitrary", "arbitrary")),
        name="fox_attn",
    )(jlo, base, qb, kb, ca, vbt)


def _rms(x, g):
    return x * lax.rsqrt(jnp.mean(x * x, axis=-1, keepdims=True) + RMS_EPS) * g


def _pack_rows(v):
    halves = []
    for j in range(2):
        lo = v[:, (2 * j) * PACK_W:(2 * j + 1) * PACK_W].astype(BF16).astype(F32)
        hi = v[:, (2 * j + 1) * PACK_W:(2 * j + 2) * PACK_W].astype(BF16).astype(F32)
        lo_bits = lax.bitcast_convert_type(lo, jnp.uint32)
        hi_bits = lax.bitcast_convert_type(hi, jnp.uint32)
        halves.append(hi_bits | (lo_bits >> 16))
    return halves


def _unpack_words(w):
    lo = lax.bitcast_convert_type(w << 16, F32)
    hi = lax.bitcast_convert_type(w & jnp.uint32(0xFFFF0000), F32)
    return lo, hi


def _outproj_kernel(x_ref, ya_ref, yb_ref, sga_ref, sgb_ref, pa_ref, pb_ref, wo_ref, gt_ref, g_ref, sh_ref, sc_ref,
                    wr_ref, br_ref, x1_ref, u2p_ref, e4_ref, r4_ref, w4_ref, cnt_ref, carry_ref):
    i = pl.program_id(0)
    tm = x_ref.shape[0]

    @pl.when(i == 0)
    def _():
        carry_ref[...] = jnp.zeros_like(carry_ref)

    merged = (sga_ref[...].astype(F32) * _dot(ya_ref[...], pa_ref[...])
              + sgb_ref[...].astype(F32) * _dot(yb_ref[...], pb_ref[...]))
    x1 = x_ref[...] + gt_ref[...] * _dot(merged.astype(BF16), wo_ref[...])
    x1_ref[...] = x1
    u2 = _rms(x1, g_ref[...]) * (1.0 + sc_ref[...]) + sh_ref[...]
    halves = _pack_rows(u2)
    u2p_ref[0] = halves[0]
    u2p_ref[1] = halves[1]

    logits = jnp.dot(u2, wr_ref[...], preferred_element_type=F32, precision=HIGHEST) + br_ref[...]
    eidx = lax.broadcasted_iota(jnp.int32, logits.shape, 1)
    work = logits
    sel = jnp.zeros(logits.shape, jnp.bool_)
    picks, vals = [], []
    for k in range(TOP_K):
        m = jnp.max(work, axis=-1, keepdims=True)
        first = jnp.min(jnp.where(work == m, eidx, N_EXPERTS), axis=-1, keepdims=True)
        hit = eidx == first
        sel = sel | hit
        work = jnp.where(hit, -jnp.inf, work)
        picks.append(first)
        vals.append(m)
    exps = [jnp.exp(v - vals[0]) for v in vals]
    denom = exps[0] + exps[1] + exps[2] + exps[3]

    r = lax.broadcasted_iota(jnp.int32, (tm, tm), 0)
    cc = lax.broadcasted_iota(jnp.int32, (tm, tm), 1)
    before = (cc < r).astype(BF16)
    chosen = sel.astype(BF16)
    rank = _dot(before, chosen) + carry_ref[...]
    cnt = carry_ref[...] + jnp.sum(chosen.astype(F32), axis=0, keepdims=True)
    carry_ref[...] = cnt
    cnt_ref[...] = cnt
    for k in range(TOP_K):
        e4_ref[:, k:k + 1] = picks[k]
        rk = jnp.sum(jnp.where(eidx == picks[k], rank, 0.0), axis=-1, keepdims=True)
        r4_ref[:, k:k + 1] = rk.astype(jnp.int32)
        w4_ref[:, k:k + 1] = exps[k] / denom


def _outproj(x2, ya, yb, sga, sgb, pa, pb, wo, gt, g, sh, sc, wr, br, tm):
    s = x2.shape[0]
    row = lambda n: pl.BlockSpec((tm, n), lambda i: (i, 0))
    consts = [pa, pb, wo, gt, g, sh, sc, wr, br]
    return pl.pallas_call(
        _outproj_kernel,
        grid=(s // tm,),
        in_specs=[row(D_MODEL), row(SWA_Q), row(FOX_W), row(D_MODEL), row(D_MODEL)] + [_const_spec(a.shape) for a in consts],
        out_specs=[row(D_MODEL), pl.BlockSpec((2, tm, PACK_W), lambda i: (0, i, 0)), row(TOP_K), row(TOP_K), row(TOP_K),
                   _const_spec((1, N_EXPERTS))],
        out_shape=[jax.ShapeDtypeStruct((s, D_MODEL), F32),
                   jax.ShapeDtypeStruct((2, s, PACK_W), jnp.uint32),
                   jax.ShapeDtypeStruct((s, TOP_K), jnp.int32),
                   jax.ShapeDtypeStruct((s, TOP_K), jnp.int32),
                   jax.ShapeDtypeStruct((s, TOP_K), F32),
                   jax.ShapeDtypeStruct((1, N_EXPERTS), F32)],
        scratch_shapes=[pltpu.VMEM((1, N_EXPERTS), F32)],
        compiler_params=_params(("arbitrary",)),
        name="out_proj_router",
    )(x2, ya, yb, sga, sgb, *consts)


def _sc_mesh():
    return plsc.VectorSubcoreMesh(core_axis_name="core", subcore_axis_name="subcore")


def _sc_dispatch(rows, idx, n_out):
    n, width = rows.shape

    @functools.partial(pl.kernel, out_type=jax.ShapeDtypeStruct((n_out, width), rows.dtype), mesh=_sc_mesh(),
                       scratch_types=[])
    def dispatch(x_hbm, i_hbm, o_hbm):
        def body(x_vmem, i_vmem):
            for k in range(TOP_K):
                pltpu.sync_copy(x_vmem, o_hbm.at[i_vmem.at[k]])

        pltpu.emit_pipeline(
            body, grid=(n // SC_WINDOW,),
            in_specs=[pl.BlockSpec((SC_WINDOW, width), lambda i: (i, 0)),
                      pl.BlockSpec((TOP_K, SC_WINDOW), lambda i: (0, i))],
            out_specs=[], core_axis_name=("core", "subcore"), dimension_semantics=(pltpu.PARALLEL,),
        )(x_hbm, i_hbm)

    return dispatch(rows, idx)


def _sc_gather(table, idx):
    n = idx.shape[1]
    width = table.shape[1]

    @functools.partial(pl.kernel, out_type=jax.ShapeDtypeStruct((n, width), table.dtype), mesh=_sc_mesh(),
                       scratch_types=[])
    def gather(t_hbm, i_hbm, o_hbm):
        def body(i_vmem, o_vmem):
            pltpu.sync_copy(t_hbm.at[i_vmem.at[0]], o_vmem)

        pltpu.emit_pipeline(
            body, grid=(n // SC_WINDOW,),
            in_specs=[pl.BlockSpec((1, SC_WINDOW), lambda i: (0, i))],
            out_specs=[pl.BlockSpec((SC_WINDOW, width), lambda i: (i, 0))],
            core_axis_name=("core", "subcore"), dimension_semantics=(pltpu.PARALLEL,),
        )(i_hbm, o_hbm)

    return gather(table, idx)


def _moe_kernel(te_ref, nact_ref, xs_ref, w1g_ref, w1l_ref, b1g_ref, b1l_ref, w2_ref, b2_ref, y_ref):
    i = pl.program_id(0)

    @pl.when(i < nact_ref[0])
    def _():
        chunks = []
        for j in range(2):
            lo, hi = _unpack_words(xs_ref[j])
            chunks += [lo.astype(BF16), hi.astype(BF16)]
        hg = b1g_ref[0]
        hl = b1l_ref[0]
        for c, xc in enumerate(chunks):
            cols = slice(c * PACK_W, (c + 1) * PACK_W)
            hg = hg + _dot_nt(xc, w1g_ref[0, :, cols])
            hl = hl + _dot_nt(xc, w1l_ref[0, :, cols])
        glu = jnp.minimum(hg, SWIGLU_LIMIT)
        lin = jnp.clip(hl, -SWIGLU_LIMIT, SWIGLU_LIMIT)
        a = glu * jax.nn.sigmoid(SWIGLU_ALPHA * glu) * (lin + 1.0)
        y = _dot(a.astype(BF16), w2_ref[0]) + b2_ref[0]
        halves = _pack_rows(y)
        y_ref[0] = halves[0]
        y_ref[1] = halves[1]


def _moe_routed(tile_expert, n_active, xs, w1t, b1g, b1l, w2, b2):
    n_rows = xs.shape[1]
    n_tiles = n_rows // MOE_TILE
    live = lambda i, te, na: jnp.minimum(i, na[0] - 1)
    rows_spec = pl.BlockSpec((2, MOE_TILE, PACK_W), lambda i, te, na: (0, live(i, te, na), 0))
    wsp = lambda a, col=0: pl.BlockSpec((1,) + a.shape[1:], lambda i, te, na: (te[live(i, te, na)], 0, col))
    w1g_spec = pl.BlockSpec((1, D_FF, D_MODEL), lambda i, te, na: (te[live(i, te, na)], 0, 0))
    w1l_spec = pl.BlockSpec((1, D_FF, D_MODEL), lambda i, te, na: (te[live(i, te, na)], 0, 1))
    grid_spec = pltpu.PrefetchScalarGridSpec(
        num_scalar_prefetch=2,
        grid=(n_tiles,),
        in_specs=[rows_spec, w1g_spec, w1l_spec, wsp(b1g), wsp(b1l), wsp(w2), wsp(b2)],
        out_specs=rows_spec,
    )
    return pl.pallas_call(
        _moe_kernel,
        grid_spec=grid_spec,
        out_shape=jax.ShapeDtypeStruct(xs.shape, jnp.uint32),
        compiler_params=_params(("arbitrary",)),
        name="moe_routed",
    )(tile_expert, n_active, xs, w1t, w1t, b1g, b1l, w2, b2)


def _final_kernel(x1_ref, yg_ref, w4_ref, gt_ref, gf_ref, o_ref):
    w4 = w4_ref[...]
    cols = []
    for j in range(2):
        lo_acc = hi_acc = None
        for k in range(TOP_K):
            lo, hi = _unpack_words(yg_ref[j, k])
            wk = w4[:, k:k + 1]
            lo_acc = wk * lo if lo_acc is None else lo_acc + wk * lo
            hi_acc = wk * hi if hi_acc is None else hi_acc + wk * hi
        cols += [lo_acc, hi_acc]
    moe = jnp.concatenate(cols, axis=1)
    x2 = x1_ref[...] + gt_ref[...] * moe
    o_ref[...] = _rms(x2, gf_ref[...])


def _final(x1, yg, w4, gt, gf, tm):
    s = x1.shape[0]
    row = lambda n: pl.BlockSpec((tm, n), lambda i: (i, 0))
    return pl.pallas_call(
        _final_kernel,
        grid=(s // tm,),
        in_specs=[row(D_MODEL), pl.BlockSpec((2, TOP_K, tm, PACK_W), lambda i: (0, 0, i, 0)), row(TOP_K),
                  _const_spec(gt.shape), _const_spec(gf.shape)],
        out_specs=row(D_MODEL),
        out_shape=jax.ShapeDtypeStruct((s, D_MODEL), F32),
        compiler_params=_params(("arbitrary",)),
        name="combine_final_norm",
    )(x1, yg, w4, gt, gf)


def _routing_tables(e4, r4, counts, s):
    n_rows = TOP_K * s + N_EXPERTS * MOE_TILE
    cnt = counts.reshape(N_EXPERTS).astype(jnp.int32)
    padded = ((cnt + MOE_TILE - 1) // MOE_TILE) * MOE_TILE
    ends = jnp.cumsum(padded)
    starts = ends - padded
    pos = (jnp.take(starts, e4, axis=0) + r4).T
    tile_start = jnp.arange(n_rows // MOE_TILE, dtype=jnp.int32) * MOE_TILE
    tile_expert = jnp.minimum(jnp.sum(tile_start[:, None] >= ends[None, :], axis=1), N_EXPERTS - 1).astype(jnp.int32)
    n_active = (ends[-1:] // MOE_TILE).astype(jnp.int32)
    return pos, tile_expert, n_active, n_rows


def kernel(x, c, w_ada, b_ada, g_mix, w_in, b_forget, sinks, rel_bias, w_proj_a, w_proj_b, w_out, g_ffn,
           w_router, b_router, w_e1, b_e1, w_e2, b_e2, g_final):
    b, s, d = x.shape
    assert b == 1 and d == D_MODEL and w_ada.shape[0] == 1
    x2 = x.reshape(s, d)
    tm = min(512, s)

    mod = _ada(jnp.broadcast_to(c, (8, d)), w_ada[0], b_ada)[:1]
    sh_m, sc_m, gt_m, sh_f, sc_f, gt_f = [mod[:, k * d:(k + 1) * d] for k in range(N_MOD)]

    w = w_in[0]
    o_ka, o_va, o_b = SWA_Q, SWA_Q + SWA_KV, SWA_Q + 2 * SWA_KV
    o_f = o_b + 3 * FOX_W
    o_g = o_f + FOX_HEADS
    dup = lambda m: jnp.concatenate([m[:, :HEAD_DIM], m[:, :HEAD_DIM], m[:, HEAD_DIM:], m[:, HEAD_DIM:]], axis=1)
    wa = jnp.concatenate([w[:, :SWA_Q], dup(w[:, o_ka:o_va]), dup(w[:, o_va:o_b])], axis=1).astype(BF16)
    wb = w[:, o_b:o_b + 2 * FOX_W].astype(BF16)
    wvt = w[:, o_b + 2 * FOX_W:o_f].T.astype(BF16)
    wf = w[:, o_f:o_g]
    wg = w[:, o_g:].astype(BF16)
    qa, kva, qb, kb, vbt, sga, sgb, cum, ca, qn2, kn2 = _inproj(
        x2, g_mix, sh_m, sc_m, wa, wb, wvt, wg, wf, b_forget, tm)

    ya = _swa(rel_bias.reshape(-1), sinks[0], jnp.asarray(_t5_buckets_np()), qa, kva)
    jlo, base = _fox_schedule(cum, qn2, kn2)
    yb = _fox(jlo, base, qb, kb, ca, vbt)

    x1, u2p, e4, r4, w4, counts = _outproj(
        x2, ya, yb, sga, sgb, w_proj_a[0].astype(BF16), w_proj_b[0].astype(BF16), w_out[0].astype(BF16),
        gt_m, g_ffn, sh_f, sc_f, w_router[0], b_router, tm)

    pos, tile_expert, n_active, n_rows = _routing_tables(e4, r4, counts, s)
    pos2 = jnp.concatenate([pos, pos + n_rows], axis=1)
    xs = _sc_dispatch(u2p.reshape(2 * s, PACK_W), pos2, 2 * n_rows).reshape(2, n_rows, PACK_W)

    w1t = jnp.swapaxes(w_e1[0], 1, 2).astype(BF16).reshape(N_EXPERTS, D_FF, 2 * D_MODEL)
    b1 = b_e1[0].reshape(N_EXPERTS, D_FF, 2)
    b1g = b1[:, None, :, 0]
    b1l = b1[:, None, :, 1]
    ys = _moe_routed(tile_expert, n_active, xs, w1t, b1g, b1l, w_e2[0].astype(BF16), b_e2[0][:, None, :])

    gather_idx = pos2.reshape(TOP_K, 2, s).transpose(1, 0, 2).reshape(1, -1)
    yg = _sc_gather(ys.reshape(2 * n_rows, PACK_W), gather_idx)
    out = _final(x1, yg.reshape(2, TOP_K, s, PACK_W), w4, gt_f, g_final.reshape(1, d), tm)
    return out.reshape(b, s, d)
```

```python
import functools
import math

import numpy as np
import jax
import jax.numpy as jnp
from jax import lax
from jax.experimental import pallas as pl
from jax.experimental.pallas import tpu as pltpu
from jax.experimental.pallas import tpu_sc as plsc

D_MODEL = 1024
HEAD_DIM = 64
SWA_HEADS = 8
SWA_KV_HEADS = 2
WINDOW = 128
FOX_HEADS = 8
BLOCK = 128
REL_BUCKETS = 32
REL_MAX_DIST = WINDOW
N_EXPERTS = 32
TOP_K = 4
D_FF = D_MODEL
SWIGLU_LIMIT = 7.0
SWIGLU_ALPHA = 1.702
RMS_EPS = 1e-5
N_MOD = 6

SWA_Q = SWA_HEADS * HEAD_DIM
SWA_KV = SWA_KV_HEADS * HEAD_DIM
FOX_W = FOX_HEADS * HEAD_DIM
LANES = 128
N_PAIRS = FOX_HEADS // 2
NEG_BIG = -1e30
LOG2E = math.log2(math.e)
FOX_TILE = 256
FOX_CHUNK = 4
N_SPLIT = 3
SKIP_LOG2 = 127.0
NORM_SLACK = 1.01
PACK_W = 256
MOE_TILE = 512
SC_WINDOW = 128
PREP_COLS = 256
VMEM_LIMIT = 56 * 1024 * 1024

F32 = jnp.float32
BF16 = jnp.bfloat16
HIGHEST = lax.Precision.HIGHEST


def _dot(a, b):
    return jnp.dot(a, b, preferred_element_type=F32)


def _dot_nt(a, b, precision=None):
    return lax.dot_general(a, b, (((1,), (1,)), ((), ())), preferred_element_type=F32, precision=precision)


def _const_spec(shape):
    nd = len(shape)
    return pl.BlockSpec(shape, lambda *_: (0,) * nd)


def _params(sem):
    return pltpu.CompilerParams(dimension_semantics=sem, vmem_limit_bytes=VMEM_LIMIT)


def _ada_kernel(c_ref, w_ref, b_ref, o_ref):
    c = c_ref[...]
    act = c * jax.nn.sigmoid(c)
    o_ref[...] = jnp.sum(act * w_ref[...], axis=0, keepdims=True) + b_ref[...]


def _ada(c_col, w_ada, b_ada):
    n = w_ada.shape[1]
    tn = 1024
    return pl.pallas_call(
        _ada_kernel,
        grid=(n // tn,),
        in_specs=[_const_spec((D_MODEL, 1)),
                  pl.BlockSpec((D_MODEL, tn), lambda j: (0, j)),
                  pl.BlockSpec((1, tn), lambda j: (0, j))],
        out_specs=pl.BlockSpec((1, tn), lambda j: (0, j)),
        out_shape=jax.ShapeDtypeStruct((1, n), F32),
        compiler_params=_params(("arbitrary",)),
        name="ada_mod",
    )(c_col, w_ada, b_ada)


def _split_bf16(v):
    parts = []
    for _ in range(N_SPLIT):
        p = v.astype(BF16)
        v = v - p.astype(F32)
        parts.append(p)
    return parts


def _inproj_kernel(x_ref, g_ref, sh_ref, sc_ref, wa_ref, wb_ref, wvt_ref, wg_ref, bf_ref, hind_ref, place_ref,
                   qa_ref, kva_ref, qb_ref, kb_ref, vbt_ref, sga_ref, sgb_ref, c_ref, ca_ref, qn_ref, kn_ref,
                   carry_ref):
    i = pl.program_id(0)
    tm = x_ref.shape[0]
    t = FOX_TILE
    nsub = tm // t

    @pl.when(i == 0)
    def _():
        carry_ref[...] = jnp.zeros_like(carry_ref)

    xf = x_ref[...]
    ms = jnp.mean(xf * xf, axis=-1, keepdims=True)
    y = xf * lax.rsqrt(ms + RMS_EPS) * g_ref[...]
    u = y * (1.0 + sc_ref[...]) + sh_ref[...]
    ub = u.astype(BF16)

    za = _dot(ub, wa_ref[...])
    qa_ref[...] = (za[:, :SWA_Q] * (HEAD_DIM ** -0.5)).astype(BF16)
    kva_ref[...] = za[:, SWA_Q:].astype(BF16)

    zb = _dot(ub, wb_ref[...])
    qb = (zb[:, :FOX_W] * (HEAD_DIM ** -0.5 * LOG2E)).astype(BF16)
    kb = zb[:, FOX_W:2 * FOX_W].astype(BF16)
    qb_ref[...] = qb
    kb_ref[...] = kb
    vbt_ref[...] = _dot_nt(wvt_ref[...], ub).astype(BF16)

    def tile_norm_max(z, o_ref):
        zf = z.astype(F32)
        n2 = _dot((zf * zf).astype(BF16), hind_ref[...])
        for sb in range(nsub):
            o_ref[sb] = jnp.max(n2[sb * t:(sb + 1) * t], axis=0, keepdims=True)

    tile_norm_max(qb, qn_ref)
    tile_norm_max(kb, kn_ref)

    zg = _dot(ub, wg_ref[...])
    sg = jax.nn.sigmoid(zg)
    sga_ref[...] = sg[:, :D_MODEL].astype(BF16)
    sgb_ref[...] = sg[:, D_MODEL:].astype(BF16)

    fb = zb[:, 2 * FOX_W:2 * FOX_W + FOX_HEADS] + bf_ref[...]
    lf = jnp.minimum(fb, 0.0) - jnp.log1p(jnp.exp(-jnp.abs(fb)))
    r = lax.broadcasted_iota(jnp.int32, (t, t), 0)
    cc = lax.broadcasted_iota(jnp.int32, (t, t), 1)
    lower = (cc <= r).astype(BF16)
    carry = carry_ref[...]
    for sb in range(nsub):
        rows = slice(sb * t, (sb + 1) * t)
        local = sum(_dot(lower, p) for p in _split_bf16(lf[rows]))
        c_ref[rows, :] = local + carry
        carry = carry + local[t - 1:t, :]
        aug = sum(_dot(p, place_ref[k]) for k, p in enumerate(_split_bf16(local * LOG2E)))
        ca_ref[rows, :] = aug.astype(BF16)
    carry_ref[...] = carry


def _inproj(x2, g, sh, sc, wa, wb, wvt, wg, bfor, tm):
    s = x2.shape[0]
    nsub = tm // FOX_TILE
    row = lambda n: pl.BlockSpec((tm, n), lambda i: (i, 0))
    hind = np.zeros((FOX_W, FOX_HEADS), np.float32)
    hind[np.arange(FOX_W), np.arange(FOX_W) // HEAD_DIM] = 1.0
    place = np.zeros((N_SPLIT, FOX_HEADS, N_PAIRS * LANES), np.float32)
    for k in range(N_SPLIT):
        for h in range(FOX_HEADS):
            place[k, h, (h // 2) * LANES + N_SPLIT * (h % 2) + k] = 1.0
    hind = jnp.asarray(hind, BF16)
    place = jnp.asarray(place, BF16)
    out_shape = [
        jax.ShapeDtypeStruct((s, SWA_Q), BF16),
        jax.ShapeDtypeStruct((s, 4 * LANES), BF16),
        jax.ShapeDtypeStruct((s, FOX_W), BF16),
        jax.ShapeDtypeStruct((s, FOX_W), BF16),
        jax.ShapeDtypeStruct((FOX_W, s), BF16),
        jax.ShapeDtypeStruct((s, D_MODEL), BF16),
        jax.ShapeDtypeStruct((s, D_MODEL), BF16),
        jax.ShapeDtypeStruct((s, FOX_HEADS), F32),
        jax.ShapeDtypeStruct((s, N_PAIRS * LANES), BF16),
        jax.ShapeDtypeStruct((s // FOX_TILE, 1, FOX_HEADS), F32),
        jax.ShapeDtypeStruct((s // FOX_TILE, 1, FOX_HEADS), F32),
    ]
    stat = pl.BlockSpec((nsub, 1, FOX_HEADS), lambda i: (i, 0, 0))
    out_specs = [row(SWA_Q), row(4 * LANES), row(FOX_W), row(FOX_W), pl.BlockSpec((FOX_W, tm), lambda i: (0, i)),
                 row(D_MODEL), row(D_MODEL), row(FOX_HEADS), row(N_PAIRS * LANES), stat, stat]
    consts = [g, sh, sc, wa, wb, wvt, wg, bfor, hind, place]
    return pl.pallas_call(
        _inproj_kernel,
        grid=(s // tm,),
        in_specs=[row(D_MODEL)] + [_const_spec(a.shape) for a in consts],
        out_specs=out_specs,
        out_shape=out_shape,
        scratch_shapes=[pltpu.VMEM((1, FOX_HEADS), F32)],
        compiler_params=_params(("arbitrary",)),
        name="in_proj",
    )(x2, *consts)


def _t5_buckets_np():
    qi = np.arange(BLOCK)[:, None]
    kj = np.arange(2 * BLOCK)[None, :]
    dist = BLOCK + qi - kj
    n = np.maximum(dist, 0)
    max_exact = REL_BUCKETS // 2
    nf = np.maximum(n, 1).astype(np.float32)
    large = max_exact + (np.log(nf / np.float32(max_exact)) / np.float32(math.log(REL_MAX_DIST / max_exact))
                         * np.float32(REL_BUCKETS - max_exact)).astype(np.int32)
    large = np.minimum(large, REL_BUCKETS - 1)
    bucket = np.where(n < max_exact, n, large).astype(np.int32)
    band = (dist >= 0) & (dist < WINDOW)
    return np.where(band, bucket, -1).astype(np.int32)


def _swa_kernel(rel_ref, sink_ref, bkt_ref, q_ref, kvc_ref, kvp_ref, o_ref, bias_ref):
    n = pl.program_id(0)

    @pl.when(n == 0)
    def _():
        bkt = bkt_ref[...]
        for h in range(SWA_HEADS):
            acc = jnp.full(bkt.shape, NEG_BIG, F32)
            for b in range(REL_BUCKETS):
                acc = jnp.where(bkt == b, rel_ref[b * SWA_HEADS + h], acc)
            bias_ref[h // 2, (h % 2) * BLOCK:(h % 2 + 1) * BLOCK, :] = acc

    lane = lax.broadcasted_iota(jnp.int32, (BLOCK, LANES), 1)
    row2 = lax.broadcasted_iota(jnp.int32, (2 * BLOCK, 1), 0)
    col = lax.broadcasted_iota(jnp.int32, (2 * BLOCK, 2 * BLOCK), 1)
    key_ok = col >= jnp.where(n > 0, 0, BLOCK)
    kv = jnp.concatenate([kvp_ref[...], kvc_ref[...]], axis=0)
    for p in range(SWA_HEADS // 2):
        g = p // 2
        qp = q_ref[:, p * LANES:(p + 1) * LANES]
        zero = jnp.zeros_like(qp)
        qs = jnp.concatenate([jnp.where(lane < HEAD_DIM, qp, zero), jnp.where(lane >= HEAD_DIM, qp, zero)], axis=0)
        kd = kv[:, g * LANES:(g + 1) * LANES]
        vd = kv[:, (2 + g) * LANES:(3 + g) * LANES]
        s = _dot_nt(qs, kd) + bias_ref[p]
        s = jnp.where(key_ok, s, NEG_BIG)
        sink = jnp.where(row2 < BLOCK, sink_ref[2 * p], sink_ref[2 * p + 1])
        m = jnp.maximum(jnp.max(s, axis=-1, keepdims=True), sink)
        e = jnp.exp(s - m)
        denom = jnp.sum(e, axis=-1, keepdims=True) + jnp.exp(sink - m)
        o = _dot(e.astype(BF16), vd) / denom
        o_ref[:, p * LANES:(p + 1) * LANES] = jnp.where(lane < HEAD_DIM, o[:BLOCK], o[BLOCK:]).astype(BF16)


def _swa(rel_flat, sinks, bkt, qa, kva):
    s = qa.shape[0]
    nb = s // BLOCK
    smem = pl.BlockSpec(memory_space=pltpu.SMEM)
    return pl.pallas_call(
        _swa_kernel,
        grid=(nb,),
        in_specs=[smem, smem, _const_spec(bkt.shape),
                  pl.BlockSpec((BLOCK, SWA_Q), lambda n: (n, 0)),
                  pl.BlockSpec((BLOCK, 4 * LANES), lambda n: (n, 0)),
                  pl.BlockSpec((BLOCK, 4 * LANES), lambda n: (jnp.maximum(n - 1, 0), 0))],
        out_specs=pl.BlockSpec((BLOCK, SWA_Q), lambda n: (n, 0)),
        out_shape=jax.ShapeDtypeStruct((s, SWA_Q), BF16),
        scratch_shapes=[pltpu.VMEM((SWA_HEADS // 2, 2 * BLOCK, 2 * BLOCK), F32)],
        compiler_params=_params(("arbitrary",)),
        name="swa_attn",
    )(rel_flat, sinks, bkt, qa, kva, kva)


def _fox_kernel(jlo_ref, base_ref, q_ref, k_ref, ca_ref, vt_ref, o_ref):
    p = pl.program_id(0)
    i = pl.program_id(1)
    t = FOX_TILE
    lane = lax.broadcasted_iota(jnp.int32, (t, LANES), 1)
    q = q_ref[...]
    wq = []
    for hh in range(2):
        in_head = (lane >= hh * HEAD_DIM) & (lane < (hh + 1) * HEAD_DIM)
        qm = jnp.where(in_head, q, jnp.zeros_like(q))
        sel = (lane >= N_SPLIT * hh) & (lane < N_SPLIT * (hh + 1))
        aug = jnp.where(sel, -1.0, 0.0).astype(BF16)
        wq.append(jnp.concatenate([qm, aug], axis=1))
    key = lax.broadcasted_iota(jnp.int32, (t, t), 0)
    qry = lax.broadcasted_iota(jnp.int32, (t, t), 1)
    causal = key <= qry

    def step(j, carry, heads, nsub, diagonal=False):
        start = pl.multiple_of(j * t, t)
        rows = nsub * t
        lhs = jnp.concatenate([k_ref[pl.ds(start, rows), :], ca_ref[pl.ds(start, rows), :]], axis=1)
        scores = [_dot_nt(lhs, wq[hh]) for hh in heads]
        mid = []
        for hh, s, (m, l, acc) in zip(heads, scores, carry):
            h = 2 * p + hh
            parts = [s[k * t:(k + 1) * t] for k in range(nsub)]
            if diagonal:
                parts[-1] = jnp.where(causal, parts[-1], NEG_BIG)
            djs = [base_ref[h, i] - base_ref[h, j + k] for k in range(nsub)]
            m_new = m
            for part, dj in zip(parts, djs):
                m_new = jnp.maximum(m_new, jnp.max(part, axis=0, keepdims=True) + dj)
            alpha = jnp.exp2(m - m_new)
            es = [jnp.exp2(part + (dj - m_new)) for part, dj in zip(parts, djs)]
            l = alpha * l
            for e in es:
                l = l + jnp.sum(e, axis=0, keepdims=True)
            e_all = es[0] if nsub == 1 else jnp.concatenate(es, axis=0)
            mid.append((m_new, l, alpha, acc, e_all.astype(BF16)))
        new = []
        for hh, (m_new, l, alpha, acc, e_all) in zip(heads, mid):
            vt = vt_ref[hh * HEAD_DIM:(hh + 1) * HEAD_DIM, pl.ds(start, rows)]
            new.append((m_new, l, alpha * acc + _dot(vt, e_all)))
        return tuple(new)

    def run_alone(lo, hi, carry, heads):
        n_full = (hi - lo) // FOX_CHUNK
        carry = lax.fori_loop(0, n_full, lambda n, c: step(lo + FOX_CHUNK * n, c, heads, FOX_CHUNK), carry)
        rest = lo + FOX_CHUNK * n_full
        tails = [lambda c: c] + [functools.partial(lambda c, k: step(rest, c, heads, k), k=k)
                                 for k in range(1, FOX_CHUNK)]
        return lax.switch(hi - rest, tails, carry)

    def run_to_diagonal(lo, carry, heads):
        count = i + 1 - lo
        last = (count - 1) % FOX_CHUNK + 1
        n_full = (count - last) // FOX_CHUNK
        carry = lax.fori_loop(0, n_full, lambda n, c: step(lo + FOX_CHUNK * n, c, heads, FOX_CHUNK), carry)
        tails = [functools.partial(lambda c, k: step(i + 1 - k, c, heads, k, diagonal=True), k=k)
                 for k in range(1, FOX_CHUNK + 1)]
        return lax.switch(last - 1, tails, carry)

    lo0 = jlo_ref[2 * p, i]
    lo1 = jlo_ref[2 * p + 1, i]
    lo_both = jnp.maximum(lo0, lo1)
    init = (jnp.full((1, t), NEG_BIG, F32), jnp.zeros((1, t), F32), jnp.zeros((HEAD_DIM, t), F32))
    (c0,) = run_alone(lo0, lo_both, (init,), (0,))
    (c1,) = run_alone(lo1, lo_both, (init,), (1,))
    carry = run_to_diagonal(lo_both, (c0, c1), (0, 1))
    ot = jnp.concatenate([carry[0][2] / carry[0][1], carry[1][2] / carry[1][1]], axis=0)
    o_ref[...] = ot.T.astype(BF16)


def _fox_schedule(cum, qn2, kn2):
    nt = qn2.shape[0]
    t = FOX_TILE
    bq = jnp.sqrt(qn2.reshape(nt, FOX_HEADS)) * NORM_SLACK
    bk = jnp.sqrt(kn2.reshape(nt, FOX_HEADS)) * NORM_SLACK
    c2 = cum.reshape(nt, t, FOX_HEADS) * LOG2E
    c_first, c_last = c2[:, 0, :], c2[:, t - 1, :]
    upper = (bq[:, None, :] * bk[None, :, :] + (bq * bk)[:, None, :] + c_first[:, None, :] - c_last[None, :, :])
    ii = jnp.arange(nt)[:, None, None]
    jj = jnp.arange(nt)[None, :, None]
    needed = (jj <= ii) & ((upper >= -SKIP_LOG2) | (jj == ii))
    jlo = jnp.min(jnp.where(needed, jj, nt), axis=1).T.astype(jnp.int32)
    base = jnp.concatenate([jnp.zeros((1, FOX_HEADS), F32), c_last[:-1]], axis=0).T
    return jlo, base


def _fox(jlo, base, qb, kb, ca, vbt):
    s = qb.shape[0]
    t = FOX_TILE
    grid_spec = pltpu.PrefetchScalarGridSpec(
        num_scalar_prefetch=2,
        grid=(N_PAIRS, s // t),
        in_specs=[pl.BlockSpec((t, LANES), lambda p, i, *_: (i, p)),
                  pl.BlockSpec((s, LANES), lambda p, i, *_: (0, p)),
                  pl.BlockSpec((s, LANES), lambda p, i, *_: (0, p)),
                  pl.BlockSpec((LANES, s), lambda p, i, *_: (p, 0))],
        out_specs=pl.BlockSpec((t, LANES), lambda p, i, *_: (i, p)),
    )
    return pl.pallas_call(
        _fox_kernel,
        grid_spec=grid_spec,
        out_shape=jax.ShapeDtypeStruct((s, FOX_W), BF16),
        compiler_params=_params(("arbitrary", "arbitrary")),
        name="fox_attn",
    )(jlo, base, qb, kb, ca, vbt)


def _rms(x, g):
    return x * lax.rsqrt(jnp.mean(x * x, axis=-1, keepdims=True) + RMS_EPS) * g


def _pack_rows(v):
    halves = []
    for j in range(2):
        lo = v[:, (2 * j) * PACK_W:(2 * j + 1) * PACK_W].astype(BF16).astype(F32)
        hi = v[:, (2 * j + 1) * PACK_W:(2 * j + 2) * PACK_W].astype(BF16).astype(F32)
        lo_bits = lax.bitcast_convert_type(lo, jnp.uint32)
        hi_bits = lax.bitcast_convert_type(hi, jnp.uint32)
        halves.append(hi_bits | (lo_bits >> 16))
    return halves


def _unpack_words(w):
    lo = lax.bitcast_convert_type(w << 16, F32)
    hi = lax.bitcast_convert_type(w & jnp.uint32(0xFFFF0000), F32)
    return lo, hi


def _outproj_kernel(x_ref, ya_ref, yb_ref, sga_ref, sgb_ref, pa_ref, pb_ref, wo_ref, gt_ref, g_ref, sh_ref, sc_ref,
                    wr_ref, br_ref, x1_ref, u2p_ref, e4_ref, r4_ref, w4_ref, cnt_ref, carry_ref):
    i = pl.program_id(0)
    tm = x_ref.shape[0]

    @pl.when(i == 0)
    def _():
        carry_ref[...] = jnp.zeros_like(carry_ref)

    merged = (sga_ref[...].astype(F32) * _dot(ya_ref[...], pa_ref[...])
              + sgb_ref[...].astype(F32) * _dot(yb_ref[...], pb_ref[...]))
    x1 = x_ref[...] + gt_ref[...] * _dot(merged.astype(BF16), wo_ref[...])
    x1_ref[...] = x1
    u2 = _rms(x1, g_ref[...]) * (1.0 + sc_ref[...]) + sh_ref[...]
    halves = _pack_rows(u2)
    u2p_ref[0] = halves[0]
    u2p_ref[1] = halves[1]

    logits = _dot(u2.astype(BF16), wr_ref[...]) + br_ref[...]
    eidx = lax.broadcasted_iota(jnp.int32, logits.shape, 1)
    work = logits
    sel = jnp.zeros(logits.shape, jnp.bool_)
    picks, vals = [], []
    for k in range(TOP_K):
        m = jnp.max(work, axis=-1, keepdims=True)
        first = jnp.min(jnp.where(work == m, eidx, N_EXPERTS), axis=-1, keepdims=True)
        hit = eidx == first
        sel = sel | hit
        work = jnp.where(hit, -jnp.inf, work)
        picks.append(first)
        vals.append(m)
    exps = [jnp.exp(v - vals[0]) for v in vals]
    denom = exps[0] + exps[1] + exps[2] + exps[3]

    r = lax.broadcasted_iota(jnp.int32, (tm, tm), 0)
    cc = lax.broadcasted_iota(jnp.int32, (tm, tm), 1)
    before = (cc < r).astype(BF16)
    chosen = sel.astype(BF16)
    rank = _dot(before, chosen) + carry_ref[...]
    cnt = carry_ref[...] + jnp.sum(chosen.astype(F32), axis=0, keepdims=True)
    carry_ref[...] = cnt
    cnt_ref[...] = cnt
    for k in range(TOP_K):
        e4_ref[:, k:k + 1] = picks[k]
        rk = jnp.sum(jnp.where(eidx == picks[k], rank, 0.0), axis=-1, keepdims=True)
        r4_ref[:, k:k + 1] = rk.astype(jnp.int32)
        w4_ref[:, k:k + 1] = exps[k] / denom


def _outproj(x2, ya, yb, sga, sgb, pa, pb, wo, gt, g, sh, sc, wr, br, tm):
    s = x2.shape[0]
    row = lambda n: pl.BlockSpec((tm, n), lambda i: (i, 0))
    consts = [pa, pb, wo, gt, g, sh, sc, wr, br]
    return pl.pallas_call(
        _outproj_kernel,
        grid=(s // tm,),
        in_specs=[row(D_MODEL), row(SWA_Q), row(FOX_W), row(D_MODEL), row(D_MODEL)] + [_const_spec(a.shape) for a in consts],
        out_specs=[row(D_MODEL), pl.BlockSpec((2, tm, PACK_W), lambda i: (0, i, 0)), row(TOP_K), row(TOP_K), row(TOP_K),
                   _const_spec((1, N_EXPERTS))],
        out_shape=[jax.ShapeDtypeStruct((s, D_MODEL), F32),
                   jax.ShapeDtypeStruct((2, s, PACK_W), jnp.uint32),
                   jax.ShapeDtypeStruct((s, TOP_K), jnp.int32),
                   jax.ShapeDtypeStruct((s, TOP_K), jnp.int32),
                   jax.ShapeDtypeStruct((s, TOP_K), F32),
                   jax.ShapeDtypeStruct((1, N_EXPERTS), F32)],
        scratch_shapes=[pltpu.VMEM((1, N_EXPERTS), F32)],
        compiler_params=_params(("arbitrary",)),
        name="out_proj_router",
    )(x2, ya, yb, sga, sgb, *consts)


def _sc_mesh():
    return plsc.VectorSubcoreMesh(core_axis_name="core", subcore_axis_name="subcore")


def _sc_dispatch(rows, idx, n_out):
    n, width = rows.shape

    @functools.partial(pl.kernel, out_type=jax.ShapeDtypeStruct((n_out, width), rows.dtype), mesh=_sc_mesh(),
                       scratch_types=[])
    def dispatch(x_hbm, i_hbm, o_hbm):
        def body(x_vmem, i_vmem):
            for k in range(TOP_K):
                pltpu.sync_copy(x_vmem, o_hbm.at[i_vmem.at[k]])

        pltpu.emit_pipeline(
            body, grid=(n // SC_WINDOW,),
            in_specs=[pl.BlockSpec((SC_WINDOW, width), lambda i: (i, 0)),
                      pl.BlockSpec((TOP_K, SC_WINDOW), lambda i: (0, i))],
            out_specs=[], core_axis_name=("core", "subcore"), dimension_semantics=(pltpu.PARALLEL,),
        )(x_hbm, i_hbm)

    return dispatch(rows, idx)


def _sc_gather(table, idx):
    n = idx.shape[1]
    width = table.shape[1]

    @functools.partial(pl.kernel, out_type=jax.ShapeDtypeStruct((n, width), table.dtype), mesh=_sc_mesh(),
                       scratch_types=[])
    def gather(t_hbm, i_hbm, o_hbm):
        def body(i_vmem, o_vmem):
            pltpu.sync_copy(t_hbm.at[i_vmem.at[0]], o_vmem)

        pltpu.emit_pipeline(
            body, grid=(n // SC_WINDOW,),
            in_specs=[pl.BlockSpec((1, SC_WINDOW), lambda i: (0, i))],
            out_specs=[pl.BlockSpec((SC_WINDOW, width), lambda i: (i, 0))],
            core_axis_name=("core", "subcore"), dimension_semantics=(pltpu.PARALLEL,),
        )(i_hbm, o_hbm)

    return gather(table, idx)


def _moe_kernel(te_ref, first_ref, nact_ref, xs_ref, w1_ref, b1g_ref, b1l_ref, w2_ref, b2_ref, y_ref,
                wg_ref, wl_ref, w2b_ref, tmp_ref):
    i = pl.program_id(0)
    live = i < nact_ref[0]

    @pl.when(jnp.logical_and(live, first_ref[i] == 1))
    def _():
        half = PREP_COLS // 2
        for c in range(2 * D_FF // PREP_COLS):
            tr = w1_ref[0, :, c * PREP_COLS:(c + 1) * PREP_COLS].T
            rows = slice(c * half, (c + 1) * half)
            for j in range(D_MODEL // LANES):
                lanes = slice(j * LANES, (j + 1) * LANES)
                tmp_ref[j] = tr[:, lanes]
                wg_ref[rows, lanes] = tmp_ref[j, pl.ds(0, half, stride=2), :].astype(BF16)
                wl_ref[rows, lanes] = tmp_ref[j, pl.ds(1, half, stride=2), :].astype(BF16)
        w2b_ref[...] = w2_ref[0].astype(BF16)

    @pl.when(live)
    def _():
        chunks = []
        for j in range(2):
            lo, hi = _unpack_words(xs_ref[j])
            chunks += [lo.astype(BF16), hi.astype(BF16)]
        hg = b1g_ref[0]
        hl = b1l_ref[0]
        for c, xc in enumerate(chunks):
            cols = slice(c * PACK_W, (c + 1) * PACK_W)
            hg = hg + _dot_nt(xc, wg_ref[:, cols])
            hl = hl + _dot_nt(xc, wl_ref[:, cols])
        glu = jnp.minimum(hg, SWIGLU_LIMIT)
        lin = jnp.clip(hl, -SWIGLU_LIMIT, SWIGLU_LIMIT)
        a = glu * jax.nn.sigmoid(SWIGLU_ALPHA * glu) * (lin + 1.0)
        y = _dot(a.astype(BF16), w2b_ref[...]) + b2_ref[0]
        halves = _pack_rows(y)
        y_ref[0] = halves[0]
        y_ref[1] = halves[1]


def _moe_routed(tile_expert, tile_first, n_active, xs, w1, b1g, b1l, w2, b2):
    n_rows = xs.shape[1]
    n_tiles = n_rows // MOE_TILE
    live = lambda i, na: jnp.minimum(i, na[0] - 1)
    rows_spec = pl.BlockSpec((2, MOE_TILE, PACK_W), lambda i, te, tf, na: (0, live(i, na), 0))
    wsp = lambda a: pl.BlockSpec((1,) + a.shape[1:], lambda i, te, tf, na: (te[live(i, na)], 0, 0))
    grid_spec = pltpu.PrefetchScalarGridSpec(
        num_scalar_prefetch=3,
        grid=(n_tiles,),
        in_specs=[rows_spec, wsp(w1), wsp(b1g), wsp(b1l), wsp(w2), wsp(b2)],
        out_specs=rows_spec,
        scratch_shapes=[pltpu.VMEM((D_FF, D_MODEL), BF16), pltpu.VMEM((D_FF, D_MODEL), BF16),
                        pltpu.VMEM((D_FF, D_MODEL), BF16), pltpu.VMEM((D_MODEL // LANES, PREP_COLS, LANES), F32)],
    )
    return pl.pallas_call(
        _moe_kernel,
        grid_spec=grid_spec,
        out_shape=jax.ShapeDtypeStruct(xs.shape, jnp.uint32),
        compiler_params=_params(("arbitrary",)),
        name="moe_routed",
    )(tile_expert, tile_first, n_active, xs, w1, b1g, b1l, w2, b2)


def _final_kernel(x1_ref, yg_ref, w4_ref, gt_ref, gf_ref, o_ref):
    w4 = w4_ref[...]
    cols = []
    for j in range(2):
        lo_acc = hi_acc = None
        for k in range(TOP_K):
            lo, hi = _unpack_words(yg_ref[j, k])
            wk = w4[:, k:k + 1]
            lo_acc = wk * lo if lo_acc is None else lo_acc + wk * lo
            hi_acc = wk * hi if hi_acc is None else hi_acc + wk * hi
        cols += [lo_acc, hi_acc]
    moe = jnp.concatenate(cols, axis=1)
    x2 = x1_ref[...] + gt_ref[...] * moe
    o_ref[...] = _rms(x2, gf_ref[...])


def _final(x1, yg, w4, gt, gf, tm):
    s = x1.shape[0]
    row = lambda n: pl.BlockSpec((tm, n), lambda i: (i, 0))
    return pl.pallas_call(
        _final_kernel,
        grid=(s // tm,),
        in_specs=[row(D_MODEL), pl.BlockSpec((2, TOP_K, tm, PACK_W), lambda i: (0, 0, i, 0)), row(TOP_K),
                  _const_spec(gt.shape), _const_spec(gf.shape)],
        out_specs=row(D_MODEL),
        out_shape=jax.ShapeDtypeStruct((s, D_MODEL), F32),
        compiler_params=_params(("arbitrary",)),
        name="combine_final_norm",
    )(x1, yg, w4, gt, gf)


def _routing_tables(e4, r4, counts, s):
    n_rows = TOP_K * s + N_EXPERTS * MOE_TILE
    cnt = counts.reshape(N_EXPERTS).astype(jnp.int32)
    padded = ((cnt + MOE_TILE - 1) // MOE_TILE) * MOE_TILE
    ends = jnp.cumsum(padded)
    starts = ends - padded
    pos = (jnp.take(starts, e4, axis=0) + r4).T
    tile_start = jnp.arange(n_rows // MOE_TILE, dtype=jnp.int32) * MOE_TILE
    tile_expert = jnp.minimum(jnp.sum(tile_start[:, None] >= ends[None, :], axis=1), N_EXPERTS - 1).astype(jnp.int32)
    tile_first = jnp.concatenate([jnp.ones((1,), jnp.int32), (tile_expert[1:] != tile_expert[:-1]).astype(jnp.int32)])
    n_active = (ends[-1:] // MOE_TILE).astype(jnp.int32)
    return pos, tile_expert, tile_first, n_active, n_rows


def kernel(x, c, w_ada, b_ada, g_mix, w_in, b_forget, sinks, rel_bias, w_proj_a, w_proj_b, w_out, g_ffn,
           w_router, b_router, w_e1, b_e1, w_e2, b_e2, g_final):
    b, s, d = x.shape
    assert b == 1 and d == D_MODEL and w_ada.shape[0] == 1
    x2 = x.reshape(s, d)
    tm = min(512, s)

    mod = _ada(c.reshape(d, 1), w_ada[0], b_ada)
    sh_m, sc_m, gt_m, sh_f, sc_f, gt_f = [mod[:, k * d:(k + 1) * d] for k in range(N_MOD)]

    w = w_in[0]
    o_ka, o_va, o_b = SWA_Q, SWA_Q + SWA_KV, SWA_Q + 2 * SWA_KV
    o_f = o_b + 3 * FOX_W
    o_g = o_f + FOX_HEADS
    dup = lambda m: jnp.concatenate([m[:, :HEAD_DIM], m[:, :HEAD_DIM], m[:, HEAD_DIM:], m[:, HEAD_DIM:]], axis=1)
    wa = jnp.concatenate([w[:, :SWA_Q], dup(w[:, o_ka:o_va]), dup(w[:, o_va:o_b])], axis=1).astype(BF16)
    pad = jnp.zeros((d, LANES - FOX_HEADS), F32)
    wb = jnp.concatenate([w[:, o_b:o_b + 2 * FOX_W], w[:, o_f:o_g], pad], axis=1).astype(BF16)
    wvt = w[:, o_b + 2 * FOX_W:o_f].T.astype(BF16)
    wg = w[:, o_g:].astype(BF16)
    qa, kva, qb, kb, vbt, sga, sgb, cum, ca, qn2, kn2 = _inproj(
        x2, g_mix, sh_m, sc_m, wa, wb, wvt, wg, b_forget, tm)

    ya = _swa(rel_bias.reshape(-1), sinks[0], jnp.asarray(_t5_buckets_np()), qa, kva)
    jlo, base = _fox_schedule(cum, qn2, kn2)
    yb = _fox(jlo, base, qb, kb, ca, vbt)

    x1, u2p, e4, r4, w4, counts = _outproj(
        x2, ya, yb, sga, sgb, w_proj_a[0].astype(BF16), w_proj_b[0].astype(BF16), w_out[0].astype(BF16),
        gt_m, g_ffn, sh_f, sc_f, w_router[0].astype(BF16), b_router, tm)

    pos, tile_expert, tile_first, n_active, n_rows = _routing_tables(e4, r4, counts, s)
    pos2 = jnp.concatenate([pos, pos + n_rows], axis=1)
    xs = _sc_dispatch(u2p.reshape(2 * s, PACK_W), pos2, 2 * n_rows).reshape(2, n_rows, PACK_W)

    b1 = b_e1[0].reshape(N_EXPERTS, D_FF, 2)
    b1g = b1[:, None, :, 0]
    b1l = b1[:, None, :, 1]
    ys = _moe_routed(tile_expert, tile_first, n_active, xs, w_e1[0], b1g, b1l, w_e2[0], b_e2[0][:, None, :])

    gather_idx = pos2.reshape(TOP_K, 2, s).transpose(1, 0, 2).reshape(1, -1)
    yg = _sc_gather(ys.reshape(2 * n_rows, PACK_W), gather_idx)
    out = _final(x1, yg.reshape(2, TOP_K, s, PACK_W), w4, gt_f, g_final.reshape(1, d), tm)
    return out.reshape(b, s, d)
```

```python
import functools
import math

import numpy as np
import jax
import jax.numpy as jnp
from jax import lax
from jax.experimental import pallas as pl
from jax.experimental.pallas import tpu as pltpu
from jax.experimental.pallas import tpu_sc as plsc

D_MODEL = 1024
HEAD_DIM = 64
SWA_HEADS = 8
SWA_KV_HEADS = 2
WINDOW = 128
FOX_HEADS = 8
BLOCK = 128
REL_BUCKETS = 32
REL_MAX_DIST = WINDOW
N_EXPERTS = 32
TOP_K = 4
D_FF = D_MODEL
SWIGLU_LIMIT = 7.0
SWIGLU_ALPHA = 1.702
RMS_EPS = 1e-5
N_MOD = 6

SWA_Q = SWA_HEADS * HEAD_DIM
SWA_KV = SWA_KV_HEADS * HEAD_DIM
FOX_W = FOX_HEADS * HEAD_DIM
LANES = 128
N_PAIRS = FOX_HEADS // 2
NEG_BIG = -1e30
LOG2E = math.log2(math.e)
FOX_TILE = 256
FOX_CHUNK = 4
N_SPLIT = 3
SKIP_LOG2 = 127.0
NORM_SLACK = 1.01
PACK_W = 256
MOE_TILE = 512
SC_WINDOW = 128
COMBINE_CHUNKS = 2
PREP_COLS = 256
VMEM_LIMIT = 56 * 1024 * 1024

F32 = jnp.float32
BF16 = jnp.bfloat16
HIGHEST = lax.Precision.HIGHEST


def _dot(a, b):
    return jnp.dot(a, b, preferred_element_type=F32)


def _dot_nt(a, b, precision=None):
    return lax.dot_general(a, b, (((1,), (1,)), ((), ())), preferred_element_type=F32, precision=precision)


def _const_spec(shape):
    nd = len(shape)
    return pl.BlockSpec(shape, lambda *_: (0,) * nd)


def _params(sem):
    return pltpu.CompilerParams(dimension_semantics=sem, vmem_limit_bytes=VMEM_LIMIT)


def _ada_kernel(c_ref, w_ref, b_ref, o_ref):
    c = c_ref[...]
    act = c * jax.nn.sigmoid(c)
    o_ref[...] = jnp.sum(act * w_ref[...], axis=0, keepdims=True) + b_ref[...]


def _ada(c_col, w_ada, b_ada):
    n = w_ada.shape[1]
    tn = 1024
    return pl.pallas_call(
        _ada_kernel,
        grid=(n // tn,),
        in_specs=[_const_spec((D_MODEL, 1)),
                  pl.BlockSpec((D_MODEL, tn), lambda j: (0, j)),
                  pl.BlockSpec((1, tn), lambda j: (0, j))],
        out_specs=pl.BlockSpec((1, tn), lambda j: (0, j)),
        out_shape=jax.ShapeDtypeStruct((1, n), F32),
        compiler_params=_params(("arbitrary",)),
        name="ada_mod",
    )(c_col, w_ada, b_ada)


def _split_bf16(v):
    parts = []
    for _ in range(N_SPLIT):
        p = v.astype(BF16)
        v = v - p.astype(F32)
        parts.append(p)
    return parts


def _inproj_kernel(x_ref, g_ref, sh_ref, sc_ref, wa_ref, wb_ref, wvt_ref, wg_ref, bf_ref, hind_ref, place_ref,
                   qa_ref, kva_ref, vat_ref, qb_ref, kb_ref, vbt_ref, sga_ref, sgb_ref, c_ref, ca_ref, qn_ref, kn_ref,
                   carry_ref):
    i = pl.program_id(0)
    tm = x_ref.shape[0]
    t = FOX_TILE
    nsub = tm // t

    @pl.when(i == 0)
    def _():
        carry_ref[...] = jnp.zeros_like(carry_ref)

    xf = x_ref[...]
    ms = jnp.mean(xf * xf, axis=-1, keepdims=True)
    y = xf * lax.rsqrt(ms + RMS_EPS) * g_ref[...]
    u = y * (1.0 + sc_ref[...]) + sh_ref[...]
    ub = u.astype(BF16)

    za = _dot(ub, wa_ref[...])
    qa_ref[...] = (za[:, :SWA_Q] * (HEAD_DIM ** -0.5 * LOG2E)).astype(BF16)
    kva_ref[...] = za[:, SWA_Q:].astype(BF16)

    zb = _dot(ub, wb_ref[...])
    qb = (zb[:, :FOX_W] * (HEAD_DIM ** -0.5 * LOG2E)).astype(BF16)
    kb = zb[:, FOX_W:2 * FOX_W].astype(BF16)
    qb_ref[...] = qb
    kb_ref[...] = kb
    vt = _dot_nt(wvt_ref[...], ub).astype(BF16)
    vbt_ref[...] = vt[:FOX_W]
    vat_ref[...] = vt[FOX_W:]

    def tile_norm_max(z, o_ref):
        zf = z.astype(F32)
        n2 = _dot((zf * zf).astype(BF16), hind_ref[...])
        for sb in range(nsub):
            o_ref[sb] = jnp.max(n2[sb * t:(sb + 1) * t], axis=0, keepdims=True)

    tile_norm_max(qb, qn_ref)
    tile_norm_max(kb, kn_ref)

    zg = _dot(ub, wg_ref[...])
    sg = jax.nn.sigmoid(zg)
    sga_ref[...] = sg[:, :D_MODEL].astype(BF16)
    sgb_ref[...] = sg[:, D_MODEL:].astype(BF16)

    fb = zb[:, 2 * FOX_W:2 * FOX_W + FOX_HEADS] + bf_ref[...]
    lf = jnp.minimum(fb, 0.0) - jnp.log1p(jnp.exp(-jnp.abs(fb)))
    r = lax.broadcasted_iota(jnp.int32, (t, t), 0)
    cc = lax.broadcasted_iota(jnp.int32, (t, t), 1)
    lower = (cc <= r).astype(BF16)
    carry = carry_ref[...]
    for sb in range(nsub):
        rows = slice(sb * t, (sb + 1) * t)
        local = sum(_dot(lower, p) for p in _split_bf16(lf[rows]))
        c_ref[rows, :] = local + carry
        carry = carry + local[t - 1:t, :]
        aug = sum(_dot(p, place_ref[k]) for k, p in enumerate(_split_bf16(local * LOG2E)))
        ca_ref[rows, :] = aug.astype(BF16)
    carry_ref[...] = carry


def _inproj(x2, g, sh, sc, wa, wb, wvt, wg, bfor, tm):
    s = x2.shape[0]
    nsub = tm // FOX_TILE
    row = lambda n: pl.BlockSpec((tm, n), lambda i: (i, 0))
    hind = np.zeros((FOX_W, FOX_HEADS), np.float32)
    hind[np.arange(FOX_W), np.arange(FOX_W) // HEAD_DIM] = 1.0
    place = np.zeros((N_SPLIT, FOX_HEADS, N_PAIRS * LANES), np.float32)
    for k in range(N_SPLIT):
        for h in range(FOX_HEADS):
            place[k, h, (h // 2) * LANES + N_SPLIT * (h % 2) + k] = 1.0
    hind = jnp.asarray(hind, BF16)
    place = jnp.asarray(place, BF16)
    out_shape = [
        jax.ShapeDtypeStruct((s, SWA_Q), BF16),
        jax.ShapeDtypeStruct((s, 2 * LANES), BF16),
        jax.ShapeDtypeStruct((SWA_KV, s), BF16),
        jax.ShapeDtypeStruct((s, FOX_W), BF16),
        jax.ShapeDtypeStruct((s, FOX_W), BF16),
        jax.ShapeDtypeStruct((FOX_W, s), BF16),
        jax.ShapeDtypeStruct((s, D_MODEL), BF16),
        jax.ShapeDtypeStruct((s, D_MODEL), BF16),
        jax.ShapeDtypeStruct((s, FOX_HEADS), F32),
        jax.ShapeDtypeStruct((s, N_PAIRS * LANES), BF16),
        jax.ShapeDtypeStruct((s // FOX_TILE, 1, FOX_HEADS), F32),
        jax.ShapeDtypeStruct((s // FOX_TILE, 1, FOX_HEADS), F32),
    ]
    stat = pl.BlockSpec((nsub, 1, FOX_HEADS), lambda i: (i, 0, 0))
    out_specs = [row(SWA_Q), row(2 * LANES), pl.BlockSpec((SWA_KV, tm), lambda i: (0, i)),
                 row(FOX_W), row(FOX_W), pl.BlockSpec((FOX_W, tm), lambda i: (0, i)),
                 row(D_MODEL), row(D_MODEL), row(FOX_HEADS), row(N_PAIRS * LANES), stat, stat]
    consts = [g, sh, sc, wa, wb, wvt, wg, bfor, hind, place]
    return pl.pallas_call(
        _inproj_kernel,
        grid=(s // tm,),
        in_specs=[row(D_MODEL)] + [_const_spec(a.shape) for a in consts],
        out_specs=out_specs,
        out_shape=out_shape,
        scratch_shapes=[pltpu.VMEM((1, FOX_HEADS), F32)],
        compiler_params=_params(("arbitrary",)),
        name="in_proj",
    )(x2, *consts)


def _t5_buckets_np():
    qi = np.arange(BLOCK)[:, None]
    kj = np.arange(2 * BLOCK)[None, :]
    dist = BLOCK + qi - kj
    n = np.maximum(dist, 0)
    max_exact = REL_BUCKETS // 2
    nf = np.maximum(n, 1).astype(np.float32)
    large = max_exact + (np.log(nf / np.float32(max_exact)) / np.float32(math.log(REL_MAX_DIST / max_exact))
                         * np.float32(REL_BUCKETS - max_exact)).astype(np.int32)
    large = np.minimum(large, REL_BUCKETS - 1)
    bucket = np.where(n < max_exact, n, large).astype(np.int32)
    band = (dist >= 0) & (dist < WINDOW)
    return np.where(band, bucket, -1).astype(np.int32)


def _swa_kernel(rel_ref, sink_ref, bkt_ref, q_ref, kc_ref, kp_ref, vc_ref, vp_ref, o_ref, bias_ref):
    n = pl.program_id(0)

    @pl.when(n == 0)
    def _():
        bkt = bkt_ref[...]
        prev = lax.broadcasted_iota(jnp.int32, bkt.shape, 0) < BLOCK
        for h in range(SWA_HEADS):
            acc = jnp.full(bkt.shape, NEG_BIG, F32)
            for b in range(REL_BUCKETS):
                acc = jnp.where(bkt == b, rel_ref[b * SWA_HEADS + h] * LOG2E, acc)
            cols = slice((h % 2) * BLOCK, (h % 2 + 1) * BLOCK)
            bias_ref[0, h // 2, :, cols] = acc
            bias_ref[1, h // 2, :, cols] = jnp.where(prev, NEG_BIG, acc)

    first = jnp.where(n == 0, 1, 0)
    lane = lax.broadcasted_iota(jnp.int32, (BLOCK, LANES), 1)
    col2 = lax.broadcasted_iota(jnp.int32, (1, 2 * BLOCK), 1)
    k2 = jnp.concatenate([kp_ref[...], kc_ref[...]], axis=0)
    v2 = jnp.concatenate([vp_ref[...], vc_ref[...]], axis=1)
    pairs = range(SWA_HEADS // 2)
    scores = []
    for p in pairs:
        qp = q_ref[:, p * LANES:(p + 1) * LANES]
        zero = jnp.zeros_like(qp)
        qs = jnp.concatenate([jnp.where(lane < HEAD_DIM, qp, zero), jnp.where(lane >= HEAD_DIM, qp, zero)], axis=0)
        g = p // 2
        scores.append(_dot_nt(k2[:, g * LANES:(g + 1) * LANES], qs))
    weights = []
    for p, s in zip(pairs, scores):
        s = s + bias_ref[first, p]
        sink = jnp.where(col2 < BLOCK, sink_ref[2 * p], sink_ref[2 * p + 1]) * LOG2E
        m = jnp.maximum(jnp.max(s, axis=0, keepdims=True), sink)
        e = jnp.exp2(s - m)
        denom = jnp.sum(e, axis=0, keepdims=True) + jnp.exp2(sink - m)
        weights.append((e.astype(BF16), denom))
    outs = []
    for p, (e, denom) in zip(pairs, weights):
        g = p // 2
        o = _dot(v2[g * HEAD_DIM:(g + 1) * HEAD_DIM], e) / denom
        outs += [o[:, :BLOCK], o[:, BLOCK:]]
    o_ref[...] = jnp.concatenate(outs, axis=0).T.astype(BF16)


def _swa(rel_flat, sinks, bkt_t, qa, ka2, vat):
    s = qa.shape[0]
    nb = s // BLOCK
    smem = pl.BlockSpec(memory_space=pltpu.SMEM)
    prev = lambda n: jnp.maximum(n - 1, 0)
    return pl.pallas_call(
        _swa_kernel,
        grid=(nb,),
        in_specs=[smem, smem, _const_spec(bkt_t.shape),
                  pl.BlockSpec((BLOCK, SWA_Q), lambda n: (n, 0)),
                  pl.BlockSpec((BLOCK, 2 * LANES), lambda n: (n, 0)),
                  pl.BlockSpec((BLOCK, 2 * LANES), lambda n: (prev(n), 0)),
                  pl.BlockSpec((SWA_KV, BLOCK), lambda n: (0, n)),
                  pl.BlockSpec((SWA_KV, BLOCK), lambda n: (0, prev(n)))],
        out_specs=pl.BlockSpec((BLOCK, SWA_Q), lambda n: (n, 0)),
        out_shape=jax.ShapeDtypeStruct((s, SWA_Q), BF16),
        scratch_shapes=[pltpu.VMEM((2, SWA_HEADS // 2, 2 * BLOCK, 2 * BLOCK), F32)],
        compiler_params=_params(("arbitrary",)),
        name="swa_attn",
    )(rel_flat, sinks, bkt_t, qa, ka2, ka2, vat, vat)


def _fox_kernel(jlo_ref, base_ref, q_ref, k_ref, ca_ref, vt_ref, o_ref):
    p = pl.program_id(0)
    i = pl.program_id(1)
    t = FOX_TILE
    lane = lax.broadcasted_iota(jnp.int32, (t, LANES), 1)
    q = q_ref[...]
    wq = []
    for hh in range(2):
        in_head = (lane >= hh * HEAD_DIM) & (lane < (hh + 1) * HEAD_DIM)
        qm = jnp.where(in_head, q, jnp.zeros_like(q))
        sel = (lane >= N_SPLIT * hh) & (lane < N_SPLIT * (hh + 1))
        aug = jnp.where(sel, -1.0, 0.0).astype(BF16)
        wq.append(jnp.concatenate([qm, aug], axis=1))
    key = lax.broadcasted_iota(jnp.int32, (t, t), 0)
    qry = lax.broadcasted_iota(jnp.int32, (t, t), 1)
    causal = key <= qry

    def step(j, carry, heads, nsub, diagonal=False):
        start = pl.multiple_of(j * t, t)
        rows = nsub * t
        lhs = jnp.concatenate([k_ref[pl.ds(start, rows), :], ca_ref[pl.ds(start, rows), :]], axis=1)
        scores = [_dot_nt(lhs, wq[hh]) for hh in heads]
        mid = []
        for hh, s, (m, l, acc) in zip(heads, scores, carry):
            h = 2 * p + hh
            parts = [s[k * t:(k + 1) * t] for k in range(nsub)]
            if diagonal:
                parts[-1] = jnp.where(causal, parts[-1], NEG_BIG)
            djs = [base_ref[h, i] - base_ref[h, j + k] for k in range(nsub)]
            m_new = m
            for part, dj in zip(parts, djs):
                m_new = jnp.maximum(m_new, jnp.max(part, axis=0, keepdims=True) + dj)
            alpha = jnp.exp2(m - m_new)
            es = [jnp.exp2(part + (dj - m_new)) for part, dj in zip(parts, djs)]
            l = alpha * l
            for e in es:
                l = l + jnp.sum(e, axis=0, keepdims=True)
            e_all = es[0] if nsub == 1 else jnp.concatenate(es, axis=0)
            mid.append((m_new, l, alpha, acc, e_all.astype(BF16)))
        new = []
        for hh, (m_new, l, alpha, acc, e_all) in zip(heads, mid):
            vt = vt_ref[hh * HEAD_DIM:(hh + 1) * HEAD_DIM, pl.ds(start, rows)]
            new.append((m_new, l, alpha * acc + _dot(vt, e_all)))
        return tuple(new)

    def run_alone(lo, hi, carry, heads):
        n_full = (hi - lo) // FOX_CHUNK
        carry = lax.fori_loop(0, n_full, lambda n, c: step(lo + FOX_CHUNK * n, c, heads, FOX_CHUNK), carry)
        rest = lo + FOX_CHUNK * n_full
        tails = [lambda c: c] + [functools.partial(lambda c, k: step(rest, c, heads, k), k=k)
                                 for k in range(1, FOX_CHUNK)]
        return lax.switch(hi - rest, tails, carry)

    def run_to_diagonal(lo, carry, heads):
        count = i + 1 - lo
        last = (count - 1) % FOX_CHUNK + 1
        n_full = (count - last) // FOX_CHUNK
        carry = lax.fori_loop(0, n_full, lambda n, c: step(lo + FOX_CHUNK * n, c, heads, FOX_CHUNK), carry)
        tails = [functools.partial(lambda c, k: step(i + 1 - k, c, heads, k, diagonal=True), k=k)
                 for k in range(1, FOX_CHUNK + 1)]
        return lax.switch(last - 1, tails, carry)

    lo0 = jlo_ref[2 * p, i]
    lo1 = jlo_ref[2 * p + 1, i]
    lo_both = jnp.maximum(lo0, lo1)
    init = (jnp.full((1, t), NEG_BIG, F32), jnp.zeros((1, t), F32), jnp.zeros((HEAD_DIM, t), F32))
    (c0,) = run_alone(lo0, lo_both, (init,), (0,))
    (c1,) = run_alone(lo1, lo_both, (init,), (1,))
    carry = run_to_diagonal(lo_both, (c0, c1), (0, 1))
    ot = jnp.concatenate([carry[0][2] / carry[0][1], carry[1][2] / carry[1][1]], axis=0)
    o_ref[...] = ot.T.astype(BF16)


def _fox_schedule(cum, qn2, kn2):
    nt = qn2.shape[0]
    t = FOX_TILE
    bq = jnp.sqrt(qn2.reshape(nt, FOX_HEADS)) * NORM_SLACK
    bk = jnp.sqrt(kn2.reshape(nt, FOX_HEADS)) * NORM_SLACK
    c2 = cum.reshape(nt, t, FOX_HEADS) * LOG2E
    c_first, c_last = c2[:, 0, :], c2[:, t - 1, :]
    upper = (bq[:, None, :] * bk[None, :, :] + (bq * bk)[:, None, :] + c_first[:, None, :] - c_last[None, :, :])
    ii = jnp.arange(nt)[:, None, None]
    jj = jnp.arange(nt)[None, :, None]
    needed = (jj <= ii) & ((upper >= -SKIP_LOG2) | (jj == ii))
    jlo = jnp.min(jnp.where(needed, jj, nt), axis=1).T.astype(jnp.int32)
    base = jnp.concatenate([jnp.zeros((1, FOX_HEADS), F32), c_last[:-1]], axis=0).T
    return jlo, base


def _fox(jlo, base, qb, kb, ca, vbt):
    s = qb.shape[0]
    t = FOX_TILE
    grid_spec = pltpu.PrefetchScalarGridSpec(
        num_scalar_prefetch=2,
        grid=(N_PAIRS, s // t),
        in_specs=[pl.BlockSpec((t, LANES), lambda p, i, *_: (i, p)),
                  pl.BlockSpec((s, LANES), lambda p, i, *_: (0, p)),
                  pl.BlockSpec((s, LANES), lambda p, i, *_: (0, p)),
                  pl.BlockSpec((LANES, s), lambda p, i, *_: (p, 0))],
        out_specs=pl.BlockSpec((t, LANES), lambda p, i, *_: (i, p)),
    )
    return pl.pallas_call(
        _fox_kernel,
        grid_spec=grid_spec,
        out_shape=jax.ShapeDtypeStruct((s, FOX_W), BF16),
        compiler_params=_params(("arbitrary", "arbitrary")),
        name="fox_attn",
    )(jlo, base, qb, kb, ca, vbt)


def _rms(x, g):
    return x * lax.rsqrt(jnp.mean(x * x, axis=-1, keepdims=True) + RMS_EPS) * g


def _pack_rows(v):
    halves = []
    for j in range(2):
        lo = v[:, (2 * j) * PACK_W:(2 * j + 1) * PACK_W].astype(BF16).astype(F32)
        hi = v[:, (2 * j + 1) * PACK_W:(2 * j + 2) * PACK_W].astype(BF16).astype(F32)
        lo_bits = lax.bitcast_convert_type(lo, jnp.uint32)
        hi_bits = lax.bitcast_convert_type(hi, jnp.uint32)
        halves.append(hi_bits | (lo_bits >> 16))
    return halves


def _unpack_words(w):
    lo = lax.bitcast_convert_type(w << 16, F32)
    hi = lax.bitcast_convert_type(w & jnp.uint32(0xFFFF0000), F32)
    return lo, hi


def _outproj_kernel(x_ref, ya_ref, yb_ref, sga_ref, sgb_ref, pa_ref, pb_ref, wo_ref, gt_ref, g_ref, sh_ref, sc_ref,
                    wr_ref, br_ref, x1_ref, u2p_ref, e4_ref, r4_ref, w4_ref, cnt_ref, carry_ref):
    i = pl.program_id(0)
    tm = x_ref.shape[0]

    @pl.when(i == 0)
    def _():
        carry_ref[...] = jnp.zeros_like(carry_ref)

    merged = (sga_ref[...].astype(F32) * _dot(ya_ref[...], pa_ref[...])
              + sgb_ref[...].astype(F32) * _dot(yb_ref[...], pb_ref[...]))
    x1 = x_ref[...] + gt_ref[...] * _dot(merged.astype(BF16), wo_ref[...])
    x1_ref[...] = x1
    u2 = _rms(x1, g_ref[...]) * (1.0 + sc_ref[...]) + sh_ref[...]
    halves = _pack_rows(u2)
    u2p_ref[0] = halves[0]
    u2p_ref[1] = halves[1]

    logits = _dot(u2.astype(BF16), wr_ref[...]) + br_ref[...]
    eidx = lax.broadcasted_iota(jnp.int32, logits.shape, 1)
    work = logits
    sel = jnp.zeros(logits.shape, jnp.bool_)
    picks, vals = [], []
    for k in range(TOP_K):
        m = jnp.max(work, axis=-1, keepdims=True)
        first = jnp.min(jnp.where(work == m, eidx, N_EXPERTS), axis=-1, keepdims=True)
        hit = eidx == first
        sel = sel | hit
        work = jnp.where(hit, -jnp.inf, work)
        picks.append(first)
        vals.append(m)
    exps = [jnp.exp(v - vals[0]) for v in vals]
    denom = exps[0] + exps[1] + exps[2] + exps[3]

    r = lax.broadcasted_iota(jnp.int32, (tm, tm), 0)
    cc = lax.broadcasted_iota(jnp.int32, (tm, tm), 1)
    before = (cc < r).astype(BF16)
    chosen = sel.astype(BF16)
    rank = _dot(before, chosen) + carry_ref[...]
    cnt = carry_ref[...] + jnp.sum(chosen.astype(F32), axis=0, keepdims=True)
    carry_ref[...] = cnt
    cnt_ref[...] = cnt
    for k in range(TOP_K):
        e4_ref[:, k:k + 1] = picks[k]
        rk = jnp.sum(jnp.where(eidx == picks[k], rank, 0.0), axis=-1, keepdims=True)
        r4_ref[:, k:k + 1] = rk.astype(jnp.int32)
        w4_ref[:, k:k + 1] = exps[k] / denom


def _outproj(x2, ya, yb, sga, sgb, pa, pb, wo, gt, g, sh, sc, wr, br, tm):
    s = x2.shape[0]
    row = lambda n: pl.BlockSpec((tm, n), lambda i: (i, 0))
    consts = [pa, pb, wo, gt, g, sh, sc, wr, br]
    return pl.pallas_call(
        _outproj_kernel,
        grid=(s // tm,),
        in_specs=[row(D_MODEL), row(SWA_Q), row(FOX_W), row(D_MODEL), row(D_MODEL)] + [_const_spec(a.shape) for a in consts],
        out_specs=[row(D_MODEL), pl.BlockSpec((2, tm, PACK_W), lambda i: (0, i, 0)), row(TOP_K), row(TOP_K), row(TOP_K),
                   _const_spec((1, N_EXPERTS))],
        out_shape=[jax.ShapeDtypeStruct((s, D_MODEL), F32),
                   jax.ShapeDtypeStruct((2, s, PACK_W), jnp.uint32),
                   jax.ShapeDtypeStruct((s, TOP_K), jnp.int32),
                   jax.ShapeDtypeStruct((s, TOP_K), jnp.int32),
                   jax.ShapeDtypeStruct((s, TOP_K), F32),
                   jax.ShapeDtypeStruct((1, N_EXPERTS), F32)],
        scratch_shapes=[pltpu.VMEM((1, N_EXPERTS), F32)],
        compiler_params=_params(("arbitrary",)),
        name="out_proj_router",
    )(x2, ya, yb, sga, sgb, *consts)


def _sc_mesh():
    return plsc.VectorSubcoreMesh(core_axis_name="core", subcore_axis_name="subcore")


def _sc_dispatch(rows, idx, n_out):
    n, width = rows.shape

    @functools.partial(pl.kernel, out_type=jax.ShapeDtypeStruct((n_out, width), rows.dtype), mesh=_sc_mesh(),
                       scratch_types=[])
    def dispatch(x_hbm, i_hbm, o_hbm):
        def body(x_vmem, i_vmem):
            for k in range(TOP_K):
                pltpu.sync_copy(x_vmem, o_hbm.at[i_vmem.at[k]])

        pltpu.emit_pipeline(
            body, grid=(n // SC_WINDOW,),
            in_specs=[pl.BlockSpec((SC_WINDOW, width), lambda i: (i, 0)),
                      pl.BlockSpec((TOP_K, SC_WINDOW), lambda i: (0, i))],
            out_specs=[], core_axis_name=("core", "subcore"), dimension_semantics=(pltpu.PARALLEL,),
        )(x_hbm, i_hbm)

    return dispatch(rows, idx)


def _sc_gather(table, idx):
    n = idx.shape[1]
    width = table.shape[1]

    @functools.partial(pl.kernel, out_type=jax.ShapeDtypeStruct((n, width), table.dtype), mesh=_sc_mesh(),
                       scratch_types=[])
    def gather(t_hbm, i_hbm, o_hbm):
        def body(i_vmem, o_vmem):
            pltpu.sync_copy(t_hbm.at[i_vmem.at[0]], o_vmem)

        pltpu.emit_pipeline(
            body, grid=(n // SC_WINDOW,),
            in_specs=[pl.BlockSpec((1, SC_WINDOW), lambda i: (0, i))],
            out_specs=[pl.BlockSpec((SC_WINDOW, width), lambda i: (i, 0))],
            core_axis_name=("core", "subcore"), dimension_semantics=(pltpu.PARALLEL,),
        )(i_hbm, o_hbm)

    return gather(table, idx)


def _moe_kernel(te_ref, first_ref, nact_ref, xs_ref, w1_ref, b1g_ref, b1l_ref, w2_ref, b2_ref, y_ref,
                wg_ref, wl_ref, w2b_ref, tmp_ref):
    i = pl.program_id(0)
    live = i < nact_ref[0]

    @pl.when(jnp.logical_and(live, first_ref[i] == 1))
    def _():
        half = PREP_COLS // 2
        for c in range(2 * D_FF // PREP_COLS):
            tr = w1_ref[0, :, c * PREP_COLS:(c + 1) * PREP_COLS].T
            rows = slice(c * half, (c + 1) * half)
            for j in range(D_MODEL // LANES):
                lanes = slice(j * LANES, (j + 1) * LANES)
                tmp_ref[j] = tr[:, lanes]
                wg_ref[rows, lanes] = tmp_ref[j, pl.ds(0, half, stride=2), :].astype(BF16)
                wl_ref[rows, lanes] = tmp_ref[j, pl.ds(1, half, stride=2), :].astype(BF16)
        w2b_ref[...] = w2_ref[0].astype(BF16)

    @pl.when(live)
    def _():
        chunks = []
        for j in range(2):
            lo, hi = _unpack_words(xs_ref[j])
            chunks += [lo.astype(BF16), hi.astype(BF16)]
        hg = b1g_ref[0]
        hl = b1l_ref[0]
        for c, xc in enumerate(chunks):
            cols = slice(c * PACK_W, (c + 1) * PACK_W)
            hg = hg + _dot_nt(xc, wg_ref[:, cols])
            hl = hl + _dot_nt(xc, wl_ref[:, cols])
        glu = jnp.minimum(hg, SWIGLU_LIMIT)
        lin = jnp.clip(hl, -SWIGLU_LIMIT, SWIGLU_LIMIT)
        a = glu * jax.nn.sigmoid(SWIGLU_ALPHA * glu) * (lin + 1.0)
        y = _dot(a.astype(BF16), w2b_ref[...]) + b2_ref[0]
        halves = _pack_rows(y)
        y_ref[0] = halves[0]
        y_ref[1] = halves[1]


def _moe_routed(tile_expert, tile_first, n_active, xs, w1, b1g, b1l, w2, b2):
    n_rows = xs.shape[1]
    n_tiles = n_rows // MOE_TILE
    live = lambda i, na: jnp.minimum(i, na[0] - 1)
    rows_spec = pl.BlockSpec((2, MOE_TILE, PACK_W), lambda i, te, tf, na: (0, live(i, na), 0))
    wsp = lambda a: pl.BlockSpec((1,) + a.shape[1:], lambda i, te, tf, na: (te[live(i, na)], 0, 0))
    grid_spec = pltpu.PrefetchScalarGridSpec(
        num_scalar_prefetch=3,
        grid=(n_tiles,),
        in_specs=[rows_spec, wsp(w1), wsp(b1g), wsp(b1l), wsp(w2), wsp(b2)],
        out_specs=rows_spec,
        scratch_shapes=[pltpu.VMEM((D_FF, D_MODEL), BF16), pltpu.VMEM((D_FF, D_MODEL), BF16),
                        pltpu.VMEM((D_FF, D_MODEL), BF16), pltpu.VMEM((D_MODEL // LANES, PREP_COLS, LANES), F32)],
    )
    return pl.pallas_call(
        _moe_kernel,
        grid_spec=grid_spec,
        out_shape=jax.ShapeDtypeStruct(xs.shape, jnp.uint32),
        compiler_params=_params(("arbitrary",)),
        name="moe_routed",
    )(tile_expert, tile_first, n_active, xs, w1, b1g, b1l, w2, b2)


def _final_kernel(x1_ref, yg_ref, w4_ref, gt_ref, gf_ref, *rest):
    o_ref = rest[-1]
    w4 = w4_ref[...]
    cols = []
    for j in range(2):
        lo_acc = hi_acc = None
        for k in range(TOP_K):
            lo, hi = _unpack_words(yg_ref[j, k])
            wk = w4[:, k:k + 1]
            lo_acc = wk * lo if lo_acc is None else lo_acc + wk * lo
            hi_acc = wk * hi if hi_acc is None else hi_acc + wk * hi
        cols += [lo_acc, hi_acc]
    moe = jnp.concatenate(cols, axis=1)
    x2 = x1_ref[...] + gt_ref[...] * moe
    o_ref[...] = _rms(x2, gf_ref[...])


def _final(x1, yg, w4, gt, gf, tm, first_tile, prev_out):
    s = x1.shape[0]
    n_tiles = yg.shape[2] // tm
    row = lambda n: pl.BlockSpec((tm, n), lambda i: (i + first_tile, 0))
    in_specs = [row(D_MODEL), pl.BlockSpec((2, TOP_K, tm, PACK_W), lambda i: (0, 0, i, 0)), row(TOP_K),
                _const_spec(gt.shape), _const_spec(gf.shape)]
    args = [x1, yg, w4, gt, gf]
    aliases = {}
    if prev_out is not None:
        in_specs.append(pl.BlockSpec(memory_space=pl.ANY))
        args.append(prev_out)
        aliases = {len(args) - 1: 0}
    return pl.pallas_call(
        _final_kernel,
        grid=(n_tiles,),
        in_specs=in_specs,
        out_specs=row(D_MODEL),
        out_shape=jax.ShapeDtypeStruct((s, D_MODEL), F32),
        input_output_aliases=aliases,
        compiler_params=_params(("arbitrary",)),
        name="combine_final_norm",
    )(*args)


def _routing_tables(e4, r4, counts, s):
    n_rows = TOP_K * s + N_EXPERTS * MOE_TILE
    cnt = counts.reshape(N_EXPERTS).astype(jnp.int32)
    padded = ((cnt + MOE_TILE - 1) // MOE_TILE) * MOE_TILE
    ends = jnp.cumsum(padded)
    starts = ends - padded
    pos = (jnp.take(starts, e4, axis=0) + r4).T
    tile_start = jnp.arange(n_rows // MOE_TILE, dtype=jnp.int32) * MOE_TILE
    tile_expert = jnp.minimum(jnp.sum(tile_start[:, None] >= ends[None, :], axis=1), N_EXPERTS - 1).astype(jnp.int32)
    tile_first = jnp.concatenate([jnp.ones((1,), jnp.int32), (tile_expert[1:] != tile_expert[:-1]).astype(jnp.int32)])
    n_active = (ends[-1:] // MOE_TILE).astype(jnp.int32)
    return pos, tile_expert, tile_first, n_active, n_rows


def kernel(x, c, w_ada, b_ada, g_mix, w_in, b_forget, sinks, rel_bias, w_proj_a, w_proj_b, w_out, g_ffn,
           w_router, b_router, w_e1, b_e1, w_e2, b_e2, g_final):
    b, s, d = x.shape
    assert b == 1 and d == D_MODEL and w_ada.shape[0] == 1
    x2 = x.reshape(s, d)
    tm = min(512, s)

    mod = _ada(c.reshape(d, 1), w_ada[0], b_ada)
    sh_m, sc_m, gt_m, sh_f, sc_f, gt_f = [mod[:, k * d:(k + 1) * d] for k in range(N_MOD)]

    w = w_in[0]
    o_ka, o_va, o_b = SWA_Q, SWA_Q + SWA_KV, SWA_Q + 2 * SWA_KV
    o_f = o_b + 3 * FOX_W
    o_g = o_f + FOX_HEADS
    dup = lambda m: jnp.concatenate([m[:, :HEAD_DIM], m[:, :HEAD_DIM], m[:, HEAD_DIM:], m[:, HEAD_DIM:]], axis=1)
    wa = jnp.concatenate([w[:, :SWA_Q], dup(w[:, o_ka:o_va])], axis=1).astype(BF16)
    pad = jnp.zeros((d, LANES - FOX_HEADS), F32)
    wb = jnp.concatenate([w[:, o_b:o_b + 2 * FOX_W], w[:, o_f:o_g], pad], axis=1).astype(BF16)
    wvt = jnp.concatenate([w[:, o_b + 2 * FOX_W:o_f], w[:, o_va:o_b]], axis=1).T.astype(BF16)
    wg = w[:, o_g:].astype(BF16)
    qa, ka2, vat, qb, kb, vbt, sga, sgb, cum, ca, qn2, kn2 = _inproj(
        x2, g_mix, sh_m, sc_m, wa, wb, wvt, wg, b_forget, tm)

    ya = _swa(rel_bias.reshape(-1), sinks[0], jnp.asarray(_t5_buckets_np().T), qa, ka2, vat)
    jlo, base = _fox_schedule(cum, qn2, kn2)
    yb = _fox(jlo, base, qb, kb, ca, vbt)

    x1, u2p, e4, r4, w4, counts = _outproj(
        x2, ya, yb, sga, sgb, w_proj_a[0].astype(BF16), w_proj_b[0].astype(BF16), w_out[0].astype(BF16),
        gt_m, g_ffn, sh_f, sc_f, w_router[0].astype(BF16), b_router, tm)

    pos, tile_expert, tile_first, n_active, n_rows = _routing_tables(e4, r4, counts, s)
    pos2 = jnp.concatenate([pos, pos + n_rows], axis=1)
    xs = _sc_dispatch(u2p.reshape(2 * s, PACK_W), pos2, 2 * n_rows).reshape(2, n_rows, PACK_W)

    b1 = b_e1[0].reshape(N_EXPERTS, D_FF, 2)
    b1g = b1[:, None, :, 0]
    b1l = b1[:, None, :, 1]
    ys = _moe_routed(tile_expert, tile_first, n_active, xs, w_e1[0], b1g, b1l, w_e2[0], b_e2[0][:, None, :])

    gather_idx = pos2.reshape(TOP_K, 2, s).transpose(1, 0, 2)
    sc_rows = s // COMBINE_CHUNKS
    out = None
    for ci in range(COMBINE_CHUNKS):
        idx = gather_idx[:, :, ci * sc_rows:(ci + 1) * sc_rows].reshape(1, -1)
        yg = _sc_gather(ys.reshape(2 * n_rows, PACK_W), idx).reshape(2, TOP_K, sc_rows, PACK_W)
        out = _final(x1, yg, w4, gt_f, g_final.reshape(1, d), tm, ci * (sc_rows // tm), out)
    return out.reshape(b, s, d)
```

```python
import functools
import math

import numpy as np
import jax
import jax.numpy as jnp
from jax import lax
from jax.experimental import pallas as pl
from jax.experimental.pallas import tpu as pltpu
from jax.experimental.pallas import tpu_sc as plsc

D_MODEL = 1024
HEAD_DIM = 64
SWA_HEADS = 8
SWA_KV_HEADS = 2
WINDOW = 128
FOX_HEADS = 8
BLOCK = 128
REL_BUCKETS = 32
REL_MAX_DIST = WINDOW
N_EXPERTS = 32
TOP_K = 4
D_FF = D_MODEL
SWIGLU_LIMIT = 7.0
SWIGLU_ALPHA = 1.702
RMS_EPS = 1e-5
N_MOD = 6

SWA_Q = SWA_HEADS * HEAD_DIM
SWA_KV = SWA_KV_HEADS * HEAD_DIM
FOX_W = FOX_HEADS * HEAD_DIM
LANES = 128
N_PAIRS = FOX_HEADS // 2
NEG_BIG = -1e30
LOG2E = math.log2(math.e)
FOX_TILE = 256
FOX_QSUB = 2
FOX_CHUNK = 4
N_SPLIT = 3
SKIP_LOG2 = 127.0
NORM_SLACK = 1.01
PACK_W = 256
MOE_TILE = 512
SC_WINDOW = 128
COMBINE_CHUNKS = 2
PREP_COLS = 256
VMEM_LIMIT = 56 * 1024 * 1024

F32 = jnp.float32
BF16 = jnp.bfloat16
HIGHEST = lax.Precision.HIGHEST


def _dot(a, b):
    return jnp.dot(a, b, preferred_element_type=F32)


def _dot_nt(a, b, precision=None):
    return lax.dot_general(a, b, (((1,), (1,)), ((), ())), preferred_element_type=F32, precision=precision)


def _const_spec(shape):
    nd = len(shape)
    return pl.BlockSpec(shape, lambda *_: (0,) * nd)


def _params(sem):
    return pltpu.CompilerParams(dimension_semantics=sem, vmem_limit_bytes=VMEM_LIMIT)


def _ada_kernel(c_ref, w_ref, b_ref, o_ref):
    c = c_ref[...]
    act = c * jax.nn.sigmoid(c)
    o_ref[...] = jnp.sum(act * w_ref[...], axis=0, keepdims=True) + b_ref[...]


def _ada(c_col, w_ada, b_ada):
    n = w_ada.shape[1]
    tn = 1024
    return pl.pallas_call(
        _ada_kernel,
        grid=(n // tn,),
        in_specs=[_const_spec((D_MODEL, 1)),
                  pl.BlockSpec((D_MODEL, tn), lambda j: (0, j)),
                  pl.BlockSpec((1, tn), lambda j: (0, j))],
        out_specs=pl.BlockSpec((1, tn), lambda j: (0, j)),
        out_shape=jax.ShapeDtypeStruct((1, n), F32),
        compiler_params=_params(("arbitrary",)),
        name="ada_mod",
    )(c_col, w_ada, b_ada)


def _split_bf16(v):
    parts = []
    for _ in range(N_SPLIT):
        p = v.astype(BF16)
        v = v - p.astype(F32)
        parts.append(p)
    return parts


def _inproj_kernel(x_ref, g_ref, sh_ref, sc_ref, wa_ref, wb_ref, wvt_ref, wg_ref, bf_ref, hind_ref, place_ref,
                   qa_ref, kva_ref, vat_ref, qb_ref, kb_ref, vbt_ref, sga_ref, sgb_ref, c_ref, ca_ref, qn_ref, kn_ref,
                   carry_ref):
    i = pl.program_id(0)
    tm = x_ref.shape[0]
    t = FOX_TILE
    nsub = tm // t

    @pl.when(i == 0)
    def _():
        carry_ref[...] = jnp.zeros_like(carry_ref)

    xf = x_ref[...]
    ms = jnp.mean(xf * xf, axis=-1, keepdims=True)
    y = xf * lax.rsqrt(ms + RMS_EPS) * g_ref[...]
    u = y * (1.0 + sc_ref[...]) + sh_ref[...]
    ub = u.astype(BF16)

    za = _dot(ub, wa_ref[...])
    qa_ref[...] = (za[:, :SWA_Q] * (HEAD_DIM ** -0.5 * LOG2E)).astype(BF16)
    kva_ref[...] = za[:, SWA_Q:].astype(BF16)

    zb = _dot(ub, wb_ref[...])
    qb = (zb[:, :FOX_W] * (HEAD_DIM ** -0.5 * LOG2E)).astype(BF16)
    kb = zb[:, FOX_W:2 * FOX_W].astype(BF16)
    qb_ref[...] = qb
    kb_ref[...] = kb
    vt = _dot_nt(wvt_ref[...], ub).astype(BF16)
    vbt_ref[...] = vt[:FOX_W]
    vat_ref[...] = vt[FOX_W:]

    def tile_norm_max(z, o_ref):
        zf = z.astype(F32)
        n2 = _dot((zf * zf).astype(BF16), hind_ref[...])
        for sb in range(nsub):
            o_ref[sb] = jnp.max(n2[sb * t:(sb + 1) * t], axis=0, keepdims=True)

    tile_norm_max(qb, qn_ref)
    tile_norm_max(kb, kn_ref)

    zg = _dot(ub, wg_ref[...])
    sg = jax.nn.sigmoid(zg)
    sga_ref[...] = sg[:, :D_MODEL].astype(BF16)
    sgb_ref[...] = sg[:, D_MODEL:].astype(BF16)

    fb = zb[:, 2 * FOX_W:2 * FOX_W + FOX_HEADS] + bf_ref[...]
    lf = jnp.minimum(fb, 0.0) - jnp.log1p(jnp.exp(-jnp.abs(fb)))
    r = lax.broadcasted_iota(jnp.int32, (t, t), 0)
    cc = lax.broadcasted_iota(jnp.int32, (t, t), 1)
    lower = (cc <= r).astype(BF16)
    carry = carry_ref[...]
    for sb in range(nsub):
        rows = slice(sb * t, (sb + 1) * t)
        local = sum(_dot(lower, p) for p in _split_bf16(lf[rows]))
        c_ref[rows, :] = local + carry
        carry = carry + local[t - 1:t, :]
        aug = sum(_dot(p, place_ref[k]) for k, p in enumerate(_split_bf16(local * LOG2E)))
        ca_ref[rows, :] = aug.astype(BF16)
    carry_ref[...] = carry


def _inproj(x2, g, sh, sc, wa, wb, wvt, wg, bfor, tm):
    s = x2.shape[0]
    nsub = tm // FOX_TILE
    row = lambda n: pl.BlockSpec((tm, n), lambda i: (i, 0))
    hind = np.zeros((FOX_W, FOX_HEADS), np.float32)
    hind[np.arange(FOX_W), np.arange(FOX_W) // HEAD_DIM] = 1.0
    place = np.zeros((N_SPLIT, FOX_HEADS, N_PAIRS * LANES), np.float32)
    for k in range(N_SPLIT):
        for h in range(FOX_HEADS):
            place[k, h, (h // 2) * LANES + N_SPLIT * (h % 2) + k] = 1.0
    hind = jnp.asarray(hind, BF16)
    place = jnp.asarray(place, BF16)
    out_shape = [
        jax.ShapeDtypeStruct((s, SWA_Q), BF16),
        jax.ShapeDtypeStruct((s, 2 * LANES), BF16),
        jax.ShapeDtypeStruct((SWA_KV, s), BF16),
        jax.ShapeDtypeStruct((s, FOX_W), BF16),
        jax.ShapeDtypeStruct((s, FOX_W), BF16),
        jax.ShapeDtypeStruct((FOX_W, s), BF16),
        jax.ShapeDtypeStruct((s, D_MODEL), BF16),
        jax.ShapeDtypeStruct((s, D_MODEL), BF16),
        jax.ShapeDtypeStruct((s, FOX_HEADS), F32),
        jax.ShapeDtypeStruct((s, N_PAIRS * LANES), BF16),
        jax.ShapeDtypeStruct((s // FOX_TILE, 1, FOX_HEADS), F32),
        jax.ShapeDtypeStruct((s // FOX_TILE, 1, FOX_HEADS), F32),
    ]
    stat = pl.BlockSpec((nsub, 1, FOX_HEADS), lambda i: (i, 0, 0))
    out_specs = [row(SWA_Q), row(2 * LANES), pl.BlockSpec((SWA_KV, tm), lambda i: (0, i)),
                 row(FOX_W), row(FOX_W), pl.BlockSpec((FOX_W, tm), lambda i: (0, i)),
                 row(D_MODEL), row(D_MODEL), row(FOX_HEADS), row(N_PAIRS * LANES), stat, stat]
    consts = [g, sh, sc, wa, wb, wvt, wg, bfor, hind, place]
    return pl.pallas_call(
        _inproj_kernel,
        grid=(s // tm,),
        in_specs=[row(D_MODEL)] + [_const_spec(a.shape) for a in consts],
        out_specs=out_specs,
        out_shape=out_shape,
        scratch_shapes=[pltpu.VMEM((1, FOX_HEADS), F32)],
        compiler_params=_params(("arbitrary",)),
        name="in_proj",
    )(x2, *consts)


def _t5_buckets_np():
    qi = np.arange(BLOCK)[:, None]
    kj = np.arange(2 * BLOCK)[None, :]
    dist = BLOCK + qi - kj
    n = np.maximum(dist, 0)
    max_exact = REL_BUCKETS // 2
    nf = np.maximum(n, 1).astype(np.float32)
    large = max_exact + (np.log(nf / np.float32(max_exact)) / np.float32(math.log(REL_MAX_DIST / max_exact))
                         * np.float32(REL_BUCKETS - max_exact)).astype(np.int32)
    large = np.minimum(large, REL_BUCKETS - 1)
    bucket = np.where(n < max_exact, n, large).astype(np.int32)
    band = (dist >= 0) & (dist < WINDOW)
    return np.where(band, bucket, -1).astype(np.int32)


def _swa_kernel(rel_ref, sink_ref, bkt_ref, q_ref, kc_ref, kp_ref, vc_ref, vp_ref, o_ref, bias_ref):
    n = pl.program_id(0)

    @pl.when(n == 0)
    def _():
        bkt = bkt_ref[...]
        prev = lax.broadcasted_iota(jnp.int32, bkt.shape, 0) < BLOCK
        for h in range(SWA_HEADS):
            acc = jnp.full(bkt.shape, NEG_BIG, F32)
            for b in range(REL_BUCKETS):
                acc = jnp.where(bkt == b, rel_ref[b * SWA_HEADS + h] * LOG2E, acc)
            cols = slice((h % 2) * BLOCK, (h % 2 + 1) * BLOCK)
            bias_ref[0, h // 2, :, cols] = acc
            bias_ref[1, h // 2, :, cols] = jnp.where(prev, NEG_BIG, acc)

    first = jnp.where(n == 0, 1, 0)
    lane = lax.broadcasted_iota(jnp.int32, (BLOCK, LANES), 1)
    col2 = lax.broadcasted_iota(jnp.int32, (1, 2 * BLOCK), 1)
    k2 = jnp.concatenate([kp_ref[...], kc_ref[...]], axis=0)
    v2 = jnp.concatenate([vp_ref[...], vc_ref[...]], axis=1)
    pairs = range(SWA_HEADS // 2)
    scores = []
    for p in pairs:
        qp = q_ref[:, p * LANES:(p + 1) * LANES]
        zero = jnp.zeros_like(qp)
        qs = jnp.concatenate([jnp.where(lane < HEAD_DIM, qp, zero), jnp.where(lane >= HEAD_DIM, qp, zero)], axis=0)
        g = p // 2
        scores.append(_dot_nt(k2[:, g * LANES:(g + 1) * LANES], qs))
    weights = []
    for p, s in zip(pairs, scores):
        s = s + bias_ref[first, p]
        sink = jnp.where(col2 < BLOCK, sink_ref[2 * p], sink_ref[2 * p + 1]) * LOG2E
        m = jnp.maximum(jnp.max(s, axis=0, keepdims=True), sink)
        e = jnp.exp2(s - m)
        denom = jnp.sum(e, axis=0, keepdims=True) + jnp.exp2(sink - m)
        weights.append((e.astype(BF16), denom))
    outs = []
    for p, (e, denom) in zip(pairs, weights):
        g = p // 2
        o = _dot(v2[g * HEAD_DIM:(g + 1) * HEAD_DIM], e) / denom
        outs += [o[:, :BLOCK], o[:, BLOCK:]]
    o_ref[...] = jnp.concatenate(outs, axis=0).T.astype(BF16)


def _swa(rel_flat, sinks, bkt_t, qa, ka2, vat):
    s = qa.shape[0]
    nb = s // BLOCK
    smem = pl.BlockSpec(memory_space=pltpu.SMEM)
    prev = lambda n: jnp.maximum(n - 1, 0)
    return pl.pallas_call(
        _swa_kernel,
        grid=(nb,),
        in_specs=[smem, smem, _const_spec(bkt_t.shape),
                  pl.BlockSpec((BLOCK, SWA_Q), lambda n: (n, 0)),
                  pl.BlockSpec((BLOCK, 2 * LANES), lambda n: (n, 0)),
                  pl.BlockSpec((BLOCK, 2 * LANES), lambda n: (prev(n), 0)),
                  pl.BlockSpec((SWA_KV, BLOCK), lambda n: (0, n)),
                  pl.BlockSpec((SWA_KV, BLOCK), lambda n: (0, prev(n)))],
        out_specs=pl.BlockSpec((BLOCK, SWA_Q), lambda n: (n, 0)),
        out_shape=jax.ShapeDtypeStruct((s, SWA_Q), BF16),
        scratch_shapes=[pltpu.VMEM((2, SWA_HEADS // 2, 2 * BLOCK, 2 * BLOCK), F32)],
        compiler_params=_params(("arbitrary",)),
        name="swa_attn",
    )(rel_flat, sinks, bkt_t, qa, ka2, ka2, vat, vat)


def _fox_kernel(jlo_ref, base_ref, q_ref, k_ref, ca_ref, vt_ref, o_ref):
    p = pl.program_id(0)
    i = pl.program_id(1)
    t = FOX_TILE
    tq = FOX_QSUB * t
    first_diag = i * FOX_QSUB
    lane = lax.broadcasted_iota(jnp.int32, (tq, LANES), 1)
    q = q_ref[...]
    wq = []
    for hh in range(2):
        in_head = (lane >= hh * HEAD_DIM) & (lane < (hh + 1) * HEAD_DIM)
        qm = jnp.where(in_head, q, jnp.zeros_like(q))
        sel = (lane >= N_SPLIT * hh) & (lane < N_SPLIT * (hh + 1))
        aug = jnp.where(sel, -1.0, 0.0).astype(BF16)
        wq.append(jnp.concatenate([qm, aug], axis=1))
    key = lax.broadcasted_iota(jnp.int32, (t, tq), 0)
    qry = lax.broadcasted_iota(jnp.int32, (t, tq), 1)
    causal = [key + d * t <= qry for d in range(FOX_QSUB)]

    def step(j, carry, heads, nsub, diagonal=False):
        start = pl.multiple_of(j * t, t)
        rows = nsub * t
        lhs = jnp.concatenate([k_ref[pl.ds(start, rows), :], ca_ref[pl.ds(start, rows), :]], axis=1)
        scores = [_dot_nt(lhs, wq[hh]) for hh in heads]
        mid = []
        for hh, s, (m, l, acc) in zip(heads, scores, carry):
            h = 2 * p + hh
            parts = [s[k * t:(k + 1) * t] for k in range(nsub)]
            if diagonal:
                for d in range(FOX_QSUB):
                    k = nsub - FOX_QSUB + d
                    parts[k] = jnp.where(causal[d], parts[k], NEG_BIG)
            djs = [base_ref[h, first_diag] - base_ref[h, j + k] for k in range(nsub)]
            m_new = m
            for part, dj in zip(parts, djs):
                m_new = jnp.maximum(m_new, jnp.max(part, axis=0, keepdims=True) + dj)
            alpha = jnp.exp2(m - m_new)
            es = [jnp.exp2(part + (dj - m_new)) for part, dj in zip(parts, djs)]
            l = alpha * l
            for e in es:
                l = l + jnp.sum(e, axis=0, keepdims=True)
            e_all = es[0] if nsub == 1 else jnp.concatenate(es, axis=0)
            mid.append((m_new, l, alpha, acc, e_all.astype(BF16)))
        new = []
        for hh, (m_new, l, alpha, acc, e_all) in zip(heads, mid):
            vt = vt_ref[hh * HEAD_DIM:(hh + 1) * HEAD_DIM, pl.ds(start, rows)]
            new.append((m_new, l, alpha * acc + _dot(vt, e_all)))
        return tuple(new)

    def run_alone(lo, hi, carry, heads):
        n_full = (hi - lo) // FOX_CHUNK
        carry = lax.fori_loop(0, n_full, lambda n, c: step(lo + FOX_CHUNK * n, c, heads, FOX_CHUNK), carry)
        rest = lo + FOX_CHUNK * n_full
        tails = [lambda c: c] + [functools.partial(lambda c, k: step(rest, c, heads, k), k=k)
                                 for k in range(1, FOX_CHUNK)]
        return lax.switch(hi - rest, tails, carry)

    def run_to_diagonal(lo, carry, heads):
        end = first_diag + FOX_QSUB
        count = end - lo
        last = (count - FOX_QSUB) % FOX_CHUNK + FOX_QSUB
        n_full = (count - last) // FOX_CHUNK
        carry = lax.fori_loop(0, n_full, lambda n, c: step(lo + FOX_CHUNK * n, c, heads, FOX_CHUNK), carry)
        tails = [functools.partial(lambda c, k: step(end - k, c, heads, k, diagonal=True), k=k)
                 for k in range(FOX_QSUB, FOX_QSUB + FOX_CHUNK)]
        return lax.switch(last - FOX_QSUB, tails, carry)

    lo0 = jlo_ref[2 * p, i]
    lo1 = jlo_ref[2 * p + 1, i]
    lo_both = jnp.maximum(lo0, lo1)
    init = (jnp.full((1, tq), NEG_BIG, F32), jnp.zeros((1, tq), F32), jnp.zeros((HEAD_DIM, tq), F32))
    (c0,) = run_alone(lo0, lo_both, (init,), (0,))
    (c1,) = run_alone(lo1, lo_both, (init,), (1,))
    carry = run_to_diagonal(lo_both, (c0, c1), (0, 1))
    ot = jnp.concatenate([carry[0][2] / carry[0][1], carry[1][2] / carry[1][1]], axis=0)
    o_ref[...] = ot.T.astype(BF16)


def _fox_schedule(cum, qn2, kn2):
    nt = qn2.shape[0]
    t = FOX_TILE
    bq = jnp.sqrt(qn2.reshape(nt, FOX_HEADS)) * NORM_SLACK
    bk = jnp.sqrt(kn2.reshape(nt, FOX_HEADS)) * NORM_SLACK
    c2 = cum.reshape(nt, t, FOX_HEADS) * LOG2E
    c_first, c_last = c2[:, 0, :], c2[:, t - 1, :]
    upper = (bq[:, None, :] * bk[None, :, :] + (bq * bk)[:, None, :] + c_first[:, None, :] - c_last[None, :, :])
    ii = jnp.arange(nt)[:, None, None]
    jj = jnp.arange(nt)[None, :, None]
    needed = (jj <= ii) & ((upper >= -SKIP_LOG2) | (jj == ii))
    jlo = jnp.min(jnp.where(needed, jj, nt), axis=1)
    jlo = jnp.min(jlo.reshape(nt // FOX_QSUB, FOX_QSUB, FOX_HEADS), axis=1).T.astype(jnp.int32)
    base = jnp.concatenate([jnp.zeros((1, FOX_HEADS), F32), c_last[:-1]], axis=0).T
    return jlo, base


def _fox(jlo, base, qb, kb, ca, vbt):
    s = qb.shape[0]
    t = FOX_QSUB * FOX_TILE
    grid_spec = pltpu.PrefetchScalarGridSpec(
        num_scalar_prefetch=2,
        grid=(N_PAIRS, s // t),
        in_specs=[pl.BlockSpec((t, LANES), lambda p, i, *_: (i, p)),
                  pl.BlockSpec((s, LANES), lambda p, i, *_: (0, p)),
                  pl.BlockSpec((s, LANES), lambda p, i, *_: (0, p)),
                  pl.BlockSpec((LANES, s), lambda p, i, *_: (p, 0))],
        out_specs=pl.BlockSpec((t, LANES), lambda p, i, *_: (i, p)),
    )
    return pl.pallas_call(
        _fox_kernel,
        grid_spec=grid_spec,
        out_shape=jax.ShapeDtypeStruct((s, FOX_W), BF16),
        compiler_params=_params(("arbitrary", "arbitrary")),
        name="fox_attn",
    )(jlo, base, qb, kb, ca, vbt)


def _rms(x, g):
    return x * lax.rsqrt(jnp.mean(x * x, axis=-1, keepdims=True) + RMS_EPS) * g


def _pack_rows(v):
    halves = []
    for j in range(2):
        lo = v[:, (2 * j) * PACK_W:(2 * j + 1) * PACK_W].astype(BF16).astype(F32)
        hi = v[:, (2 * j + 1) * PACK_W:(2 * j + 2) * PACK_W].astype(BF16).astype(F32)
        lo_bits = lax.bitcast_convert_type(lo, jnp.uint32)
        hi_bits = lax.bitcast_convert_type(hi, jnp.uint32)
        halves.append(hi_bits | (lo_bits >> 16))
    return halves


def _unpack_words(w):
    lo = lax.bitcast_convert_type(w << 16, F32)
    hi = lax.bitcast_convert_type(w & jnp.uint32(0xFFFF0000), F32)
    return lo, hi


def _outproj_kernel(x_ref, ya_ref, yb_ref, sga_ref, sgb_ref, pa_ref, pb_ref, wo_ref, gt_ref, g_ref, sh_ref, sc_ref,
                    wr_ref, br_ref, x1_ref, u2p_ref, e4_ref, r4_ref, w4_ref, cnt_ref, carry_ref):
    i = pl.program_id(0)
    tm = x_ref.shape[0]

    @pl.when(i == 0)
    def _():
        carry_ref[...] = jnp.zeros_like(carry_ref)

    merged = (sga_ref[...].astype(F32) * _dot(ya_ref[...], pa_ref[...])
              + sgb_ref[...].astype(F32) * _dot(yb_ref[...], pb_ref[...]))
    x1 = x_ref[...] + gt_ref[...] * _dot(merged.astype(BF16), wo_ref[...])
    x1_ref[...] = x1
    u2 = _rms(x1, g_ref[...]) * (1.0 + sc_ref[...]) + sh_ref[...]
    halves = _pack_rows(u2)
    u2p_ref[0] = halves[0]
    u2p_ref[1] = halves[1]

    logits = _dot(u2.astype(BF16), wr_ref[...]) + br_ref[...]
    eidx = lax.broadcasted_iota(jnp.int32, logits.shape, 1)
    work = logits
    sel = jnp.zeros(logits.shape, jnp.bool_)
    picks, vals = [], []
    for k in range(TOP_K):
        m = jnp.max(work, axis=-1, keepdims=True)
        first = jnp.min(jnp.where(work == m, eidx, N_EXPERTS), axis=-1, keepdims=True)
        hit = eidx == first
        sel = sel | hit
        work = jnp.where(hit, -jnp.inf, work)
        picks.append(first)
        vals.append(m)
    exps = [jnp.exp(v - vals[0]) for v in vals]
    denom = exps[0] + exps[1] + exps[2] + exps[3]

    r = lax.broadcasted_iota(jnp.int32, (tm, tm), 0)
    cc = lax.broadcasted_iota(jnp.int32, (tm, tm), 1)
    before = (cc < r).astype(BF16)
    chosen = sel.astype(BF16)
    rank = _dot(before, chosen) + carry_ref[...]
    cnt = carry_ref[...] + jnp.sum(chosen.astype(F32), axis=0, keepdims=True)
    carry_ref[...] = cnt
    cnt_ref[...] = cnt
    for k in range(TOP_K):
        e4_ref[:, k:k + 1] = picks[k]
        rk = jnp.sum(jnp.where(eidx == picks[k], rank, 0.0), axis=-1, keepdims=True)
        r4_ref[:, k:k + 1] = rk.astype(jnp.int32)
        w4_ref[:, k:k + 1] = exps[k] / denom


def _outproj(x2, ya, yb, sga, sgb, pa, pb, wo, gt, g, sh, sc, wr, br, tm):
    s = x2.shape[0]
    row = lambda n: pl.BlockSpec((tm, n), lambda i: (i, 0))
    consts = [pa, pb, wo, gt, g, sh, sc, wr, br]
    return pl.pallas_call(
        _outproj_kernel,
        grid=(s // tm,),
        in_specs=[row(D_MODEL), row(SWA_Q), row(FOX_W), row(D_MODEL), row(D_MODEL)] + [_const_spec(a.shape) for a in consts],
        out_specs=[row(D_MODEL), pl.BlockSpec((2, tm, PACK_W), lambda i: (0, i, 0)), row(TOP_K), row(TOP_K), row(TOP_K),
                   _const_spec((1, N_EXPERTS))],
        out_shape=[jax.ShapeDtypeStruct((s, D_MODEL), F32),
                   jax.ShapeDtypeStruct((2, s, PACK_W), jnp.uint32),
                   jax.ShapeDtypeStruct((s, TOP_K), jnp.int32),
                   jax.ShapeDtypeStruct((s, TOP_K), jnp.int32),
                   jax.ShapeDtypeStruct((s, TOP_K), F32),
                   jax.ShapeDtypeStruct((1, N_EXPERTS), F32)],
        scratch_shapes=[pltpu.VMEM((1, N_EXPERTS), F32)],
        compiler_params=_params(("arbitrary",)),
        name="out_proj_router",
    )(x2, ya, yb, sga, sgb, *consts)


def _sc_mesh():
    return plsc.VectorSubcoreMesh(core_axis_name="core", subcore_axis_name="subcore")


def _sc_dispatch(rows, idx, n_out):
    n, width = rows.shape

    @functools.partial(pl.kernel, out_type=jax.ShapeDtypeStruct((n_out, width), rows.dtype), mesh=_sc_mesh(),
                       scratch_types=[])
    def dispatch(x_hbm, i_hbm, o_hbm):
        def body(x_vmem, i_vmem):
            for k in range(TOP_K):
                pltpu.sync_copy(x_vmem, o_hbm.at[i_vmem.at[k]])

        pltpu.emit_pipeline(
            body, grid=(n // SC_WINDOW,),
            in_specs=[pl.BlockSpec((SC_WINDOW, width), lambda i: (i, 0)),
                      pl.BlockSpec((TOP_K, SC_WINDOW), lambda i: (0, i))],
            out_specs=[], core_axis_name=("core", "subcore"), dimension_semantics=(pltpu.PARALLEL,),
        )(x_hbm, i_hbm)

    return dispatch(rows, idx)


def _sc_gather(table, idx):
    n = idx.shape[1]
    width = table.shape[1]

    @functools.partial(pl.kernel, out_type=jax.ShapeDtypeStruct((n, width), table.dtype), mesh=_sc_mesh(),
                       scratch_types=[])
    def gather(t_hbm, i_hbm, o_hbm):
        def body(i_vmem, o_vmem):
            pltpu.sync_copy(t_hbm.at[i_vmem.at[0]], o_vmem)

        pltpu.emit_pipeline(
            body, grid=(n // SC_WINDOW,),
            in_specs=[pl.BlockSpec((1, SC_WINDOW), lambda i: (0, i))],
            out_specs=[pl.BlockSpec((SC_WINDOW, width), lambda i: (i, 0))],
            core_axis_name=("core", "subcore"), dimension_semantics=(pltpu.PARALLEL,),
        )(i_hbm, o_hbm)

    return gather(table, idx)


def _moe_kernel(te_ref, first_ref, nact_ref, xs_ref, w1_ref, b1g_ref, b1l_ref, w2_ref, b2_ref, y_ref,
                wg_ref, wl_ref, w2b_ref, tmp_ref):
    i = pl.program_id(0)
    live = i < nact_ref[0]

    @pl.when(jnp.logical_and(live, first_ref[i] == 1))
    def _():
        half = PREP_COLS // 2
        for c in range(2 * D_FF // PREP_COLS):
            tr = w1_ref[0, :, c * PREP_COLS:(c + 1) * PREP_COLS].T
            rows = slice(c * half, (c + 1) * half)
            for j in range(D_MODEL // LANES):
                lanes = slice(j * LANES, (j + 1) * LANES)
                tmp_ref[j] = tr[:, lanes]
                wg_ref[rows, lanes] = tmp_ref[j, pl.ds(0, half, stride=2), :].astype(BF16)
                wl_ref[rows, lanes] = tmp_ref[j, pl.ds(1, half, stride=2), :].astype(BF16)
        w2b_ref[...] = w2_ref[0].astype(BF16)

    @pl.when(live)
    def _():
        chunks = []
        for j in range(2):
            lo, hi = _unpack_words(xs_ref[j])
            chunks += [lo.astype(BF16), hi.astype(BF16)]
        hg = b1g_ref[0]
        hl = b1l_ref[0]
        for c, xc in enumerate(chunks):
            cols = slice(c * PACK_W, (c + 1) * PACK_W)
            hg = hg + _dot_nt(xc, wg_ref[:, cols])
            hl = hl + _dot_nt(xc, wl_ref[:, cols])
        glu = jnp.minimum(hg, SWIGLU_LIMIT)
        lin = jnp.clip(hl, -SWIGLU_LIMIT, SWIGLU_LIMIT)
        a = glu * jax.nn.sigmoid(SWIGLU_ALPHA * glu) * (lin + 1.0)
        y = _dot(a.astype(BF16), w2b_ref[...]) + b2_ref[0]
        halves = _pack_rows(y)
        y_ref[0] = halves[0]
        y_ref[1] = halves[1]


def _moe_routed(tile_expert, tile_first, n_active, xs, w1, b1g, b1l, w2, b2):
    n_rows = xs.shape[1]
    n_tiles = n_rows // MOE_TILE
    live = lambda i, na: jnp.minimum(i, na[0] - 1)
    rows_spec = pl.BlockSpec((2, MOE_TILE, PACK_W), lambda i, te, tf, na: (0, live(i, na), 0))
    wsp = lambda a: pl.BlockSpec((1,) + a.shape[1:], lambda i, te, tf, na: (te[live(i, na)], 0, 0))
    grid_spec = pltpu.PrefetchScalarGridSpec(
        num_scalar_prefetch=3,
        grid=(n_tiles,),
        in_specs=[rows_spec, wsp(w1), wsp(b1g), wsp(b1l), wsp(w2), wsp(b2)],
        out_specs=rows_spec,
        scratch_shapes=[pltpu.VMEM((D_FF, D_MODEL), BF16), pltpu.VMEM((D_FF, D_MODEL), BF16),
                        pltpu.VMEM((D_FF, D_MODEL), BF16), pltpu.VMEM((D_MODEL // LANES, PREP_COLS, LANES), F32)],
    )
    return pl.pallas_call(
        _moe_kernel,
        grid_spec=grid_spec,
        out_shape=jax.ShapeDtypeStruct(xs.shape, jnp.uint32),
        compiler_params=_params(("arbitrary",)),
        name="moe_routed",
    )(tile_expert, tile_first, n_active, xs, w1, b1g, b1l, w2, b2)


def _final_kernel(x1_ref, yg_ref, w4_ref, gt_ref, gf_ref, *rest):
    o_ref = rest[-1]
    w4 = w4_ref[...]
    cols = []
    for j in range(2):
        lo_acc = hi_acc = None
        for k in range(TOP_K):
            lo, hi = _unpack_words(yg_ref[j, k])
            wk = w4[:, k:k + 1]
            lo_acc = wk * lo if lo_acc is None else lo_acc + wk * lo
            hi_acc = wk * hi if hi_acc is None else hi_acc + wk * hi
        cols += [lo_acc, hi_acc]
    moe = jnp.concatenate(cols, axis=1)
    x2 = x1_ref[...] + gt_ref[...] * moe
    o_ref[...] = _rms(x2, gf_ref[...])


def _final(x1, yg, w4, gt, gf, tm, first_tile, prev_out):
    s = x1.shape[0]
    n_tiles = yg.shape[2] // tm
    row = lambda n: pl.BlockSpec((tm, n), lambda i: (i + first_tile, 0))
    in_specs = [row(D_MODEL), pl.BlockSpec((2, TOP_K, tm, PACK_W), lambda i: (0, 0, i, 0)), row(TOP_K),
                _const_spec(gt.shape), _const_spec(gf.shape)]
    args = [x1, yg, w4, gt, gf]
    aliases = {}
    if prev_out is not None:
        in_specs.append(pl.BlockSpec(memory_space=pl.ANY))
        args.append(prev_out)
        aliases = {len(args) - 1: 0}
    return pl.pallas_call(
        _final_kernel,
        grid=(n_tiles,),
        in_specs=in_specs,
        out_specs=row(D_MODEL),
        out_shape=jax.ShapeDtypeStruct((s, D_MODEL), F32),
        input_output_aliases=aliases,
        compiler_params=_params(("arbitrary",)),
        name="combine_final_norm",
    )(*args)


def _routing_tables(e4, r4, counts, s):
    n_rows = TOP_K * s + N_EXPERTS * MOE_TILE
    cnt = counts.reshape(N_EXPERTS).astype(jnp.int32)
    padded = ((cnt + MOE_TILE - 1) // MOE_TILE) * MOE_TILE
    ends = jnp.cumsum(padded)
    starts = ends - padded
    pos = (jnp.take(starts, e4, axis=0) + r4).T
    tile_start = jnp.arange(n_rows // MOE_TILE, dtype=jnp.int32) * MOE_TILE
    tile_expert = jnp.minimum(jnp.sum(tile_start[:, None] >= ends[None, :], axis=1), N_EXPERTS - 1).astype(jnp.int32)
    tile_first = jnp.concatenate([jnp.ones((1,), jnp.int32), (tile_expert[1:] != tile_expert[:-1]).astype(jnp.int32)])
    n_active = (ends[-1:] // MOE_TILE).astype(jnp.int32)
    return pos, tile_expert, tile_first, n_active, n_rows


def kernel(x, c, w_ada, b_ada, g_mix, w_in, b_forget, sinks, rel_bias, w_proj_a, w_proj_b, w_out, g_ffn,
           w_router, b_router, w_e1, b_e1, w_e2, b_e2, g_final):
    b, s, d = x.shape
    assert b == 1 and d == D_MODEL and w_ada.shape[0] == 1
    x2 = x.reshape(s, d)
    tm = min(512, s)

    mod = _ada(c.reshape(d, 1), w_ada[0], b_ada)
    sh_m, sc_m, gt_m, sh_f, sc_f, gt_f = [mod[:, k * d:(k + 1) * d] for k in range(N_MOD)]

    w = w_in[0]
    o_ka, o_va, o_b = SWA_Q, SWA_Q + SWA_KV, SWA_Q + 2 * SWA_KV
    o_f = o_b + 3 * FOX_W
    o_g = o_f + FOX_HEADS
    dup = lambda m: jnp.concatenate([m[:, :HEAD_DIM], m[:, :HEAD_DIM], m[:, HEAD_DIM:], m[:, HEAD_DIM:]], axis=1)
    wa = jnp.concatenate([w[:, :SWA_Q], dup(w[:, o_ka:o_va])], axis=1).astype(BF16)
    pad = jnp.zeros((d, LANES - FOX_HEADS), F32)
    wb = jnp.concatenate([w[:, o_b:o_b + 2 * FOX_W], w[:, o_f:o_g], pad], axis=1).astype(BF16)
    wvt = jnp.concatenate([w[:, o_b + 2 * FOX_W:o_f], w[:, o_va:o_b]], axis=1).T.astype(BF16)
    wg = w[:, o_g:].astype(BF16)
    qa, ka2, vat, qb, kb, vbt, sga, sgb, cum, ca, qn2, kn2 = _inproj(
        x2, g_mix, sh_m, sc_m, wa, wb, wvt, wg, b_forget, tm)

    ya = _swa(rel_bias.reshape(-1), sinks[0], jnp.asarray(_t5_buckets_np().T), qa, ka2, vat)
    jlo, base = _fox_schedule(cum, qn2, kn2)
    yb = _fox(jlo, base, qb, kb, ca, vbt)

    x1, u2p, e4, r4, w4, counts = _outproj(
        x2, ya, yb, sga, sgb, w_proj_a[0].astype(BF16), w_proj_b[0].astype(BF16), w_out[0].astype(BF16),
        gt_m, g_ffn, sh_f, sc_f, w_router[0].astype(BF16), b_router, tm)

    pos, tile_expert, tile_first, n_active, n_rows = _routing_tables(e4, r4, counts, s)
    pos2 = jnp.concatenate([pos, pos + n_rows], axis=1)
    xs = _sc_dispatch(u2p.reshape(2 * s, PACK_W), pos2, 2 * n_rows).reshape(2, n_rows, PACK_W)

    b1 = b_e1[0].reshape(N_EXPERTS, D_FF, 2)
    b1g = b1[:, None, :, 0]
    b1l = b1[:, None, :, 1]
    ys = _moe_routed(tile_expert, tile_first, n_active, xs, w_e1[0], b1g, b1l, w_e2[0], b_e2[0][:, None, :])

    gather_idx = pos2.reshape(TOP_K, 2, s).transpose(1, 0, 2)
    sc_rows = s // COMBINE_CHUNKS
    out = None
    for ci in range(COMBINE_CHUNKS):
        idx = gather_idx[:, :, ci * sc_rows:(ci + 1) * sc_rows].reshape(1, -1)
        yg = _sc_gather(ys.reshape(2 * n_rows, PACK_W), idx).reshape(2, TOP_K, sc_rows, PACK_W)
        out = _final(x1, yg, w4, gt_f, g_final.reshape(1, d), tm, ci * (sc_rows // tm), out)
    return out.reshape(b, s, d)
```

```python
import dataclasses
import functools
import math

import numpy as np
import jax
import jax.numpy as jnp
from jax import lax
from jax.experimental import pallas as pl
from jax.experimental.pallas import tpu as pltpu
from jax.experimental.pallas import tpu_sc as plsc

D_MODEL = 1024
HEAD_DIM = 64
SWA_HEADS = 8
SWA_KV_HEADS = 2
WINDOW = 128
FOX_HEADS = 8
BLOCK = 128
REL_BUCKETS = 32
REL_MAX_DIST = WINDOW
N_EXPERTS = 32
TOP_K = 4
D_FF = D_MODEL
SWIGLU_LIMIT = 7.0
SWIGLU_ALPHA = 1.702
RMS_EPS = 1e-5
N_MOD = 6

SWA_Q = SWA_HEADS * HEAD_DIM
SWA_KV = SWA_KV_HEADS * HEAD_DIM
FOX_W = FOX_HEADS * HEAD_DIM
LANES = 128
N_PAIRS = FOX_HEADS // 2
NEG_BIG = -1e30
LOG2E = math.log2(math.e)
FOX_TILE = 256
FOX_QSUB = 2
FOX_CHUNK = 4
N_SPLIT = 3
SKIP_LOG2 = 127.0
NORM_SLACK = 1.01
PACK_W = 256
MOE_TILE = 512
SC_WINDOW = 128
FF_SLAB = 256
COMBINE_CHUNKS = 2
SC_LANES = 16
SC_PACK_ROWS = 8
VMEM_LIMIT = 56 * 1024 * 1024

F32 = jnp.float32
BF16 = jnp.bfloat16
HIGHEST = lax.Precision.HIGHEST


def _dot(a, b):
    return jnp.dot(a, b, preferred_element_type=F32)


def _dot_nt(a, b, precision=None):
    return lax.dot_general(a, b, (((1,), (1,)), ((), ())), preferred_element_type=F32, precision=precision)


def _const_spec(shape):
    nd = len(shape)
    return pl.BlockSpec(shape, lambda *_: (0,) * nd)


def _params(sem):
    return pltpu.CompilerParams(dimension_semantics=sem, vmem_limit_bytes=VMEM_LIMIT)


def _ada_kernel(c_ref, w_ref, b_ref, o_ref):
    c = c_ref[...]
    act = c * jax.nn.sigmoid(c)
    o_ref[...] = jnp.sum(act * w_ref[...], axis=0, keepdims=True) + b_ref[...]


def _ada(c_col, w_ada, b_ada):
    n = w_ada.shape[1]
    tn = 1024
    return pl.pallas_call(
        _ada_kernel,
        grid=(n // tn,),
        in_specs=[_const_spec((D_MODEL, 1)),
                  pl.BlockSpec((D_MODEL, tn), lambda j: (0, j)),
                  pl.BlockSpec((1, tn), lambda j: (0, j))],
        out_specs=pl.BlockSpec((1, tn), lambda j: (0, j)),
        out_shape=jax.ShapeDtypeStruct((1, n), F32),
        compiler_params=_params(("arbitrary",)),
        name="ada_mod",
    )(c_col, w_ada, b_ada)


def _split_bf16(v):
    parts = []
    for _ in range(N_SPLIT):
        p = v.astype(BF16)
        v = v - p.astype(F32)
        parts.append(p)
    return parts


def _inproj_kernel(x_ref, g_ref, sh_ref, sc_ref, wa_ref, wb_ref, wvt_ref, wg_ref, bf_ref, hind_ref, place_ref,
                   qa_ref, kva_ref, vat_ref, qb_ref, kb_ref, vbt_ref, sga_ref, sgb_ref, c_ref, ca_ref, qn_ref, kn_ref,
                   carry_ref):
    i = pl.program_id(0)
    tm = x_ref.shape[0]
    t = FOX_TILE
    nsub = tm // t

    @pl.when(i == 0)
    def _():
        carry_ref[...] = jnp.zeros_like(carry_ref)

    xf = x_ref[...]
    ms = jnp.mean(xf * xf, axis=-1, keepdims=True)
    y = xf * lax.rsqrt(ms + RMS_EPS) * g_ref[...]
    u = y * (1.0 + sc_ref[...]) + sh_ref[...]
    ub = u.astype(BF16)

    za = _dot(ub, wa_ref[...])
    qa_ref[...] = (za[:, :SWA_Q] * (HEAD_DIM ** -0.5 * LOG2E)).astype(BF16)
    kva_ref[...] = za[:, SWA_Q:].astype(BF16)

    zb = _dot(ub, wb_ref[...])
    qb = (zb[:, :FOX_W] * (HEAD_DIM ** -0.5 * LOG2E)).astype(BF16)
    kb = zb[:, FOX_W:2 * FOX_W].astype(BF16)
    qb_ref[...] = qb
    kb_ref[...] = kb
    vt = _dot_nt(wvt_ref[...], ub).astype(BF16)
    vbt_ref[...] = vt[:FOX_W]
    vat_ref[...] = vt[FOX_W:]

    def tile_norm_max(z, o_ref):
        zf = z.astype(F32)
        n2 = _dot((zf * zf).astype(BF16), hind_ref[...])
        for sb in range(nsub):
            o_ref[sb] = jnp.max(n2[sb * t:(sb + 1) * t], axis=0, keepdims=True)

    tile_norm_max(qb, qn_ref)
    tile_norm_max(kb, kn_ref)

    zg = _dot(ub, wg_ref[...])
    sg = jax.nn.sigmoid(zg)
    sga_ref[...] = sg[:, :D_MODEL].astype(BF16)
    sgb_ref[...] = sg[:, D_MODEL:].astype(BF16)

    fb = zb[:, 2 * FOX_W:2 * FOX_W + FOX_HEADS] + bf_ref[...]
    lf = jnp.minimum(fb, 0.0) - jnp.log1p(jnp.exp(-jnp.abs(fb)))
    r = lax.broadcasted_iota(jnp.int32, (t, t), 0)
    cc = lax.broadcasted_iota(jnp.int32, (t, t), 1)
    lower = (cc <= r).astype(BF16)
    carry = carry_ref[...]
    for sb in range(nsub):
        rows = slice(sb * t, (sb + 1) * t)
        local = sum(_dot(lower, p) for p in _split_bf16(lf[rows]))
        c_ref[rows, :] = local + carry
        carry = carry + local[t - 1:t, :]
        aug = sum(_dot(p, place_ref[k]) for k, p in enumerate(_split_bf16(local * LOG2E)))
        ca_ref[rows, :] = aug.astype(BF16)
    carry_ref[...] = carry


def _inproj(x2, g, sh, sc, wa, wb, wvt, wg, bfor, tm):
    s = x2.shape[0]
    nsub = tm // FOX_TILE
    row = lambda n: pl.BlockSpec((tm, n), lambda i: (i, 0))
    hind = np.zeros((FOX_W, FOX_HEADS), np.float32)
    hind[np.arange(FOX_W), np.arange(FOX_W) // HEAD_DIM] = 1.0
    place = np.zeros((N_SPLIT, FOX_HEADS, N_PAIRS * LANES), np.float32)
    for k in range(N_SPLIT):
        for h in range(FOX_HEADS):
            place[k, h, (h // 2) * LANES + N_SPLIT * (h % 2) + k] = 1.0
    hind = jnp.asarray(hind, BF16)
    place = jnp.asarray(place, BF16)
    out_shape = [
        jax.ShapeDtypeStruct((s, SWA_Q), BF16),
        jax.ShapeDtypeStruct((s, 2 * LANES), BF16),
        jax.ShapeDtypeStruct((SWA_KV, s), BF16),
        jax.ShapeDtypeStruct((s, FOX_W), BF16),
        jax.ShapeDtypeStruct((s, FOX_W), BF16),
        jax.ShapeDtypeStruct((FOX_W, s), BF16),
        jax.ShapeDtypeStruct((s, D_MODEL), BF16),
        jax.ShapeDtypeStruct((s, D_MODEL), BF16),
        jax.ShapeDtypeStruct((s, FOX_HEADS), F32),
        jax.ShapeDtypeStruct((s, N_PAIRS * LANES), BF16),
        jax.ShapeDtypeStruct((s // FOX_TILE, 1, FOX_HEADS), F32),
        jax.ShapeDtypeStruct((s // FOX_TILE, 1, FOX_HEADS), F32),
    ]
    stat = pl.BlockSpec((nsub, 1, FOX_HEADS), lambda i: (i, 0, 0))
    out_specs = [row(SWA_Q), row(2 * LANES), pl.BlockSpec((SWA_KV, tm), lambda i: (0, i)),
                 row(FOX_W), row(FOX_W), pl.BlockSpec((FOX_W, tm), lambda i: (0, i)),
                 row(D_MODEL), row(D_MODEL), row(FOX_HEADS), row(N_PAIRS * LANES), stat, stat]
    consts = [g, sh, sc, wa, wb, wvt, wg, bfor, hind, place]
    return pl.pallas_call(
        _inproj_kernel,
        grid=(s // tm,),
        in_specs=[row(D_MODEL)] + [_const_spec(a.shape) for a in consts],
        out_specs=out_specs,
        out_shape=out_shape,
        scratch_shapes=[pltpu.VMEM((1, FOX_HEADS), F32)],
        compiler_params=_params(("arbitrary",)),
        name="in_proj",
    )(x2, *consts)


def _t5_buckets_np():
    qi = np.arange(BLOCK)[:, None]
    kj = np.arange(2 * BLOCK)[None, :]
    dist = BLOCK + qi - kj
    n = np.maximum(dist, 0)
    max_exact = REL_BUCKETS // 2
    nf = np.maximum(n, 1).astype(np.float32)
    large = max_exact + (np.log(nf / np.float32(max_exact)) / np.float32(math.log(REL_MAX_DIST / max_exact))
                         * np.float32(REL_BUCKETS - max_exact)).astype(np.int32)
    large = np.minimum(large, REL_BUCKETS - 1)
    bucket = np.where(n < max_exact, n, large).astype(np.int32)
    band = (dist >= 0) & (dist < WINDOW)
    return np.where(band, bucket, -1).astype(np.int32)


def _swa_kernel(rel_ref, sink_ref, bkt_ref, q_ref, kc_ref, kp_ref, vc_ref, vp_ref, o_ref, bias_ref):
    n = pl.program_id(0)

    @pl.when(n == 0)
    def _():
        bkt = bkt_ref[...]
        prev = lax.broadcasted_iota(jnp.int32, bkt.shape, 0) < BLOCK
        for h in range(SWA_HEADS):
            acc = jnp.full(bkt.shape, NEG_BIG, F32)
            for b in range(REL_BUCKETS):
                acc = jnp.where(bkt == b, rel_ref[b * SWA_HEADS + h] * LOG2E, acc)
            cols = slice((h % 2) * BLOCK, (h % 2 + 1) * BLOCK)
            bias_ref[0, h // 2, :, cols] = acc
            bias_ref[1, h // 2, :, cols] = jnp.where(prev, NEG_BIG, acc)

    first = jnp.where(n == 0, 1, 0)
    lane = lax.broadcasted_iota(jnp.int32, (BLOCK, LANES), 1)
    col2 = lax.broadcasted_iota(jnp.int32, (1, 2 * BLOCK), 1)
    k2 = jnp.concatenate([kp_ref[...], kc_ref[...]], axis=0)
    v2 = jnp.concatenate([vp_ref[...], vc_ref[...]], axis=1)
    pairs = range(SWA_HEADS // 2)
    scores = []
    for p in pairs:
        qp = q_ref[:, p * LANES:(p + 1) * LANES]
        zero = jnp.zeros_like(qp)
        qs = jnp.concatenate([jnp.where(lane < HEAD_DIM, qp, zero), jnp.where(lane >= HEAD_DIM, qp, zero)], axis=0)
        g = p // 2
        scores.append(_dot_nt(k2[:, g * LANES:(g + 1) * LANES], qs))
    weights = []
    for p, s in zip(pairs, scores):
        s = s + bias_ref[first, p]
        sink = jnp.where(col2 < BLOCK, sink_ref[2 * p], sink_ref[2 * p + 1]) * LOG2E
        m = jnp.maximum(jnp.max(s, axis=0, keepdims=True), sink)
        e = jnp.exp2(s - m)
        denom = jnp.sum(e, axis=0, keepdims=True) + jnp.exp2(sink - m)
        weights.append((e.astype(BF16), denom))
    outs = []
    for p, (e, denom) in zip(pairs, weights):
        g = p // 2
        o = _dot(v2[g * HEAD_DIM:(g + 1) * HEAD_DIM], e) / denom
        outs += [o[:, :BLOCK], o[:, BLOCK:]]
    o_ref[...] = jnp.concatenate(outs, axis=0).T.astype(BF16)


def _swa(rel_flat, sinks, bkt_t, qa, ka2, vat):
    s = qa.shape[0]
    nb = s // BLOCK
    smem = pl.BlockSpec(memory_space=pltpu.SMEM)
    prev = lambda n: jnp.maximum(n - 1, 0)
    return pl.pallas_call(
        _swa_kernel,
        grid=(nb,),
        in_specs=[smem, smem, _const_spec(bkt_t.shape),
                  pl.BlockSpec((BLOCK, SWA_Q), lambda n: (n, 0)),
                  pl.BlockSpec((BLOCK, 2 * LANES), lambda n: (n, 0)),
                  pl.BlockSpec((BLOCK, 2 * LANES), lambda n: (prev(n), 0)),
                  pl.BlockSpec((SWA_KV, BLOCK), lambda n: (0, n)),
                  pl.BlockSpec((SWA_KV, BLOCK), lambda n: (0, prev(n)))],
        out_specs=pl.BlockSpec((BLOCK, SWA_Q), lambda n: (n, 0)),
        out_shape=jax.ShapeDtypeStruct((s, SWA_Q), BF16),
        scratch_shapes=[pltpu.VMEM((2, SWA_HEADS // 2, 2 * BLOCK, 2 * BLOCK), F32)],
        compiler_params=_params(("arbitrary",)),
        name="swa_attn",
    )(rel_flat, sinks, bkt_t, qa, ka2, ka2, vat, vat)


def _fox_kernel(jlo_ref, base_ref, q_ref, k_ref, ca_ref, vt_ref, o_ref):
    p = pl.program_id(0)
    i = pl.program_id(1)
    t = FOX_TILE
    tq = FOX_QSUB * t
    first_diag = i * FOX_QSUB
    lane = lax.broadcasted_iota(jnp.int32, (tq, LANES), 1)
    q = q_ref[...]
    wq = []
    for hh in range(2):
        in_head = (lane >= hh * HEAD_DIM) & (lane < (hh + 1) * HEAD_DIM)
        qm = jnp.where(in_head, q, jnp.zeros_like(q))
        sel = (lane >= N_SPLIT * hh) & (lane < N_SPLIT * (hh + 1))
        aug = jnp.where(sel, -1.0, 0.0).astype(BF16)
        wq.append(jnp.concatenate([qm, aug], axis=1))
    key = lax.broadcasted_iota(jnp.int32, (t, tq), 0)
    qry = lax.broadcasted_iota(jnp.int32, (t, tq), 1)
    causal = [key + d * t <= qry for d in range(FOX_QSUB)]

    def step(j, carry, heads, nsub, diagonal=False):
        start = pl.multiple_of(j * t, t)
        rows = nsub * t
        lhs = jnp.concatenate([k_ref[pl.ds(start, rows), :], ca_ref[pl.ds(start, rows), :]], axis=1)
        scores = [_dot_nt(lhs, wq[hh]) for hh in heads]
        mid = []
        for hh, s, (m, l, acc) in zip(heads, scores, carry):
            h = 2 * p + hh
            parts = [s[k * t:(k + 1) * t] for k in range(nsub)]
            if diagonal:
                for d in range(FOX_QSUB):
                    k = nsub - FOX_QSUB + d
                    parts[k] = jnp.where(causal[d], parts[k], NEG_BIG)
            djs = [base_ref[h, first_diag] - base_ref[h, j + k] for k in range(nsub)]
            m_new = m
            for part, dj in zip(parts, djs):
                m_new = jnp.maximum(m_new, jnp.max(part, axis=0, keepdims=True) + dj)
            alpha = jnp.exp2(m - m_new)
            es = [jnp.exp2(part + (dj - m_new)) for part, dj in zip(parts, djs)]
            l = alpha * l
            for e in es:
                l = l + jnp.sum(e, axis=0, keepdims=True)
            e_all = es[0] if nsub == 1 else jnp.concatenate(es, axis=0)
            mid.append((m_new, l, alpha, acc, e_all.astype(BF16)))
        new = []
        for hh, (m_new, l, alpha, acc, e_all) in zip(heads, mid):
            vt = vt_ref[hh * HEAD_DIM:(hh + 1) * HEAD_DIM, pl.ds(start, rows)]
            new.append((m_new, l, alpha * acc + _dot(vt, e_all)))
        return tuple(new)

    def run_alone(lo, hi, carry, heads):
        n_full = (hi - lo) // FOX_CHUNK
        carry = lax.fori_loop(0, n_full, lambda n, c: step(lo + FOX_CHUNK * n, c, heads, FOX_CHUNK), carry)
        rest = lo + FOX_CHUNK * n_full
        tails = [lambda c: c] + [functools.partial(lambda c, k: step(rest, c, heads, k), k=k)
                                 for k in range(1, FOX_CHUNK)]
        return lax.switch(hi - rest, tails, carry)

    def run_to_diagonal(lo, carry, heads):
        end = first_diag + FOX_QSUB
        count = end - lo
        last = (count - FOX_QSUB) % FOX_CHUNK + FOX_QSUB
        n_full = (count - last) // FOX_CHUNK
        carry = lax.fori_loop(0, n_full, lambda n, c: step(lo + FOX_CHUNK * n, c, heads, FOX_CHUNK), carry)
        tails = [functools.partial(lambda c, k: step(end - k, c, heads, k, diagonal=True), k=k)
                 for k in range(FOX_QSUB, FOX_QSUB + FOX_CHUNK)]
        return lax.switch(last - FOX_QSUB, tails, carry)

    lo0 = jlo_ref[2 * p, i]
    lo1 = jlo_ref[2 * p + 1, i]
    lo_both = jnp.maximum(lo0, lo1)
    init = (jnp.full((1, tq), NEG_BIG, F32), jnp.zeros((1, tq), F32), jnp.zeros((HEAD_DIM, tq), F32))
    (c0,) = run_alone(lo0, lo_both, (init,), (0,))
    (c1,) = run_alone(lo1, lo_both, (init,), (1,))
    carry = run_to_diagonal(lo_both, (c0, c1), (0, 1))
    ot = jnp.concatenate([carry[0][2] / carry[0][1], carry[1][2] / carry[1][1]], axis=0)
    o_ref[...] = ot.T.astype(BF16)


def _fox_schedule(cum, qn2, kn2):
    nt = qn2.shape[0]
    t = FOX_TILE
    bq = jnp.sqrt(qn2.reshape(nt, FOX_HEADS)) * NORM_SLACK
    bk = jnp.sqrt(kn2.reshape(nt, FOX_HEADS)) * NORM_SLACK
    c2 = cum.reshape(nt, t, FOX_HEADS) * LOG2E
    c_first, c_last = c2[:, 0, :], c2[:, t - 1, :]
    upper = (bq[:, None, :] * bk[None, :, :] + (bq * bk)[:, None, :] + c_first[:, None, :] - c_last[None, :, :])
    ii = jnp.arange(nt)[:, None, None]
    jj = jnp.arange(nt)[None, :, None]
    needed = (jj <= ii) & ((upper >= -SKIP_LOG2) | (jj == ii))
    jlo = jnp.min(jnp.where(needed, jj, nt), axis=1)
    jlo = jnp.min(jlo.reshape(nt // FOX_QSUB, FOX_QSUB, FOX_HEADS), axis=1).T.astype(jnp.int32)
    base = jnp.concatenate([jnp.zeros((1, FOX_HEADS), F32), c_last[:-1]], axis=0).T
    return jlo, base


def _fox(jlo, base, qb, kb, ca, vbt):
    s = qb.shape[0]
    t = FOX_QSUB * FOX_TILE
    grid_spec = pltpu.PrefetchScalarGridSpec(
        num_scalar_prefetch=2,
        grid=(N_PAIRS, s // t),
        in_specs=[pl.BlockSpec((t, LANES), lambda p, i, *_: (i, p)),
                  pl.BlockSpec((s, LANES), lambda p, i, *_: (0, p)),
                  pl.BlockSpec((s, LANES), lambda p, i, *_: (0, p)),
                  pl.BlockSpec((LANES, s), lambda p, i, *_: (p, 0))],
        out_specs=pl.BlockSpec((t, LANES), lambda p, i, *_: (i, p)),
    )
    return pl.pallas_call(
        _fox_kernel,
        grid_spec=grid_spec,
        out_shape=jax.ShapeDtypeStruct((s, FOX_W), BF16),
        compiler_params=_params(("arbitrary", "arbitrary")),
        name="fox_attn",
    )(jlo, base, qb, kb, ca, vbt)


def _rms(x, g):
    return x * lax.rsqrt(jnp.mean(x * x, axis=-1, keepdims=True) + RMS_EPS) * g


def _pack_rows(v):
    halves = []
    for j in range(2):
        lo = v[:, (2 * j) * PACK_W:(2 * j + 1) * PACK_W].astype(BF16).astype(F32)
        hi = v[:, (2 * j + 1) * PACK_W:(2 * j + 2) * PACK_W].astype(BF16).astype(F32)
        lo_bits = lax.bitcast_convert_type(lo, jnp.uint32)
        hi_bits = lax.bitcast_convert_type(hi, jnp.uint32)
        halves.append(hi_bits | (lo_bits >> 16))
    return halves


def _unpack_words(w):
    lo = lax.bitcast_convert_type(w << 16, F32)
    hi = lax.bitcast_convert_type(w & jnp.uint32(0xFFFF0000), F32)
    return lo, hi


def _outproj_kernel(x_ref, ya_ref, yb_ref, sga_ref, sgb_ref, pa_ref, pb_ref, wo_ref, gt_ref, g_ref, sh_ref, sc_ref,
                    wr_ref, br_ref, x1_ref, u2p_ref, e4_ref, r4_ref, w4_ref, cnt_ref, carry_ref):
    i = pl.program_id(0)
    tm = x_ref.shape[0]

    @pl.when(i == 0)
    def _():
        carry_ref[...] = jnp.zeros_like(carry_ref)

    merged = (sga_ref[...].astype(F32) * _dot(ya_ref[...], pa_ref[...])
              + sgb_ref[...].astype(F32) * _dot(yb_ref[...], pb_ref[...]))
    x1 = x_ref[...] + gt_ref[...] * _dot(merged.astype(BF16), wo_ref[...])
    x1_ref[...] = x1
    u2 = _rms(x1, g_ref[...]) * (1.0 + sc_ref[...]) + sh_ref[...]
    halves = _pack_rows(u2)
    u2p_ref[0] = halves[0]
    u2p_ref[1] = halves[1]

    logits = _dot(u2.astype(BF16), wr_ref[...]) + br_ref[...]
    eidx = lax.broadcasted_iota(jnp.int32, logits.shape, 1)
    work = logits
    sel = jnp.zeros(logits.shape, jnp.bool_)
    picks, vals = [], []
    for k in range(TOP_K):
        m = jnp.max(work, axis=-1, keepdims=True)
        first = jnp.min(jnp.where(work == m, eidx, N_EXPERTS), axis=-1, keepdims=True)
        hit = eidx == first
        sel = sel | hit
        work = jnp.where(hit, -jnp.inf, work)
        picks.append(first)
        vals.append(m)
    exps = [jnp.exp(v - vals[0]) for v in vals]
    denom = exps[0] + exps[1] + exps[2] + exps[3]

    r = lax.broadcasted_iota(jnp.int32, (tm, tm), 0)
    cc = lax.broadcasted_iota(jnp.int32, (tm, tm), 1)
    before = (cc < r).astype(BF16)
    chosen = sel.astype(BF16)
    rank = _dot(before, chosen) + carry_ref[...]
    cnt = carry_ref[...] + jnp.sum(chosen.astype(F32), axis=0, keepdims=True)
    carry_ref[...] = cnt
    cnt_ref[...] = cnt
    for k in range(TOP_K):
        e4_ref[:, k:k + 1] = picks[k]
        rk = jnp.sum(jnp.where(eidx == picks[k], rank, 0.0), axis=-1, keepdims=True)
        r4_ref[:, k:k + 1] = rk.astype(jnp.int32)
        w4_ref[:, k:k + 1] = exps[k] / denom


def _outproj(x2, ya, yb, sga, sgb, pa, pb, wo, gt, g, sh, sc, wr, br, tm):
    s = x2.shape[0]
    row = lambda n: pl.BlockSpec((tm, n), lambda i: (i, 0))
    consts = [pa, pb, wo, gt, g, sh, sc, wr, br]
    return pl.pallas_call(
        _outproj_kernel,
        grid=(s // tm,),
        in_specs=[row(D_MODEL), row(SWA_Q), row(FOX_W), row(D_MODEL), row(D_MODEL)] + [_const_spec(a.shape) for a in consts],
        out_specs=[row(D_MODEL), pl.BlockSpec((2, tm, PACK_W), lambda i: (0, i, 0)), row(TOP_K), row(TOP_K), row(TOP_K),
                   _const_spec((1, N_EXPERTS))],
        out_shape=[jax.ShapeDtypeStruct((s, D_MODEL), F32),
                   jax.ShapeDtypeStruct((2, s, PACK_W), jnp.uint32),
                   jax.ShapeDtypeStruct((s, TOP_K), jnp.int32),
                   jax.ShapeDtypeStruct((s, TOP_K), jnp.int32),
                   jax.ShapeDtypeStruct((s, TOP_K), F32),
                   jax.ShapeDtypeStruct((1, N_EXPERTS), F32)],
        scratch_shapes=[pltpu.VMEM((1, N_EXPERTS), F32)],
        compiler_params=_params(("arbitrary",)),
        name="out_proj_router",
    )(x2, ya, yb, sga, sgb, *consts)


def _sc_mesh():
    return plsc.VectorSubcoreMesh(core_axis_name="core", subcore_axis_name="subcore")


def _sc_dispatch(rows, idx, n_out):
    n, width = rows.shape

    @functools.partial(pl.kernel, out_type=jax.ShapeDtypeStruct((n_out, width), rows.dtype), mesh=_sc_mesh(),
                       scratch_types=[])
    def dispatch(x_hbm, i_hbm, o_hbm):
        def body(x_vmem, i_vmem):
            for k in range(TOP_K):
                pltpu.sync_copy(x_vmem, o_hbm.at[i_vmem.at[k]])

        pltpu.emit_pipeline(
            body, grid=(n // SC_WINDOW,),
            in_specs=[pl.BlockSpec((SC_WINDOW, width), lambda i: (i, 0)),
                      pl.BlockSpec((TOP_K, SC_WINDOW), lambda i: (0, i))],
            out_specs=[], core_axis_name=("core", "subcore"), dimension_semantics=(pltpu.PARALLEL,),
        )(x_hbm, i_hbm)

    return dispatch(rows, idx)


def _sc_gather(table, idx):
    n = idx.shape[1]
    width = table.shape[1]

    @functools.partial(pl.kernel, out_type=jax.ShapeDtypeStruct((n, width), table.dtype), mesh=_sc_mesh(),
                       scratch_types=[])
    def gather(t_hbm, i_hbm, o_hbm):
        def body(i_vmem, o_vmem):
            pltpu.sync_copy(t_hbm.at[i_vmem.at[0]], o_vmem)

        pltpu.emit_pipeline(
            body, grid=(n // SC_WINDOW,),
            in_specs=[pl.BlockSpec((1, SC_WINDOW), lambda i: (0, i))],
            out_specs=[pl.BlockSpec((SC_WINDOW, width), lambda i: (i, 0))],
            core_axis_name=("core", "subcore"), dimension_semantics=(pltpu.PARALLEL,),
        )(i_hbm, o_hbm)

    return gather(table, idx)


def _sc_pack_weights(w1, w2):
    r1, r2 = w1.shape[0], w2.shape[0]
    half = D_FF // 2

    def bf16_bits(v):
        u = plsc.bitcast(v, jnp.uint32)
        return (u + jnp.uint32(0x7FFF) + ((u >> 16) & jnp.uint32(1))) >> 16

    cp = pltpu.CompilerParams()
    if "needs_layout_passes" in pltpu.CompilerParams.__dataclass_fields__:
        cp = dataclasses.replace(cp, needs_layout_passes=False)

    @functools.partial(
        pl.kernel, mesh=_sc_mesh(), scratch_types=[], compiler_params=cp,
        out_type=(jax.ShapeDtypeStruct((r1, D_FF), jnp.uint32), jax.ShapeDtypeStruct((r2, D_MODEL // 2), jnp.uint32)))
    def pack(w1_hbm, w2_hbm, o1_hbm, o2_hbm):
        lanes = lax.iota(jnp.int32, SC_LANES)

        def body1(x_vmem, o_vmem):
            @pl.loop(0, SC_PACK_ROWS)
            def _(r):
                rr = jnp.full((SC_LANES,), r, jnp.int32)

                @pl.loop(0, half, step=SC_LANES)
                def _(c):
                    col = 2 * (c + lanes)
                    g0 = plsc.load_gather(x_vmem, [rr, col])
                    g1 = plsc.load_gather(x_vmem, [rr, col + 2 * half])
                    l0 = plsc.load_gather(x_vmem, [rr, col + 1])
                    l1 = plsc.load_gather(x_vmem, [rr, col + 2 * half + 1])
                    o_vmem[r, pl.ds(c, SC_LANES)] = bf16_bits(g0) | (bf16_bits(g1) << 16)
                    o_vmem[r, pl.ds(half + c, SC_LANES)] = bf16_bits(l0) | (bf16_bits(l1) << 16)

        def body2(x_vmem, o_vmem):
            @pl.loop(0, SC_PACK_ROWS)
            def _(r):
                @pl.loop(0, D_MODEL // 2, step=SC_LANES)
                def _(c):
                    lo = x_vmem[r, pl.ds(c, SC_LANES)]
                    hi = x_vmem[r, pl.ds(D_MODEL // 2 + c, SC_LANES)]
                    o_vmem[r, pl.ds(c, SC_LANES)] = bf16_bits(lo) | (bf16_bits(hi) << 16)

        for body, x_hbm, o_hbm, rows in ((body1, w1_hbm, o1_hbm, r1), (body2, w2_hbm, o2_hbm, r2)):
            pltpu.emit_pipeline(
                body, grid=(rows // SC_PACK_ROWS,),
                in_specs=[pl.BlockSpec((SC_PACK_ROWS, x_hbm.shape[1]), lambda i: (i, 0))],
                out_specs=[pl.BlockSpec((SC_PACK_ROWS, o_hbm.shape[1]), lambda i: (i, 0))],
                core_axis_name=("core", "subcore"), dimension_semantics=(pltpu.PARALLEL,),
            )(x_hbm, o_hbm)

    return pack(w1, w2)


def _moe_kernel(te_ref, first_ref, nact_ref, xs_ref, w1_ref, b1g_ref, b1l_ref, w2_ref, b2_ref, y_ref,
                wg_ref, wl_ref, w2b_ref):
    i = pl.program_id(0)
    live = i < nact_ref[0]

    @pl.when(jnp.logical_and(live, first_ref[i] == 1))
    def _():
        half = D_FF // 2
        for dst, words in ((wg_ref, w1_ref[0, :, :half]), (wl_ref, w1_ref[0, :, half:]), (w2b_ref, w2_ref[0])):
            lo, hi = _unpack_words(words)
            dst[:, :half] = lo.astype(BF16)
            dst[:, half:] = hi.astype(BF16)

    @pl.when(live)
    def _():
        chunks = []
        for j in range(2):
            lo, hi = _unpack_words(xs_ref[j])
            chunks += [lo.astype(BF16), hi.astype(BF16)]
        x = jnp.concatenate(chunks, axis=1)
        y = b2_ref[0]
        for n in range(D_FF // FF_SLAB):
            units = slice(n * FF_SLAB, (n + 1) * FF_SLAB)
            hg = _dot(x, wg_ref[:, units]) + b1g_ref[0, :, units]
            hl = _dot(x, wl_ref[:, units]) + b1l_ref[0, :, units]
            glu = jnp.minimum(hg, SWIGLU_LIMIT)
            lin = jnp.clip(hl, -SWIGLU_LIMIT, SWIGLU_LIMIT)
            a = glu * jax.nn.sigmoid(SWIGLU_ALPHA * glu) * (lin + 1.0)
            y = y + _dot(a.astype(BF16), w2b_ref[units, :])
        halves = _pack_rows(y)
        y_ref[0] = halves[0]
        y_ref[1] = halves[1]


def _moe_routed(tile_expert, tile_first, n_active, xs, w1, b1g, b1l, w2, b2):
    n_rows = xs.shape[1]
    n_tiles = n_rows // MOE_TILE
    live = lambda i, na: jnp.minimum(i, na[0] - 1)
    rows_spec = pl.BlockSpec((2, MOE_TILE, PACK_W), lambda i, te, tf, na: (0, live(i, na), 0))
    wsp = lambda a: pl.BlockSpec((1,) + a.shape[1:], lambda i, te, tf, na: (te[live(i, na)], 0, 0))
    grid_spec = pltpu.PrefetchScalarGridSpec(
        num_scalar_prefetch=3,
        grid=(n_tiles,),
        in_specs=[rows_spec, wsp(w1), wsp(b1g), wsp(b1l), wsp(w2), wsp(b2)],
        out_specs=rows_spec,
        scratch_shapes=[pltpu.VMEM((D_MODEL, D_FF), BF16), pltpu.VMEM((D_MODEL, D_FF), BF16),
                        pltpu.VMEM((D_FF, D_MODEL), BF16)],
    )
    return pl.pallas_call(
        _moe_kernel,
        grid_spec=grid_spec,
        out_shape=jax.ShapeDtypeStruct(xs.shape, jnp.uint32),
        compiler_params=_params(("arbitrary",)),
        name="moe_routed",
    )(tile_expert, tile_first, n_active, xs, w1, b1g, b1l, w2, b2)


def _final_kernel(x1_ref, yg_ref, w4_ref, gt_ref, gf_ref, *rest):
    o_ref = rest[-1]
    w4 = w4_ref[...]
    cols = []
    for j in range(2):
        lo_acc = hi_acc = None
        for k in range(TOP_K):
            lo, hi = _unpack_words(yg_ref[j, k])
            wk = w4[:, k:k + 1]
            lo_acc = wk * lo if lo_acc is None else lo_acc + wk * lo
            hi_acc = wk * hi if hi_acc is None else hi_acc + wk * hi
        cols += [lo_acc, hi_acc]
    moe = jnp.concatenate(cols, axis=1)
    x2 = x1_ref[...] + gt_ref[...] * moe
    o_ref[...] = _rms(x2, gf_ref[...])


def _final(x1, yg, w4, gt, gf, tm, first_tile, prev_out):
    s = x1.shape[0]
    n_tiles = yg.shape[2] // tm
    row = lambda n: pl.BlockSpec((tm, n), lambda i: (i + first_tile, 0))
    in_specs = [row(D_MODEL), pl.BlockSpec((2, TOP_K, tm, PACK_W), lambda i: (0, 0, i, 0)), row(TOP_K),
                _const_spec(gt.shape), _const_spec(gf.shape)]
    args = [x1, yg, w4, gt, gf]
    aliases = {}
    if prev_out is not None:
        in_specs.append(pl.BlockSpec(memory_space=pl.ANY))
        args.append(prev_out)
        aliases = {len(args) - 1: 0}
    return pl.pallas_call(
        _final_kernel,
        grid=(n_tiles,),
        in_specs=in_specs,
        out_specs=row(D_MODEL),
        out_shape=jax.ShapeDtypeStruct((s, D_MODEL), F32),
        input_output_aliases=aliases,
        compiler_params=_params(("arbitrary",)),
        name="combine_final_norm",
    )(*args)


def _routing_tables(e4, r4, counts, s):
    n_rows = TOP_K * s + N_EXPERTS * MOE_TILE
    cnt = counts.reshape(N_EXPERTS).astype(jnp.int32)
    padded = ((cnt + MOE_TILE - 1) // MOE_TILE) * MOE_TILE
    ends = jnp.cumsum(padded)
    starts = ends - padded
    pos = (jnp.take(starts, e4, axis=0) + r4).T
    tile_start = jnp.arange(n_rows // MOE_TILE, dtype=jnp.int32) * MOE_TILE
    tile_expert = jnp.minimum(jnp.sum(tile_start[:, None] >= ends[None, :], axis=1), N_EXPERTS - 1).astype(jnp.int32)
    tile_first = jnp.concatenate([jnp.ones((1,), jnp.int32), (tile_expert[1:] != tile_expert[:-1]).astype(jnp.int32)])
    n_active = (ends[-1:] // MOE_TILE).astype(jnp.int32)
    return pos, tile_expert, tile_first, n_active, n_rows


def kernel(x, c, w_ada, b_ada, g_mix, w_in, b_forget, sinks, rel_bias, w_proj_a, w_proj_b, w_out, g_ffn,
           w_router, b_router, w_e1, b_e1, w_e2, b_e2, g_final):
    b, s, d = x.shape
    assert b == 1 and d == D_MODEL and w_ada.shape[0] == 1
    x2 = x.reshape(s, d)
    tm = min(512, s)

    mod = _ada(c.reshape(d, 1), w_ada[0], b_ada)
    sh_m, sc_m, gt_m, sh_f, sc_f, gt_f = [mod[:, k * d:(k + 1) * d] for k in range(N_MOD)]

    w = w_in[0]
    o_ka, o_va, o_b = SWA_Q, SWA_Q + SWA_KV, SWA_Q + 2 * SWA_KV
    o_f = o_b + 3 * FOX_W
    o_g = o_f + FOX_HEADS
    dup = lambda m: jnp.concatenate([m[:, :HEAD_DIM], m[:, :HEAD_DIM], m[:, HEAD_DIM:], m[:, HEAD_DIM:]], axis=1)
    wa = jnp.concatenate([w[:, :SWA_Q], dup(w[:, o_ka:o_va])], axis=1).astype(BF16)
    pad = jnp.zeros((d, LANES - FOX_HEADS), F32)
    wb = jnp.concatenate([w[:, o_b:o_b + 2 * FOX_W], w[:, o_f:o_g], pad], axis=1).astype(BF16)
    wvt = jnp.concatenate([w[:, o_b + 2 * FOX_W:o_f], w[:, o_va:o_b]], axis=1).T.astype(BF16)
    wg = w[:, o_g:].astype(BF16)
    qa, ka2, vat, qb, kb, vbt, sga, sgb, cum, ca, qn2, kn2 = _inproj(
        x2, g_mix, sh_m, sc_m, wa, wb, wvt, wg, b_forget, tm)

    ya = _swa(rel_bias.reshape(-1), sinks[0], jnp.asarray(_t5_buckets_np().T), qa, ka2, vat)
    jlo, base = _fox_schedule(cum, qn2, kn2)
    yb = _fox(jlo, base, qb, kb, ca, vbt)

    x1, u2p, e4, r4, w4, counts = _outproj(
        x2, ya, yb, sga, sgb, w_proj_a[0].astype(BF16), w_proj_b[0].astype(BF16), w_out[0].astype(BF16),
        gt_m, g_ffn, sh_f, sc_f, w_router[0].astype(BF16), b_router, tm)

    pos, tile_expert, tile_first, n_active, n_rows = _routing_tables(e4, r4, counts, s)
    pos2 = jnp.concatenate([pos, pos + n_rows], axis=1)
    xs = _sc_dispatch(u2p.reshape(2 * s, PACK_W), pos2, 2 * n_rows).reshape(2, n_rows, PACK_W)

    w1p, w2p = _sc_pack_weights(w_e1[0].reshape(N_EXPERTS * d, 2 * D_FF), w_e2[0].reshape(N_EXPERTS * D_FF, d))
    b1 = b_e1[0].reshape(N_EXPERTS, D_FF, 2)
    b1g = b1[:, None, :, 0]
    b1l = b1[:, None, :, 1]
    ys = _moe_routed(tile_expert, tile_first, n_active, xs, w1p.reshape(N_EXPERTS, d, D_FF), b1g, b1l,
                     w2p.reshape(N_EXPERTS, D_FF, d // 2), b_e2[0][:, None, :])

    gather_idx = pos2.reshape(TOP_K, 2, s).transpose(1, 0, 2)
    sc_rows = s // COMBINE_CHUNKS
    out = None
    for ci in range(COMBINE_CHUNKS):
        idx = gather_idx[:, :, ci * sc_rows:(ci + 1) * sc_rows].reshape(1, -1)
        yg = _sc_gather(ys.reshape(2 * n_rows, PACK_W), idx).reshape(2, TOP_K, sc_rows, PACK_W)
        out = _final(x1, yg, w4, gt_f, g_final.reshape(1, d), tm, ci * (sc_rows // tm), out)
    return out.reshape(b, s, d)
```

```python
import dataclasses
import functools
import math

import numpy as np
import jax
import jax.numpy as jnp
from jax import lax
from jax.experimental import pallas as pl
from jax.experimental.pallas import tpu as pltpu
from jax.experimental.pallas import tpu_sc as plsc

D_MODEL = 1024
HEAD_DIM = 64
SWA_HEADS = 8
SWA_KV_HEADS = 2
WINDOW = 128
FOX_HEADS = 8
BLOCK = 128
REL_BUCKETS = 32
REL_MAX_DIST = WINDOW
N_EXPERTS = 32
TOP_K = 4
D_FF = D_MODEL
SWIGLU_LIMIT = 7.0
SWIGLU_ALPHA = 1.702
RMS_EPS = 1e-5
N_MOD = 6

SWA_Q = SWA_HEADS * HEAD_DIM
SWA_KV = SWA_KV_HEADS * HEAD_DIM
FOX_W = FOX_HEADS * HEAD_DIM
LANES = 128
N_PAIRS = FOX_HEADS // 2
NEG_BIG = -1e30
LOG2E = math.log2(math.e)
FOX_TILE = 256
FOX_QSUB = 2
FOX_CHUNK = 4
N_SPLIT = 3
SKIP_LOG2 = 127.0
NORM_SLACK = 1.01
PACK_W = 256
MOE_TILE = 512
SC_WINDOW = 128
FF_SLAB = 256
COMBINE_CHUNKS = 2
SC_LANES = 16
SC_PACK_ROWS = 8
VMEM_LIMIT = 56 * 1024 * 1024

F32 = jnp.float32
BF16 = jnp.bfloat16
HIGHEST = lax.Precision.HIGHEST


def _dot(a, b):
    return jnp.dot(a, b, preferred_element_type=F32)


def _dot_nt(a, b, precision=None):
    return lax.dot_general(a, b, (((1,), (1,)), ((), ())), preferred_element_type=F32, precision=precision)


def _const_spec(shape):
    nd = len(shape)
    return pl.BlockSpec(shape, lambda *_: (0,) * nd)


def _params(sem):
    return pltpu.CompilerParams(dimension_semantics=sem, vmem_limit_bytes=VMEM_LIMIT)


def _ada_kernel(c_ref, w_ref, b_ref, o_ref):
    c = c_ref[...]
    act = c * jax.nn.sigmoid(c)
    o_ref[...] = jnp.sum(act * w_ref[...], axis=0, keepdims=True) + b_ref[...]


def _ada(c_col, w_ada, b_ada):
    n = w_ada.shape[1]
    tn = 1024
    return pl.pallas_call(
        _ada_kernel,
        grid=(n // tn,),
        in_specs=[_const_spec((D_MODEL, 1)),
                  pl.BlockSpec((D_MODEL, tn), lambda j: (0, j)),
                  pl.BlockSpec((1, tn), lambda j: (0, j))],
        out_specs=pl.BlockSpec((1, tn), lambda j: (0, j)),
        out_shape=jax.ShapeDtypeStruct((1, n), F32),
        compiler_params=_params(("arbitrary",)),
        name="ada_mod",
    )(c_col, w_ada, b_ada)


def _split_bf16(v):
    parts = []
    for _ in range(N_SPLIT):
        p = v.astype(BF16)
        v = v - p.astype(F32)
        parts.append(p)
    return parts


def _inproj_kernel(x_ref, g_ref, sh_ref, sc_ref, wa_ref, wb_ref, wvt_ref, wg_ref, bf_ref, hind_ref, place_ref,
                   qa_ref, kva_ref, vat_ref, qb_ref, kb_ref, vbt_ref, sga_ref, sgb_ref, c_ref, ca_ref, qn_ref, kn_ref,
                   carry_ref):
    i = pl.program_id(0)
    tm = x_ref.shape[0]
    t = FOX_TILE
    nsub = tm // t

    @pl.when(i == 0)
    def _():
        carry_ref[...] = jnp.zeros_like(carry_ref)

    xf = x_ref[...]
    ms = jnp.mean(xf * xf, axis=-1, keepdims=True)
    y = xf * lax.rsqrt(ms + RMS_EPS) * g_ref[...]
    u = y * (1.0 + sc_ref[...]) + sh_ref[...]
    ub = u.astype(BF16)

    za = _dot(ub, wa_ref[...])
    qa_ref[...] = (za[:, :SWA_Q] * (HEAD_DIM ** -0.5 * LOG2E)).astype(BF16)
    kva_ref[...] = za[:, SWA_Q:].astype(BF16)

    zb = _dot(ub, wb_ref[...])
    qb = (zb[:, :FOX_W] * (HEAD_DIM ** -0.5 * LOG2E)).astype(BF16)
    kb = zb[:, FOX_W:2 * FOX_W].astype(BF16)
    qb_ref[...] = qb
    kb_ref[...] = kb
    vt = _dot_nt(wvt_ref[...], ub).astype(BF16)
    vbt_ref[...] = vt[:FOX_W]
    vat_ref[...] = vt[FOX_W:]

    def tile_norm_max(z, o_ref):
        zf = z.astype(F32)
        n2 = _dot((zf * zf).astype(BF16), hind_ref[...])
        for sb in range(nsub):
            o_ref[sb] = jnp.max(n2[sb * t:(sb + 1) * t], axis=0, keepdims=True)

    tile_norm_max(qb, qn_ref)
    tile_norm_max(kb, kn_ref)

    zg = _dot(ub, wg_ref[...])
    sg = jax.nn.sigmoid(zg)
    sga_ref[...] = sg[:, :D_MODEL].astype(BF16)
    sgb_ref[...] = sg[:, D_MODEL:].astype(BF16)

    fb = zb[:, 2 * FOX_W:2 * FOX_W + FOX_HEADS] + bf_ref[...]
    lf = jnp.minimum(fb, 0.0) - jnp.log1p(jnp.exp(-jnp.abs(fb)))
    r = lax.broadcasted_iota(jnp.int32, (t, t), 0)
    cc = lax.broadcasted_iota(jnp.int32, (t, t), 1)
    lower = (cc <= r).astype(BF16)
    carry = carry_ref[...]
    for sb in range(nsub):
        rows = slice(sb * t, (sb + 1) * t)
        local = sum(_dot(lower, p) for p in _split_bf16(lf[rows]))
        c_ref[rows, :] = local + carry
        carry = carry + local[t - 1:t, :]
        aug = sum(_dot(p, place_ref[k]) for k, p in enumerate(_split_bf16(local * LOG2E)))
        ca_ref[rows, :] = aug.astype(BF16)
    carry_ref[...] = carry


def _inproj(x2, g, sh, sc, wa, wb, wvt, wg, bfor, tm):
    s = x2.shape[0]
    nsub = tm // FOX_TILE
    row = lambda n: pl.BlockSpec((tm, n), lambda i: (i, 0))
    hind = np.zeros((FOX_W, FOX_HEADS), np.float32)
    hind[np.arange(FOX_W), np.arange(FOX_W) // HEAD_DIM] = 1.0
    place = np.zeros((N_SPLIT, FOX_HEADS, N_PAIRS * LANES), np.float32)
    for k in range(N_SPLIT):
        for h in range(FOX_HEADS):
            place[k, h, (h // 2) * LANES + N_SPLIT * (h % 2) + k] = 1.0
    hind = jnp.asarray(hind, BF16)
    place = jnp.asarray(place, BF16)
    out_shape = [
        jax.ShapeDtypeStruct((s, SWA_Q), BF16),
        jax.ShapeDtypeStruct((s, 2 * LANES), BF16),
        jax.ShapeDtypeStruct((SWA_KV, s), BF16),
        jax.ShapeDtypeStruct((s, FOX_W), BF16),
        jax.ShapeDtypeStruct((s, FOX_W), BF16),
        jax.ShapeDtypeStruct((FOX_W, s), BF16),
        jax.ShapeDtypeStruct((s, D_MODEL), BF16),
        jax.ShapeDtypeStruct((s, D_MODEL), BF16),
        jax.ShapeDtypeStruct((s, FOX_HEADS), F32),
        jax.ShapeDtypeStruct((s, N_PAIRS * LANES), BF16),
        jax.ShapeDtypeStruct((s // FOX_TILE, 1, FOX_HEADS), F32),
        jax.ShapeDtypeStruct((s // FOX_TILE, 1, FOX_HEADS), F32),
    ]
    stat = pl.BlockSpec((nsub, 1, FOX_HEADS), lambda i: (i, 0, 0))
    out_specs = [row(SWA_Q), row(2 * LANES), pl.BlockSpec((SWA_KV, tm), lambda i: (0, i)),
                 row(FOX_W), row(FOX_W), pl.BlockSpec((FOX_W, tm), lambda i: (0, i)),
                 row(D_MODEL), row(D_MODEL), row(FOX_HEADS), row(N_PAIRS * LANES), stat, stat]
    consts = [g, sh, sc, wa, wb, wvt, wg, bfor, hind, place]
    return pl.pallas_call(
        _inproj_kernel,
        grid=(s // tm,),
        in_specs=[row(D_MODEL)] + [_const_spec(a.shape) for a in consts],
        out_specs=out_specs,
        out_shape=out_shape,
        scratch_shapes=[pltpu.VMEM((1, FOX_HEADS), F32)],
        compiler_params=_params(("arbitrary",)),
        name="in_proj",
    )(x2, *consts)


def _t5_buckets_np():
    qi = np.arange(BLOCK)[:, None]
    kj = np.arange(2 * BLOCK)[None, :]
    dist = BLOCK + qi - kj
    n = np.maximum(dist, 0)
    max_exact = REL_BUCKETS // 2
    nf = np.maximum(n, 1).astype(np.float32)
    large = max_exact + (np.log(nf / np.float32(max_exact)) / np.float32(math.log(REL_MAX_DIST / max_exact))
                         * np.float32(REL_BUCKETS - max_exact)).astype(np.int32)
    large = np.minimum(large, REL_BUCKETS - 1)
    bucket = np.where(n < max_exact, n, large).astype(np.int32)
    band = (dist >= 0) & (dist < WINDOW)
    return np.where(band, bucket, -1).astype(np.int32)


def _swa_kernel(rel_ref, sink_ref, bkt_ref, q_ref, kc_ref, kp_ref, vc_ref, vp_ref, o_ref, bias_ref):
    n = pl.program_id(0)

    @pl.when(n == 0)
    def _():
        bkt = bkt_ref[...]
        prev = lax.broadcasted_iota(jnp.int32, bkt.shape, 0) < BLOCK
        for h in range(SWA_HEADS):
            acc = jnp.full(bkt.shape, NEG_BIG, F32)
            for b in range(REL_BUCKETS):
                acc = jnp.where(bkt == b, rel_ref[b * SWA_HEADS + h] * LOG2E, acc)
            cols = slice((h % 2) * BLOCK, (h % 2 + 1) * BLOCK)
            bias_ref[0, h // 2, :, cols] = acc
            bias_ref[1, h // 2, :, cols] = jnp.where(prev, NEG_BIG, acc)

    first = jnp.where(n == 0, 1, 0)
    lane = lax.broadcasted_iota(jnp.int32, (BLOCK, LANES), 1)
    col2 = lax.broadcasted_iota(jnp.int32, (1, 2 * BLOCK), 1)
    k2 = jnp.concatenate([kp_ref[...], kc_ref[...]], axis=0)
    v2 = jnp.concatenate([vp_ref[...], vc_ref[...]], axis=1)
    pairs = range(SWA_HEADS // 2)
    scores = []
    for p in pairs:
        qp = q_ref[:, p * LANES:(p + 1) * LANES]
        zero = jnp.zeros_like(qp)
        qs = jnp.concatenate([jnp.where(lane < HEAD_DIM, qp, zero), jnp.where(lane >= HEAD_DIM, qp, zero)], axis=0)
        g = p // 2
        scores.append(_dot_nt(k2[:, g * LANES:(g + 1) * LANES], qs))
    weights = []
    for p, s in zip(pairs, scores):
        s = s + bias_ref[first, p]
        sink = jnp.where(col2 < BLOCK, sink_ref[2 * p], sink_ref[2 * p + 1]) * LOG2E
        m = jnp.maximum(jnp.max(s, axis=0, keepdims=True), sink)
        e = jnp.exp2(s - m)
        denom = jnp.sum(e, axis=0, keepdims=True) + jnp.exp2(sink - m)
        weights.append((e.astype(BF16), denom))
    outs = []
    for p, (e, denom) in zip(pairs, weights):
        g = p // 2
        o = _dot(v2[g * HEAD_DIM:(g + 1) * HEAD_DIM], e) / denom
        outs += [o[:, :BLOCK], o[:, BLOCK:]]
    o_ref[...] = jnp.concatenate(outs, axis=0).T.astype(BF16)


def _swa(rel_flat, sinks, bkt_t, qa, ka2, vat):
    s = qa.shape[0]
    nb = s // BLOCK
    smem = pl.BlockSpec(memory_space=pltpu.SMEM)
    prev = lambda n: jnp.maximum(n - 1, 0)
    return pl.pallas_call(
        _swa_kernel,
        grid=(nb,),
        in_specs=[smem, smem, _const_spec(bkt_t.shape),
                  pl.BlockSpec((BLOCK, SWA_Q), lambda n: (n, 0)),
                  pl.BlockSpec((BLOCK, 2 * LANES), lambda n: (n, 0)),
                  pl.BlockSpec((BLOCK, 2 * LANES), lambda n: (prev(n), 0)),
                  pl.BlockSpec((SWA_KV, BLOCK), lambda n: (0, n)),
                  pl.BlockSpec((SWA_KV, BLOCK), lambda n: (0, prev(n)))],
        out_specs=pl.BlockSpec((BLOCK, SWA_Q), lambda n: (n, 0)),
        out_shape=jax.ShapeDtypeStruct((s, SWA_Q), BF16),
        scratch_shapes=[pltpu.VMEM((2, SWA_HEADS // 2, 2 * BLOCK, 2 * BLOCK), F32)],
        compiler_params=_params(("arbitrary",)),
        name="swa_attn",
    )(rel_flat, sinks, bkt_t, qa, ka2, ka2, vat, vat)


def _fox_kernel(jlo_ref, base_ref, q_ref, k_ref, ca_ref, vt_ref, o_ref):
    p = pl.program_id(0)
    i = pl.program_id(1)
    t = FOX_TILE
    tq = FOX_QSUB * t
    first_diag = i * FOX_QSUB
    lane = lax.broadcasted_iota(jnp.int32, (tq, LANES), 1)
    q = q_ref[...]
    wq = []
    for hh in range(2):
        in_head = (lane >= hh * HEAD_DIM) & (lane < (hh + 1) * HEAD_DIM)
        qm = jnp.where(in_head, q, jnp.zeros_like(q))
        sel = (lane >= N_SPLIT * hh) & (lane < N_SPLIT * (hh + 1))
        aug = jnp.where(sel, -1.0, 0.0).astype(BF16)
        wq.append(jnp.concatenate([qm, aug], axis=1))
    key = lax.broadcasted_iota(jnp.int32, (t, tq), 0)
    qry = lax.broadcasted_iota(jnp.int32, (t, tq), 1)
    causal = [key + d * t <= qry for d in range(FOX_QSUB)]

    def step(j, carry, heads, nsub, diagonal=False):
        start = pl.multiple_of(j * t, t)
        rows = nsub * t
        lhs = jnp.concatenate([k_ref[pl.ds(start, rows), :], ca_ref[pl.ds(start, rows), :]], axis=1)
        scores = [_dot_nt(lhs, wq[hh]) for hh in heads]
        mid = []
        for hh, s, (m, l, acc) in zip(heads, scores, carry):
            h = 2 * p + hh
            parts = [s[k * t:(k + 1) * t] for k in range(nsub)]
            if diagonal:
                for d in range(FOX_QSUB):
                    k = nsub - FOX_QSUB + d
                    parts[k] = jnp.where(causal[d], parts[k], NEG_BIG)
            djs = [base_ref[h, first_diag] - base_ref[h, j + k] for k in range(nsub)]
            m_new = m
            for part, dj in zip(parts, djs):
                m_new = jnp.maximum(m_new, jnp.max(part, axis=0, keepdims=True) + dj)
            alpha = jnp.exp2(m - m_new)
            es = [jnp.exp2(part + (dj - m_new)) for part, dj in zip(parts, djs)]
            l = alpha * l
            for e in es:
                l = l + jnp.sum(e, axis=0, keepdims=True)
            e_all = es[0] if nsub == 1 else jnp.concatenate(es, axis=0)
            mid.append((m_new, l, alpha, acc, e_all.astype(BF16)))
        new = []
        for hh, (m_new, l, alpha, acc, e_all) in zip(heads, mid):
            vt = vt_ref[hh * HEAD_DIM:(hh + 1) * HEAD_DIM, pl.ds(start, rows)]
            new.append((m_new, l, alpha * acc + _dot(vt, e_all)))
        return tuple(new)

    def run_alone(lo, hi, carry, heads):
        n_full = (hi - lo) // FOX_CHUNK
        carry = lax.fori_loop(0, n_full, lambda n, c: step(lo + FOX_CHUNK * n, c, heads, FOX_CHUNK), carry)
        rest = lo + FOX_CHUNK * n_full
        tails = [lambda c: c] + [functools.partial(lambda c, k: step(rest, c, heads, k), k=k)
                                 for k in range(1, FOX_CHUNK)]
        return lax.switch(hi - rest, tails, carry)

    def run_to_diagonal(lo, carry, heads):
        end = first_diag + FOX_QSUB
        count = end - lo
        last = (count - FOX_QSUB) % FOX_CHUNK + FOX_QSUB
        n_full = (count - last) // FOX_CHUNK
        carry = lax.fori_loop(0, n_full, lambda n, c: step(lo + FOX_CHUNK * n, c, heads, FOX_CHUNK), carry)
        tails = [functools.partial(lambda c, k: step(end - k, c, heads, k, diagonal=True), k=k)
                 for k in range(FOX_QSUB, FOX_QSUB + FOX_CHUNK)]
        return lax.switch(last - FOX_QSUB, tails, carry)

    lo0 = jlo_ref[2 * p, i]
    lo1 = jlo_ref[2 * p + 1, i]
    lo_both = jnp.maximum(lo0, lo1)
    init = (jnp.full((1, tq), NEG_BIG, F32), jnp.zeros((1, tq), F32), jnp.zeros((HEAD_DIM, tq), F32))
    (c0,) = run_alone(lo0, lo_both, (init,), (0,))
    (c1,) = run_alone(lo1, lo_both, (init,), (1,))
    carry = run_to_diagonal(lo_both, (c0, c1), (0, 1))
    ot = jnp.concatenate([carry[0][2] / carry[0][1], carry[1][2] / carry[1][1]], axis=0)
    o_ref[...] = ot.T.astype(BF16)


def _fox_schedule(cum, qn2, kn2):
    nt = qn2.shape[0]
    t = FOX_TILE
    bq = jnp.sqrt(qn2.reshape(nt, FOX_HEADS)) * NORM_SLACK
    bk = jnp.sqrt(kn2.reshape(nt, FOX_HEADS)) * NORM_SLACK
    c2 = cum.reshape(nt, t, FOX_HEADS) * LOG2E
    c_first, c_last = c2[:, 0, :], c2[:, t - 1, :]
    upper = (bq[:, None, :] * bk[None, :, :] + (bq * bk)[:, None, :] + c_first[:, None, :] - c_last[None, :, :])
    ii = jnp.arange(nt)[:, None, None]
    jj = jnp.arange(nt)[None, :, None]
    needed = (jj <= ii) & ((upper >= -SKIP_LOG2) | (jj == ii))
    jlo = jnp.min(jnp.where(needed, jj, nt), axis=1)
    jlo = jnp.min(jlo.reshape(nt // FOX_QSUB, FOX_QSUB, FOX_HEADS), axis=1).T.astype(jnp.int32)
    base = jnp.concatenate([jnp.zeros((1, FOX_HEADS), F32), c_last[:-1]], axis=0).T
    return jlo, base


def _fox(jlo, base, qb, kb, ca, vbt):
    s = qb.shape[0]
    t = FOX_QSUB * FOX_TILE
    grid_spec = pltpu.PrefetchScalarGridSpec(
        num_scalar_prefetch=2,
        grid=(N_PAIRS, s // t),
        in_specs=[pl.BlockSpec((t, LANES), lambda p, i, *_: (i, p)),
                  pl.BlockSpec((s, LANES), lambda p, i, *_: (0, p)),
                  pl.BlockSpec((s, LANES), lambda p, i, *_: (0, p)),
                  pl.BlockSpec((LANES, s), lambda p, i, *_: (p, 0))],
        out_specs=pl.BlockSpec((t, LANES), lambda p, i, *_: (i, p)),
    )
    return pl.pallas_call(
        _fox_kernel,
        grid_spec=grid_spec,
        out_shape=jax.ShapeDtypeStruct((s, FOX_W), BF16),
        compiler_params=_params(("arbitrary", "arbitrary")),
        name="fox_attn",
    )(jlo, base, qb, kb, ca, vbt)


def _rms(x, g):
    return x * lax.rsqrt(jnp.mean(x * x, axis=-1, keepdims=True) + RMS_EPS) * g


def _pack_rows(v):
    halves = []
    for j in range(2):
        lo = v[:, (2 * j) * PACK_W:(2 * j + 1) * PACK_W].astype(BF16).astype(F32)
        hi = v[:, (2 * j + 1) * PACK_W:(2 * j + 2) * PACK_W].astype(BF16).astype(F32)
        lo_bits = lax.bitcast_convert_type(lo, jnp.uint32)
        hi_bits = lax.bitcast_convert_type(hi, jnp.uint32)
        halves.append(hi_bits | (lo_bits >> 16))
    return halves


def _unpack_words(w):
    lo = lax.bitcast_convert_type(w << 16, F32)
    hi = lax.bitcast_convert_type(w & jnp.uint32(0xFFFF0000), F32)
    return lo, hi


def _outproj_kernel(x_ref, ya_ref, yb_ref, sga_ref, sgb_ref, pa_ref, pb_ref, wo_ref, gt_ref, g_ref, sh_ref, sc_ref,
                    wr_ref, br_ref, x1_ref, u2p_ref, e4_ref, r4_ref, w4_ref, cnt_ref, carry_ref):
    i = pl.program_id(0)
    tm = x_ref.shape[0]

    @pl.when(i == 0)
    def _():
        carry_ref[...] = jnp.zeros_like(carry_ref)

    merged = (sga_ref[...].astype(F32) * _dot(ya_ref[...], pa_ref[...])
              + sgb_ref[...].astype(F32) * _dot(yb_ref[...], pb_ref[...]))
    x1 = x_ref[...] + gt_ref[...] * _dot(merged.astype(BF16), wo_ref[...])
    x1_ref[...] = x1
    u2 = _rms(x1, g_ref[...]) * (1.0 + sc_ref[...]) + sh_ref[...]
    halves = _pack_rows(u2)
    u2p_ref[0] = halves[0]
    u2p_ref[1] = halves[1]

    logits = _dot(u2.astype(BF16), wr_ref[...]) + br_ref[...]
    eidx = lax.broadcasted_iota(jnp.int32, logits.shape, 1)
    work = logits
    sel = jnp.zeros(logits.shape, jnp.bool_)
    picks, vals = [], []
    for k in range(TOP_K):
        m = jnp.max(work, axis=-1, keepdims=True)
        first = jnp.min(jnp.where(work == m, eidx, N_EXPERTS), axis=-1, keepdims=True)
        hit = eidx == first
        sel = sel | hit
        work = jnp.where(hit, -jnp.inf, work)
        picks.append(first)
        vals.append(m)
    exps = [jnp.exp(v - vals[0]) for v in vals]
    denom = exps[0] + exps[1] + exps[2] + exps[3]

    r = lax.broadcasted_iota(jnp.int32, (tm, tm), 0)
    cc = lax.broadcasted_iota(jnp.int32, (tm, tm), 1)
    before = (cc < r).astype(BF16)
    chosen = sel.astype(BF16)
    rank = _dot(before, chosen) + carry_ref[...]
    cnt = carry_ref[...] + jnp.sum(chosen.astype(F32), axis=0, keepdims=True)
    carry_ref[...] = cnt
    cnt_ref[...] = cnt
    for k in range(TOP_K):
        e4_ref[:, k:k + 1] = picks[k]
        rk = jnp.sum(jnp.where(eidx == picks[k], rank, 0.0), axis=-1, keepdims=True)
        r4_ref[:, k:k + 1] = rk.astype(jnp.int32)
        w4_ref[:, k:k + 1] = exps[k] / denom


def _outproj(x2, ya, yb, sga, sgb, pa, pb, wo, gt, g, sh, sc, wr, br, tm):
    s = x2.shape[0]
    row = lambda n: pl.BlockSpec((tm, n), lambda i: (i, 0))
    consts = [pa, pb, wo, gt, g, sh, sc, wr, br]
    return pl.pallas_call(
        _outproj_kernel,
        grid=(s // tm,),
        in_specs=[row(D_MODEL), row(SWA_Q), row(FOX_W), row(D_MODEL), row(D_MODEL)] + [_const_spec(a.shape) for a in consts],
        out_specs=[row(D_MODEL), pl.BlockSpec((2, tm, PACK_W), lambda i: (0, i, 0)), row(TOP_K), row(TOP_K), row(TOP_K),
                   _const_spec((1, N_EXPERTS))],
        out_shape=[jax.ShapeDtypeStruct((s, D_MODEL), F32),
                   jax.ShapeDtypeStruct((2, s, PACK_W), jnp.uint32),
                   jax.ShapeDtypeStruct((s, TOP_K), jnp.int32),
                   jax.ShapeDtypeStruct((s, TOP_K), jnp.int32),
                   jax.ShapeDtypeStruct((s, TOP_K), F32),
                   jax.ShapeDtypeStruct((1, N_EXPERTS), F32)],
        scratch_shapes=[pltpu.VMEM((1, N_EXPERTS), F32)],
        compiler_params=_params(("arbitrary",)),
        name="out_proj_router",
    )(x2, ya, yb, sga, sgb, *consts)


def _sc_mesh():
    return plsc.VectorSubcoreMesh(core_axis_name="core", subcore_axis_name="subcore")


def _sc_dispatch(rows, idx, n_out):
    n, width = rows.shape

    @functools.partial(pl.kernel, out_type=jax.ShapeDtypeStruct((n_out, width), rows.dtype), mesh=_sc_mesh(),
                       scratch_types=[])
    def dispatch(x_hbm, i_hbm, o_hbm):
        def body(x_vmem, i_vmem):
            for k in range(TOP_K):
                pltpu.sync_copy(x_vmem, o_hbm.at[i_vmem.at[k]])

        pltpu.emit_pipeline(
            body, grid=(n // SC_WINDOW,),
            in_specs=[pl.BlockSpec((SC_WINDOW, width), lambda i: (i, 0)),
                      pl.BlockSpec((TOP_K, SC_WINDOW), lambda i: (0, i))],
            out_specs=[], core_axis_name=("core", "subcore"), dimension_semantics=(pltpu.PARALLEL,),
        )(x_hbm, i_hbm)

    return dispatch(rows, idx)


def _sc_gather(table, idx):
    n = idx.shape[1]
    width = table.shape[1]

    @functools.partial(pl.kernel, out_type=jax.ShapeDtypeStruct((n, width), table.dtype), mesh=_sc_mesh(),
                       scratch_types=[])
    def gather(t_hbm, i_hbm, o_hbm):
        def body(i_vmem, o_vmem):
            pltpu.sync_copy(t_hbm.at[i_vmem.at[0]], o_vmem)

        pltpu.emit_pipeline(
            body, grid=(n // SC_WINDOW,),
            in_specs=[pl.BlockSpec((1, SC_WINDOW), lambda i: (0, i))],
            out_specs=[pl.BlockSpec((SC_WINDOW, width), lambda i: (i, 0))],
            core_axis_name=("core", "subcore"), dimension_semantics=(pltpu.PARALLEL,),
        )(i_hbm, o_hbm)

    return gather(table, idx)


def _sc_pack_weights(w1, w2):
    r1, r2 = w1.shape[0], w2.shape[0]
    half = D_FF // 2

    def bf16_bits(v):
        u = plsc.bitcast(v, jnp.uint32)
        return (u + jnp.uint32(0x7FFF) + ((u >> 16) & jnp.uint32(1))) >> 16

    cp = pltpu.CompilerParams()
    if "needs_layout_passes" in pltpu.CompilerParams.__dataclass_fields__:
        cp = dataclasses.replace(cp, needs_layout_passes=False)

    n_in = w1.size + w2.size
    cost = pl.CostEstimate(flops=8 * n_in, transcendentals=0, bytes_accessed=6 * n_in)
    @functools.partial(
        pl.kernel, mesh=_sc_mesh(), scratch_types=[], compiler_params=cp, cost_estimate=cost,
        out_type=(jax.ShapeDtypeStruct((r1, D_FF), jnp.uint32), jax.ShapeDtypeStruct((r2, D_MODEL // 2), jnp.uint32)))
    def pack(w1_hbm, w2_hbm, o1_hbm, o2_hbm):
        lanes = lax.iota(jnp.int32, SC_LANES)

        def body1(x_vmem, o_vmem):
            @pl.loop(0, SC_PACK_ROWS)
            def _(r):
                rr = jnp.full((SC_LANES,), r, jnp.int32)

                @pl.loop(0, half, step=SC_LANES)
                def _(c):
                    col = 2 * (c + lanes)
                    g0 = plsc.load_gather(x_vmem, [rr, col])
                    g1 = plsc.load_gather(x_vmem, [rr, col + 2 * half])
                    l0 = plsc.load_gather(x_vmem, [rr, col + 1])
                    l1 = plsc.load_gather(x_vmem, [rr, col + 2 * half + 1])
                    o_vmem[r, pl.ds(c, SC_LANES)] = bf16_bits(g0) | (bf16_bits(g1) << 16)
                    o_vmem[r, pl.ds(half + c, SC_LANES)] = bf16_bits(l0) | (bf16_bits(l1) << 16)

        def body2(x_vmem, o_vmem):
            @pl.loop(0, SC_PACK_ROWS)
            def _(r):
                @pl.loop(0, D_MODEL // 2, step=SC_LANES)
                def _(c):
                    lo = x_vmem[r, pl.ds(c, SC_LANES)]
                    hi = x_vmem[r, pl.ds(D_MODEL // 2 + c, SC_LANES)]
                    o_vmem[r, pl.ds(c, SC_LANES)] = bf16_bits(lo) | (bf16_bits(hi) << 16)

        for body, x_hbm, o_hbm, rows in ((body1, w1_hbm, o1_hbm, r1), (body2, w2_hbm, o2_hbm, r2)):
            pltpu.emit_pipeline(
                body, grid=(rows // SC_PACK_ROWS,),
                in_specs=[pl.BlockSpec((SC_PACK_ROWS, x_hbm.shape[1]), lambda i: (i, 0))],
                out_specs=[pl.BlockSpec((SC_PACK_ROWS, o_hbm.shape[1]), lambda i: (i, 0))],
                core_axis_name=("core", "subcore"), dimension_semantics=(pltpu.PARALLEL,),
            )(x_hbm, o_hbm)

    return pack(w1, w2)


def _moe_kernel(te_ref, first_ref, nact_ref, xs_ref, w1_ref, b1g_ref, b1l_ref, w2_ref, b2_ref, y_ref,
                wg_ref, wl_ref, w2b_ref):
    i = pl.program_id(0)
    live = i < nact_ref[0]

    @pl.when(jnp.logical_and(live, first_ref[i] == 1))
    def _():
        half = D_FF // 2
        for dst, words in ((wg_ref, w1_ref[0, :, :half]), (wl_ref, w1_ref[0, :, half:]), (w2b_ref, w2_ref[0])):
            lo, hi = _unpack_words(words)
            dst[:, :half] = lo.astype(BF16)
            dst[:, half:] = hi.astype(BF16)

    @pl.when(live)
    def _():
        chunks = []
        for j in range(2):
            lo, hi = _unpack_words(xs_ref[j])
            chunks += [lo.astype(BF16), hi.astype(BF16)]
        x = jnp.concatenate(chunks, axis=1)
        y = b2_ref[0]
        for n in range(D_FF // FF_SLAB):
            units = slice(n * FF_SLAB, (n + 1) * FF_SLAB)
            hg = _dot(x, wg_ref[:, units]) + b1g_ref[0, :, units]
            hl = _dot(x, wl_ref[:, units]) + b1l_ref[0, :, units]
            glu = jnp.minimum(hg, SWIGLU_LIMIT)
            lin = jnp.clip(hl, -SWIGLU_LIMIT, SWIGLU_LIMIT)
            a = glu * jax.nn.sigmoid(SWIGLU_ALPHA * glu) * (lin + 1.0)
            y = y + _dot(a.astype(BF16), w2b_ref[units, :])
        halves = _pack_rows(y)
        y_ref[0] = halves[0]
        y_ref[1] = halves[1]


def _moe_routed(tile_expert, tile_first, n_active, xs, w1, b1g, b1l, w2, b2):
    n_rows = xs.shape[1]
    n_tiles = n_rows // MOE_TILE
    live = lambda i, na: jnp.minimum(i, na[0] - 1)
    rows_spec = pl.BlockSpec((2, MOE_TILE, PACK_W), lambda i, te, tf, na: (0, live(i, na), 0))
    wsp = lambda a: pl.BlockSpec((1,) + a.shape[1:], lambda i, te, tf, na: (te[live(i, na)], 0, 0))
    grid_spec = pltpu.PrefetchScalarGridSpec(
        num_scalar_prefetch=3,
        grid=(n_tiles,),
        in_specs=[rows_spec, wsp(w1), wsp(b1g), wsp(b1l), wsp(w2), wsp(b2)],
        out_specs=rows_spec,
        scratch_shapes=[pltpu.VMEM((D_MODEL, D_FF), BF16), pltpu.VMEM((D_MODEL, D_FF), BF16),
                        pltpu.VMEM((D_FF, D_MODEL), BF16)],
    )
    return pl.pallas_call(
        _moe_kernel,
        grid_spec=grid_spec,
        out_shape=jax.ShapeDtypeStruct(xs.shape, jnp.uint32),
        compiler_params=_params(("arbitrary",)),
        name="moe_routed",
    )(tile_expert, tile_first, n_active, xs, w1, b1g, b1l, w2, b2)


def _final_kernel(x1_ref, yg_ref, w4_ref, gt_ref, gf_ref, *rest):
    o_ref = rest[-1]
    w4 = w4_ref[...]
    cols = []
    for j in range(2):
        lo_acc = hi_acc = None
        for k in range(TOP_K):
            lo, hi = _unpack_words(yg_ref[j, k])
            wk = w4[:, k:k + 1]
            lo_acc = wk * lo if lo_acc is None else lo_acc + wk * lo
            hi_acc = wk * hi if hi_acc is None else hi_acc + wk * hi
        cols += [lo_acc, hi_acc]
    moe = jnp.concatenate(cols, axis=1)
    x2 = x1_ref[...] + gt_ref[...] * moe
    o_ref[...] = _rms(x2, gf_ref[...])


def _final(x1, yg, w4, gt, gf, tm, first_tile, prev_out):
    s = x1.shape[0]
    n_tiles = yg.shape[2] // tm
    row = lambda n: pl.BlockSpec((tm, n), lambda i: (i + first_tile, 0))
    in_specs = [row(D_MODEL), pl.BlockSpec((2, TOP_K, tm, PACK_W), lambda i: (0, 0, i, 0)), row(TOP_K),
                _const_spec(gt.shape), _const_spec(gf.shape)]
    args = [x1, yg, w4, gt, gf]
    aliases = {}
    if prev_out is not None:
        in_specs.append(pl.BlockSpec(memory_space=pl.ANY))
        args.append(prev_out)
        aliases = {len(args) - 1: 0}
    return pl.pallas_call(
        _final_kernel,
        grid=(n_tiles,),
        in_specs=in_specs,
        out_specs=row(D_MODEL),
        out_shape=jax.ShapeDtypeStruct((s, D_MODEL), F32),
        input_output_aliases=aliases,
        compiler_params=_params(("arbitrary",)),
        name="combine_final_norm",
    )(*args)


def _routing_tables(e4, r4, counts, s):
    n_rows = TOP_K * s + N_EXPERTS * MOE_TILE
    cnt = counts.reshape(N_EXPERTS).astype(jnp.int32)
    padded = ((cnt + MOE_TILE - 1) // MOE_TILE) * MOE_TILE
    ends = jnp.cumsum(padded)
    starts = ends - padded
    pos = (jnp.take(starts, e4, axis=0) + r4).T
    tile_start = jnp.arange(n_rows // MOE_TILE, dtype=jnp.int32) * MOE_TILE
    tile_expert = jnp.minimum(jnp.sum(tile_start[:, None] >= ends[None, :], axis=1), N_EXPERTS - 1).astype(jnp.int32)
    tile_first = jnp.concatenate([jnp.ones((1,), jnp.int32), (tile_expert[1:] != tile_expert[:-1]).astype(jnp.int32)])
    n_active = (ends[-1:] // MOE_TILE).astype(jnp.int32)
    return pos, tile_expert, tile_first, n_active, n_rows


def kernel(x, c, w_ada, b_ada, g_mix, w_in, b_forget, sinks, rel_bias, w_proj_a, w_proj_b, w_out, g_ffn,
           w_router, b_router, w_e1, b_e1, w_e2, b_e2, g_final):
    b, s, d = x.shape
    assert b == 1 and d == D_MODEL and w_ada.shape[0] == 1
    x2 = x.reshape(s, d)
    tm = min(512, s)

    mod = _ada(c.reshape(d, 1), w_ada[0], b_ada)
    sh_m, sc_m, gt_m, sh_f, sc_f, gt_f = [mod[:, k * d:(k + 1) * d] for k in range(N_MOD)]

    w = w_in[0]
    o_ka, o_va, o_b = SWA_Q, SWA_Q + SWA_KV, SWA_Q + 2 * SWA_KV
    o_f = o_b + 3 * FOX_W
    o_g = o_f + FOX_HEADS
    dup = lambda m: jnp.concatenate([m[:, :HEAD_DIM], m[:, :HEAD_DIM], m[:, HEAD_DIM:], m[:, HEAD_DIM:]], axis=1)
    wa = jnp.concatenate([w[:, :SWA_Q], dup(w[:, o_ka:o_va])], axis=1).astype(BF16)
    pad = jnp.zeros((d, LANES - FOX_HEADS), F32)
    wb = jnp.concatenate([w[:, o_b:o_b + 2 * FOX_W], w[:, o_f:o_g], pad], axis=1).astype(BF16)
    wvt = jnp.concatenate([w[:, o_b + 2 * FOX_W:o_f], w[:, o_va:o_b]], axis=1).T.astype(BF16)
    wg = w[:, o_g:].astype(BF16)
    qa, ka2, vat, qb, kb, vbt, sga, sgb, cum, ca, qn2, kn2 = _inproj(
        x2, g_mix, sh_m, sc_m, wa, wb, wvt, wg, b_forget, tm)

    ya = _swa(rel_bias.reshape(-1), sinks[0], jnp.asarray(_t5_buckets_np().T), qa, ka2, vat)
    jlo, base = _fox_schedule(cum, qn2, kn2)
    yb = _fox(jlo, base, qb, kb, ca, vbt)

    x1, u2p, e4, r4, w4, counts = _outproj(
        x2, ya, yb, sga, sgb, w_proj_a[0].astype(BF16), w_proj_b[0].astype(BF16), w_out[0].astype(BF16),
        gt_m, g_ffn, sh_f, sc_f, w_router[0].astype(BF16), b_router, tm)

    pos, tile_expert, tile_first, n_active, n_rows = _routing_tables(e4, r4, counts, s)
    pos2 = jnp.concatenate([pos, pos + n_rows], axis=1)
    w1p, w2p = _sc_pack_weights(w_e1[0].reshape(N_EXPERTS * d, 2 * D_FF), w_e2[0].reshape(N_EXPERTS * D_FF, d))
    pos2, w1p, w2p = lax.optimization_barrier((pos2, w1p, w2p))
    xs = _sc_dispatch(u2p.reshape(2 * s, PACK_W), pos2, 2 * n_rows).reshape(2, n_rows, PACK_W)

    b1 = b_e1[0].reshape(N_EXPERTS, D_FF, 2)
    b1g = b1[:, None, :, 0]
    b1l = b1[:, None, :, 1]
    ys = _moe_routed(tile_expert, tile_first, n_active, xs, w1p.reshape(N_EXPERTS, d, D_FF), b1g, b1l,
                     w2p.reshape(N_EXPERTS, D_FF, d // 2), b_e2[0][:, None, :])

    gather_idx = pos2.reshape(TOP_K, 2, s).transpose(1, 0, 2)
    sc_rows = s // COMBINE_CHUNKS
    out = None
    for ci in range(COMBINE_CHUNKS):
        idx = gather_idx[:, :, ci * sc_rows:(ci + 1) * sc_rows].reshape(1, -1)
        yg = _sc_gather(ys.reshape(2 * n_rows, PACK_W), idx).reshape(2, TOP_K, sc_rows, PACK_W)
        out = _final(x1, yg, w4, gt_f, g_final.reshape(1, d), tm, ci * (sc_rows // tm), out)
    return out.reshape(b, s, d)
```

```python
import dataclasses
import functools
import math

import numpy as np
import jax
import jax.numpy as jnp
from jax import lax
from jax.experimental import pallas as pl
from jax.experimental.pallas import tpu as pltpu
from jax.experimental.pallas import tpu_sc as plsc

D_MODEL = 1024
HEAD_DIM = 64
SWA_HEADS = 8
SWA_KV_HEADS = 2
WINDOW = 128
FOX_HEADS = 8
BLOCK = 128
REL_BUCKETS = 32
REL_MAX_DIST = WINDOW
N_EXPERTS = 32
TOP_K = 4
D_FF = D_MODEL
SWIGLU_LIMIT = 7.0
SWIGLU_ALPHA = 1.702
RMS_EPS = 1e-5
N_MOD = 6

SWA_Q = SWA_HEADS * HEAD_DIM
SWA_KV = SWA_KV_HEADS * HEAD_DIM
FOX_W = FOX_HEADS * HEAD_DIM
LANES = 128
N_PAIRS = FOX_HEADS // 2
NEG_BIG = -1e30
LOG2E = math.log2(math.e)
FOX_TILE = 256
FOX_QSUB = 2
FOX_CHUNK = 4
N_SPLIT = 3
SKIP_LOG2 = 127.0
NORM_SLACK = 1.01
PACK_W = 256
MOE_TILE = 512
SC_WINDOW = 128
RANK_RADIX = 128
FF_SLAB = 256
COMBINE_CHUNKS = 2
SC_LANES = 16
SC_PACK_ROWS = 8
VMEM_LIMIT = 56 * 1024 * 1024

F32 = jnp.float32
BF16 = jnp.bfloat16
HIGHEST = lax.Precision.HIGHEST


def _dot(a, b):
    return jnp.dot(a, b, preferred_element_type=F32)


def _dot_nt(a, b, precision=None):
    return lax.dot_general(a, b, (((1,), (1,)), ((), ())), preferred_element_type=F32, precision=precision)


def _const_spec(shape):
    nd = len(shape)
    return pl.BlockSpec(shape, lambda *_: (0,) * nd)


def _params(sem):
    return pltpu.CompilerParams(dimension_semantics=sem, vmem_limit_bytes=VMEM_LIMIT)


def _ada_kernel(c_ref, w_ref, b_ref, o_ref):
    c = c_ref[...]
    act = c * jax.nn.sigmoid(c)
    o_ref[...] = jnp.sum(act * w_ref[...], axis=0, keepdims=True) + b_ref[...]


def _ada(c_col, w_ada, b_ada):
    n = w_ada.shape[1]
    tn = 1024
    return pl.pallas_call(
        _ada_kernel,
        grid=(n // tn,),
        in_specs=[_const_spec((D_MODEL, 1)),
                  pl.BlockSpec((D_MODEL, tn), lambda j: (0, j)),
                  pl.BlockSpec((1, tn), lambda j: (0, j))],
        out_specs=pl.BlockSpec((1, tn), lambda j: (0, j)),
        out_shape=jax.ShapeDtypeStruct((1, n), F32),
        compiler_params=_params(("arbitrary",)),
        name="ada_mod",
    )(c_col, w_ada, b_ada)


def _split_bf16(v):
    parts = []
    for _ in range(N_SPLIT):
        p = v.astype(BF16)
        v = v - p.astype(F32)
        parts.append(p)
    return parts


def _inproj_kernel(x_ref, g_ref, sh_ref, sc_ref, wa_ref, wb_ref, wvt_ref, wg_ref, bf_ref, hind_ref, place_ref,
                   qa_ref, kva_ref, vat_ref, qb_ref, kb_ref, vbt_ref, sga_ref, sgb_ref, ca_ref, qn_ref, kn_ref,
                   cf_ref, cl_ref, carry_ref):
    i = pl.program_id(0)
    tm = x_ref.shape[0]
    t = FOX_TILE
    nsub = tm // t

    @pl.when(i == 0)
    def _():
        carry_ref[...] = jnp.zeros_like(carry_ref)

    xf = x_ref[...]
    ms = jnp.mean(xf * xf, axis=-1, keepdims=True)
    y = xf * lax.rsqrt(ms + RMS_EPS) * g_ref[...]
    u = y * (1.0 + sc_ref[...]) + sh_ref[...]
    ub = u.astype(BF16)

    za = _dot(ub, wa_ref[...])
    qa_ref[...] = (za[:, :SWA_Q] * (HEAD_DIM ** -0.5 * LOG2E)).astype(BF16)
    kva_ref[...] = za[:, SWA_Q:].astype(BF16)

    zb = _dot(ub, wb_ref[...])
    qb = (zb[:, :FOX_W] * (HEAD_DIM ** -0.5 * LOG2E)).astype(BF16)
    kb = zb[:, FOX_W:2 * FOX_W].astype(BF16)
    qb_ref[...] = qb
    kb_ref[...] = kb
    vt = _dot_nt(wvt_ref[...], ub).astype(BF16)
    vbt_ref[...] = vt[:FOX_W]
    vat_ref[...] = vt[FOX_W:]

    def tile_norm_max(z, o_ref):
        zf = z.astype(F32)
        n2 = _dot((zf * zf).astype(BF16), hind_ref[...])
        for sb in range(nsub):
            o_ref[sb] = jnp.max(n2[sb * t:(sb + 1) * t], axis=0, keepdims=True)

    tile_norm_max(qb, qn_ref)
    tile_norm_max(kb, kn_ref)

    zg = _dot(ub, wg_ref[...])
    sg = jax.nn.sigmoid(zg)
    sga_ref[...] = sg[:, :D_MODEL].astype(BF16)
    sgb_ref[...] = sg[:, D_MODEL:].astype(BF16)

    fb = zb[:, 2 * FOX_W:2 * FOX_W + FOX_HEADS] + bf_ref[...]
    lf = jnp.minimum(fb, 0.0) - jnp.log1p(jnp.exp(-jnp.abs(fb)))
    r = lax.broadcasted_iota(jnp.int32, (t, t), 0)
    cc = lax.broadcasted_iota(jnp.int32, (t, t), 1)
    lower = (cc <= r).astype(BF16)
    carry = carry_ref[...]
    for sb in range(nsub):
        rows = slice(sb * t, (sb + 1) * t)
        local = sum(_dot(lower, p) for p in _split_bf16(lf[rows]))
        cf_ref[sb] = (local[0:1, :] + carry) * LOG2E
        carry = carry + local[t - 1:t, :]
        cl_ref[sb] = carry * LOG2E
        aug = sum(_dot(p, place_ref[k]) for k, p in enumerate(_split_bf16(local * LOG2E)))
        ca_ref[rows, :] = aug.astype(BF16)
    carry_ref[...] = carry


def _inproj(x2, g, sh, sc, wa, wb, wvt, wg, bfor, tm):
    s = x2.shape[0]
    nsub = tm // FOX_TILE
    row = lambda n: pl.BlockSpec((tm, n), lambda i: (i, 0))
    hind = np.zeros((FOX_W, FOX_HEADS), np.float32)
    hind[np.arange(FOX_W), np.arange(FOX_W) // HEAD_DIM] = 1.0
    place = np.zeros((N_SPLIT, FOX_HEADS, N_PAIRS * LANES), np.float32)
    for k in range(N_SPLIT):
        for h in range(FOX_HEADS):
            place[k, h, (h // 2) * LANES + N_SPLIT * (h % 2) + k] = 1.0
    hind = jnp.asarray(hind, BF16)
    place = jnp.asarray(place, BF16)
    out_shape = [
        jax.ShapeDtypeStruct((s, SWA_Q), BF16),
        jax.ShapeDtypeStruct((s, 2 * LANES), BF16),
        jax.ShapeDtypeStruct((SWA_KV, s), BF16),
        jax.ShapeDtypeStruct((s, FOX_W), BF16),
        jax.ShapeDtypeStruct((s, FOX_W), BF16),
        jax.ShapeDtypeStruct((FOX_W, s), BF16),
        jax.ShapeDtypeStruct((s, D_MODEL), BF16),
        jax.ShapeDtypeStruct((s, D_MODEL), BF16),
        jax.ShapeDtypeStruct((s, N_PAIRS * LANES), BF16),
        jax.ShapeDtypeStruct((s // FOX_TILE, 1, FOX_HEADS), F32),
        jax.ShapeDtypeStruct((s // FOX_TILE, 1, FOX_HEADS), F32),
        jax.ShapeDtypeStruct((s // FOX_TILE, 1, FOX_HEADS), F32),
        jax.ShapeDtypeStruct((s // FOX_TILE, 1, FOX_HEADS), F32),
    ]
    stat = pl.BlockSpec((nsub, 1, FOX_HEADS), lambda i: (i, 0, 0))
    out_specs = [row(SWA_Q), row(2 * LANES), pl.BlockSpec((SWA_KV, tm), lambda i: (0, i)),
                 row(FOX_W), row(FOX_W), pl.BlockSpec((FOX_W, tm), lambda i: (0, i)),
                 row(D_MODEL), row(D_MODEL), row(N_PAIRS * LANES), stat, stat, stat, stat]
    consts = [g, sh, sc, wa, wb, wvt, wg, bfor, hind, place]
    return pl.pallas_call(
        _inproj_kernel,
        grid=(s // tm,),
        in_specs=[row(D_MODEL)] + [_const_spec(a.shape) for a in consts],
        out_specs=out_specs,
        out_shape=out_shape,
        scratch_shapes=[pltpu.VMEM((1, FOX_HEADS), F32)],
        compiler_params=_params(("arbitrary",)),
        name="in_proj",
    )(x2, *consts)


def _t5_buckets_np():
    qi = np.arange(BLOCK)[:, None]
    kj = np.arange(2 * BLOCK)[None, :]
    dist = BLOCK + qi - kj
    n = np.maximum(dist, 0)
    max_exact = REL_BUCKETS // 2
    nf = np.maximum(n, 1).astype(np.float32)
    large = max_exact + (np.log(nf / np.float32(max_exact)) / np.float32(math.log(REL_MAX_DIST / max_exact))
                         * np.float32(REL_BUCKETS - max_exact)).astype(np.int32)
    large = np.minimum(large, REL_BUCKETS - 1)
    bucket = np.where(n < max_exact, n, large).astype(np.int32)
    band = (dist >= 0) & (dist < WINDOW)
    return np.where(band, bucket, -1).astype(np.int32)


def _swa_kernel(rel_ref, sink_ref, bkt_ref, q_ref, kc_ref, kp_ref, vc_ref, vp_ref, o_ref, bias_ref):
    n = pl.program_id(0)

    @pl.when(n == 0)
    def _():
        bkt = bkt_ref[...]
        prev = lax.broadcasted_iota(jnp.int32, bkt.shape, 0) < BLOCK
        for h in range(SWA_HEADS):
            acc = jnp.full(bkt.shape, NEG_BIG, F32)
            for b in range(REL_BUCKETS):
                acc = jnp.where(bkt == b, rel_ref[b * SWA_HEADS + h] * LOG2E, acc)
            cols = slice((h % 2) * BLOCK, (h % 2 + 1) * BLOCK)
            bias_ref[0, h // 2, :, cols] = acc
            bias_ref[1, h // 2, :, cols] = jnp.where(prev, NEG_BIG, acc)

    first = jnp.where(n == 0, 1, 0)
    lane = lax.broadcasted_iota(jnp.int32, (BLOCK, LANES), 1)
    col2 = lax.broadcasted_iota(jnp.int32, (1, 2 * BLOCK), 1)
    k2 = jnp.concatenate([kp_ref[...], kc_ref[...]], axis=0)
    v2 = jnp.concatenate([vp_ref[...], vc_ref[...]], axis=1)
    pairs = range(SWA_HEADS // 2)
    scores = []
    for p in pairs:
        qp = q_ref[:, p * LANES:(p + 1) * LANES]
        zero = jnp.zeros_like(qp)
        qs = jnp.concatenate([jnp.where(lane < HEAD_DIM, qp, zero), jnp.where(lane >= HEAD_DIM, qp, zero)], axis=0)
        g = p // 2
        scores.append(_dot_nt(k2[:, g * LANES:(g + 1) * LANES], qs))
    weights = []
    for p, s in zip(pairs, scores):
        s = s + bias_ref[first, p]
        sink = jnp.where(col2 < BLOCK, sink_ref[2 * p], sink_ref[2 * p + 1]) * LOG2E
        m = jnp.maximum(jnp.max(s, axis=0, keepdims=True), sink)
        e = jnp.exp2(s - m)
        denom = jnp.sum(e, axis=0, keepdims=True) + jnp.exp2(sink - m)
        weights.append((e.astype(BF16), denom))
    outs = []
    for p, (e, denom) in zip(pairs, weights):
        g = p // 2
        o = _dot(v2[g * HEAD_DIM:(g + 1) * HEAD_DIM], e) / denom
        outs += [o[:, :BLOCK], o[:, BLOCK:]]
    o_ref[...] = jnp.concatenate(outs, axis=0).T.astype(BF16)


def _swa(rel_flat, sinks, bkt_t, qa, ka2, vat):
    s = qa.shape[0]
    nb = s // BLOCK
    smem = pl.BlockSpec(memory_space=pltpu.SMEM)
    prev = lambda n: jnp.maximum(n - 1, 0)
    return pl.pallas_call(
        _swa_kernel,
        grid=(nb,),
        in_specs=[smem, smem, _const_spec(bkt_t.shape),
                  pl.BlockSpec((BLOCK, SWA_Q), lambda n: (n, 0)),
                  pl.BlockSpec((BLOCK, 2 * LANES), lambda n: (n, 0)),
                  pl.BlockSpec((BLOCK, 2 * LANES), lambda n: (prev(n), 0)),
                  pl.BlockSpec((SWA_KV, BLOCK), lambda n: (0, n)),
                  pl.BlockSpec((SWA_KV, BLOCK), lambda n: (0, prev(n)))],
        out_specs=pl.BlockSpec((BLOCK, SWA_Q), lambda n: (n, 0)),
        out_shape=jax.ShapeDtypeStruct((s, SWA_Q), BF16),
        scratch_shapes=[pltpu.VMEM((2, SWA_HEADS // 2, 2 * BLOCK, 2 * BLOCK), F32)],
        compiler_params=_params(("arbitrary",)),
        name="swa_attn",
    )(rel_flat, sinks, bkt_t, qa, ka2, ka2, vat, vat)


def _fox_kernel(jlo_ref, base_ref, q_ref, k_ref, ca_ref, vt_ref, o_ref):
    p = pl.program_id(0)
    i = pl.program_id(1)
    t = FOX_TILE
    tq = FOX_QSUB * t
    first_diag = i * FOX_QSUB
    lane = lax.broadcasted_iota(jnp.int32, (tq, LANES), 1)
    q = q_ref[...]
    wq = []
    for hh in range(2):
        in_head = (lane >= hh * HEAD_DIM) & (lane < (hh + 1) * HEAD_DIM)
        qm = jnp.where(in_head, q, jnp.zeros_like(q))
        sel = (lane >= N_SPLIT * hh) & (lane < N_SPLIT * (hh + 1))
        aug = jnp.where(sel, -1.0, 0.0).astype(BF16)
        wq.append(jnp.concatenate([qm, aug], axis=1))
    key = lax.broadcasted_iota(jnp.int32, (t, tq), 0)
    qry = lax.broadcasted_iota(jnp.int32, (t, tq), 1)
    causal = [key + d * t <= qry for d in range(FOX_QSUB)]

    def step(j, carry, heads, nsub, diagonal=False):
        start = pl.multiple_of(j * t, t)
        rows = nsub * t
        lhs = jnp.concatenate([k_ref[pl.ds(start, rows), :], ca_ref[pl.ds(start, rows), :]], axis=1)
        scores = [_dot_nt(lhs, wq[hh]) for hh in heads]
        mid = []
        for hh, s, (m, l, acc) in zip(heads, scores, carry):
            h = 2 * p + hh
            parts = [s[k * t:(k + 1) * t] for k in range(nsub)]
            if diagonal:
                for d in range(FOX_QSUB):
                    k = nsub - FOX_QSUB + d
                    parts[k] = jnp.where(causal[d], parts[k], NEG_BIG)
            djs = [base_ref[h, first_diag] - base_ref[h, j + k] for k in range(nsub)]
            m_new = m
            for part, dj in zip(parts, djs):
                m_new = jnp.maximum(m_new, jnp.max(part, axis=0, keepdims=True) + dj)
            alpha = jnp.exp2(m - m_new)
            es = [jnp.exp2(part + (dj - m_new)) for part, dj in zip(parts, djs)]
            l = alpha * l
            for e in es:
                l = l + jnp.sum(e, axis=0, keepdims=True)
            e_all = es[0] if nsub == 1 else jnp.concatenate(es, axis=0)
            mid.append((m_new, l, alpha, acc, e_all.astype(BF16)))
        new = []
        for hh, (m_new, l, alpha, acc, e_all) in zip(heads, mid):
            vt = vt_ref[hh * HEAD_DIM:(hh + 1) * HEAD_DIM, pl.ds(start, rows)]
            new.append((m_new, l, alpha * acc + _dot(vt, e_all)))
        return tuple(new)

    def run_alone(lo, hi, carry, heads):
        n_full = (hi - lo) // FOX_CHUNK
        carry = lax.fori_loop(0, n_full, lambda n, c: step(lo + FOX_CHUNK * n, c, heads, FOX_CHUNK), carry)
        rest = lo + FOX_CHUNK * n_full
        tails = [lambda c: c] + [functools.partial(lambda c, k: step(rest, c, heads, k), k=k)
                                 for k in range(1, FOX_CHUNK)]
        return lax.switch(hi - rest, tails, carry)

    def run_to_diagonal(lo, carry, heads):
        end = first_diag + FOX_QSUB
        count = end - lo
        last = (count - FOX_QSUB) % FOX_CHUNK + FOX_QSUB
        n_full = (count - last) // FOX_CHUNK
        carry = lax.fori_loop(0, n_full, lambda n, c: step(lo + FOX_CHUNK * n, c, heads, FOX_CHUNK), carry)
        tails = [functools.partial(lambda c, k: step(end - k, c, heads, k, diagonal=True), k=k)
                 for k in range(FOX_QSUB, FOX_QSUB + FOX_CHUNK)]
        return lax.switch(last - FOX_QSUB, tails, carry)

    lo0 = jlo_ref[2 * p, i]
    lo1 = jlo_ref[2 * p + 1, i]
    lo_both = jnp.maximum(lo0, lo1)
    init = (jnp.full((1, tq), NEG_BIG, F32), jnp.zeros((1, tq), F32), jnp.zeros((HEAD_DIM, tq), F32))
    (c0,) = run_alone(lo0, lo_both, (init,), (0,))
    (c1,) = run_alone(lo1, lo_both, (init,), (1,))
    carry = run_to_diagonal(lo_both, (c0, c1), (0, 1))
    ot = jnp.concatenate([carry[0][2] / carry[0][1], carry[1][2] / carry[1][1]], axis=0)
    o_ref[...] = ot.T.astype(BF16)


def _fox_schedule(c_first, c_last, qn2, kn2):
    nt = qn2.shape[0]
    t = FOX_TILE
    bq = jnp.sqrt(qn2.reshape(nt, FOX_HEADS)) * NORM_SLACK
    bk = jnp.sqrt(kn2.reshape(nt, FOX_HEADS)) * NORM_SLACK
    c_first = c_first.reshape(nt, FOX_HEADS)
    c_last = c_last.reshape(nt, FOX_HEADS)
    upper = (bq[:, None, :] * bk[None, :, :] + (bq * bk)[:, None, :] + c_first[:, None, :] - c_last[None, :, :])
    ii = jnp.arange(nt)[:, None, None]
    jj = jnp.arange(nt)[None, :, None]
    needed = (jj <= ii) & ((upper >= -SKIP_LOG2) | (jj == ii))
    jlo = jnp.min(jnp.where(needed, jj, nt), axis=1)
    jlo = jnp.min(jlo.reshape(nt // FOX_QSUB, FOX_QSUB, FOX_HEADS), axis=1).T.astype(jnp.int32)
    base = jnp.concatenate([jnp.zeros((1, FOX_HEADS), F32), c_last[:-1]], axis=0).T
    return jlo, base


def _fox(jlo, base, qb, kb, ca, vbt):
    s = qb.shape[0]
    t = FOX_QSUB * FOX_TILE
    grid_spec = pltpu.PrefetchScalarGridSpec(
        num_scalar_prefetch=2,
        grid=(N_PAIRS, s // t),
        in_specs=[pl.BlockSpec((t, LANES), lambda p, i, *_: (i, p)),
                  pl.BlockSpec((s, LANES), lambda p, i, *_: (0, p)),
                  pl.BlockSpec((s, LANES), lambda p, i, *_: (0, p)),
                  pl.BlockSpec((LANES, s), lambda p, i, *_: (p, 0))],
        out_specs=pl.BlockSpec((t, LANES), lambda p, i, *_: (i, p)),
    )
    return pl.pallas_call(
        _fox_kernel,
        grid_spec=grid_spec,
        out_shape=jax.ShapeDtypeStruct((s, FOX_W), BF16),
        compiler_params=_params(("arbitrary", "arbitrary")),
        name="fox_attn",
    )(jlo, base, qb, kb, ca, vbt)


def _rms(x, g):
    return x * lax.rsqrt(jnp.mean(x * x, axis=-1, keepdims=True) + RMS_EPS) * g


def _pack_rows(v):
    halves = []
    for j in range(2):
        lo = v[:, (2 * j) * PACK_W:(2 * j + 1) * PACK_W].astype(BF16).astype(F32)
        hi = v[:, (2 * j + 1) * PACK_W:(2 * j + 2) * PACK_W].astype(BF16).astype(F32)
        lo_bits = lax.bitcast_convert_type(lo, jnp.uint32)
        hi_bits = lax.bitcast_convert_type(hi, jnp.uint32)
        halves.append(hi_bits | (lo_bits >> 16))
    return halves


def _unpack_words(w):
    lo = lax.bitcast_convert_type(w << 16, F32)
    hi = lax.bitcast_convert_type(w & jnp.uint32(0xFFFF0000), F32)
    return lo, hi


def _outproj_kernel(x_ref, ya_ref, yb_ref, sga_ref, sgb_ref, pa_ref, pb_ref, wo_ref, gt_ref, g_ref, sh_ref, sc_ref,
                    wr_ref, br_ref, x1_ref, u2p_ref, e4_ref, r4_ref, w4_ref, cnt_ref, carry_ref):
    i = pl.program_id(0)
    tm = x_ref.shape[0]

    @pl.when(i == 0)
    def _():
        carry_ref[...] = jnp.zeros_like(carry_ref)

    merged = (sga_ref[...].astype(F32) * _dot(ya_ref[...], pa_ref[...])
              + sgb_ref[...].astype(F32) * _dot(yb_ref[...], pb_ref[...]))
    x1 = x_ref[...] + gt_ref[...] * _dot(merged.astype(BF16), wo_ref[...])
    x1_ref[...] = x1
    u2 = _rms(x1, g_ref[...]) * (1.0 + sc_ref[...]) + sh_ref[...]
    halves = _pack_rows(u2)
    u2p_ref[0] = halves[0]
    u2p_ref[1] = halves[1]

    logits = _dot(u2.astype(BF16), wr_ref[...]) + br_ref[...]
    eidx = lax.broadcasted_iota(jnp.int32, logits.shape, 1)
    work = logits
    sel = jnp.zeros(logits.shape, jnp.bool_)
    picks, vals = [], []
    for k in range(TOP_K):
        m = jnp.max(work, axis=-1, keepdims=True)
        first = jnp.min(jnp.where(work == m, eidx, N_EXPERTS), axis=-1, keepdims=True)
        hit = eidx == first
        sel = sel | hit
        work = jnp.where(hit, -jnp.inf, work)
        picks.append(first)
        vals.append(m)
    exps = [jnp.exp(v - vals[0]) for v in vals]
    denom = exps[0] + exps[1] + exps[2] + exps[3]

    r = lax.broadcasted_iota(jnp.int32, (tm, tm), 0)
    cc = lax.broadcasted_iota(jnp.int32, (tm, tm), 1)
    before = (cc < r).astype(BF16)
    chosen = sel.astype(BF16)
    rank = _dot(before, chosen) + carry_ref[...]
    cnt = carry_ref[...] + jnp.sum(chosen.astype(F32), axis=0, keepdims=True)
    carry_ref[...] = cnt
    cnt_ref[...] = cnt
    slot = lax.broadcasted_iota(jnp.int32, (tm, 4 * TOP_K), 1)
    cols = jnp.zeros((tm, 4 * TOP_K), F32)
    for k in range(TOP_K):
        rk = jnp.sum(jnp.where(eidx == picks[k], rank, 0.0), axis=-1, keepdims=True)
        hi = jnp.floor(rk * (1.0 / RANK_RADIX))
        for j, col in ((k, picks[k].astype(F32)), (TOP_K + k, hi), (2 * TOP_K + k, rk - hi * RANK_RADIX)):
            cols = jnp.where(slot == j, col, cols)
        w4_ref[:, k:k + 1] = exps[k] / denom
    eye = (lax.broadcasted_iota(jnp.int32, (4 * TOP_K, 4 * TOP_K), 0)
           == lax.broadcasted_iota(jnp.int32, (4 * TOP_K, 4 * TOP_K), 1)).astype(BF16)
    flipped = _dot_nt(eye, cols.astype(BF16))
    e4_ref[...] = flipped[:TOP_K].astype(jnp.int32)
    r4_ref[...] = (flipped[TOP_K:2 * TOP_K] * RANK_RADIX + flipped[2 * TOP_K:3 * TOP_K]).astype(jnp.int32)


def _outproj(x2, ya, yb, sga, sgb, pa, pb, wo, gt, g, sh, sc, wr, br, tm):
    s = x2.shape[0]
    row = lambda n: pl.BlockSpec((tm, n), lambda i: (i, 0))
    consts = [pa, pb, wo, gt, g, sh, sc, wr, br]
    return pl.pallas_call(
        _outproj_kernel,
        grid=(s // tm,),
        in_specs=[row(D_MODEL), row(SWA_Q), row(FOX_W), row(D_MODEL), row(D_MODEL)] + [_const_spec(a.shape) for a in consts],
        out_specs=[row(D_MODEL), pl.BlockSpec((2, tm, PACK_W), lambda i: (0, i, 0)),
                   pl.BlockSpec((TOP_K, tm), lambda i: (0, i)), pl.BlockSpec((TOP_K, tm), lambda i: (0, i)), row(TOP_K),
                   _const_spec((1, N_EXPERTS))],
        out_shape=[jax.ShapeDtypeStruct((s, D_MODEL), F32),
                   jax.ShapeDtypeStruct((2, s, PACK_W), jnp.uint32),
                   jax.ShapeDtypeStruct((TOP_K, s), jnp.int32),
                   jax.ShapeDtypeStruct((TOP_K, s), jnp.int32),
                   jax.ShapeDtypeStruct((s, TOP_K), F32),
                   jax.ShapeDtypeStruct((1, N_EXPERTS), F32)],
        scratch_shapes=[pltpu.VMEM((1, N_EXPERTS), F32)],
        compiler_params=_params(("arbitrary",)),
        name="out_proj_router",
    )(x2, ya, yb, sga, sgb, *consts)


def _sc_mesh():
    return plsc.VectorSubcoreMesh(core_axis_name="core", subcore_axis_name="subcore")


def _sc_dispatch(rows, idx, n_out):
    n, width = rows.shape

    @functools.partial(pl.kernel, out_type=jax.ShapeDtypeStruct((n_out, width), rows.dtype), mesh=_sc_mesh(),
                       scratch_types=[])
    def dispatch(x_hbm, i_hbm, o_hbm):
        def body(x_vmem, i_vmem):
            for k in range(TOP_K):
                pltpu.sync_copy(x_vmem, o_hbm.at[i_vmem.at[k]])

        pltpu.emit_pipeline(
            body, grid=(n // SC_WINDOW,),
            in_specs=[pl.BlockSpec((SC_WINDOW, width), lambda i: (i, 0)),
                      pl.BlockSpec((TOP_K, SC_WINDOW), lambda i: (0, i))],
            out_specs=[], core_axis_name=("core", "subcore"), dimension_semantics=(pltpu.PARALLEL,),
        )(x_hbm, i_hbm)

    return dispatch(rows, idx)


def _sc_gather(table, idx):
    n = idx.shape[1]
    width = table.shape[1]

    @functools.partial(pl.kernel, out_type=jax.ShapeDtypeStruct((n, width), table.dtype), mesh=_sc_mesh(),
                       scratch_types=[])
    def gather(t_hbm, i_hbm, o_hbm):
        def body(i_vmem, o_vmem):
            pltpu.sync_copy(t_hbm.at[i_vmem.at[0]], o_vmem)

        pltpu.emit_pipeline(
            body, grid=(n // SC_WINDOW,),
            in_specs=[pl.BlockSpec((1, SC_WINDOW), lambda i: (0, i))],
            out_specs=[pl.BlockSpec((SC_WINDOW, width), lambda i: (i, 0))],
            core_axis_name=("core", "subcore"), dimension_semantics=(pltpu.PARALLEL,),
        )(i_hbm, o_hbm)

    return gather(table, idx)


def _sc_pack_weights(w1, w2):
    r1, r2 = w1.shape[0], w2.shape[0]
    half = D_FF // 2

    def bf16_bits(v):
        u = plsc.bitcast(v, jnp.uint32)
        return (u + jnp.uint32(0x7FFF) + ((u >> 16) & jnp.uint32(1))) >> 16

    cp = pltpu.CompilerParams()
    if "needs_layout_passes" in pltpu.CompilerParams.__dataclass_fields__:
        cp = dataclasses.replace(cp, needs_layout_passes=False)

    n_in = w1.size + w2.size
    cost = pl.CostEstimate(flops=8 * n_in, transcendentals=0, bytes_accessed=6 * n_in)
    @functools.partial(
        pl.kernel, mesh=_sc_mesh(), scratch_types=[], compiler_params=cp, cost_estimate=cost,
        out_type=(jax.ShapeDtypeStruct((r1, D_FF), jnp.uint32), jax.ShapeDtypeStruct((r2, D_MODEL // 2), jnp.uint32)))
    def pack(w1_hbm, w2_hbm, o1_hbm, o2_hbm):
        lanes = lax.iota(jnp.int32, SC_LANES)

        def body1(x_vmem, o_vmem):
            @pl.loop(0, SC_PACK_ROWS)
            def _(r):
                rr = jnp.full((SC_LANES,), r, jnp.int32)

                @pl.loop(0, half, step=SC_LANES)
                def _(c):
                    col = 2 * (c + lanes)
                    g0 = plsc.load_gather(x_vmem, [rr, col])
                    g1 = plsc.load_gather(x_vmem, [rr, col + 2 * half])
                    l0 = plsc.load_gather(x_vmem, [rr, col + 1])
                    l1 = plsc.load_gather(x_vmem, [rr, col + 2 * half + 1])
                    o_vmem[r, pl.ds(c, SC_LANES)] = bf16_bits(g0) | (bf16_bits(g1) << 16)
                    o_vmem[r, pl.ds(half + c, SC_LANES)] = bf16_bits(l0) | (bf16_bits(l1) << 16)

        def body2(x_vmem, o_vmem):
            @pl.loop(0, SC_PACK_ROWS)
            def _(r):
                @pl.loop(0, D_MODEL // 2, step=SC_LANES)
                def _(c):
                    lo = x_vmem[r, pl.ds(c, SC_LANES)]
                    hi = x_vmem[r, pl.ds(D_MODEL // 2 + c, SC_LANES)]
                    o_vmem[r, pl.ds(c, SC_LANES)] = bf16_bits(lo) | (bf16_bits(hi) << 16)

        for body, x_hbm, o_hbm, rows in ((body1, w1_hbm, o1_hbm, r1), (body2, w2_hbm, o2_hbm, r2)):
            pltpu.emit_pipeline(
                body, grid=(rows // SC_PACK_ROWS,),
                in_specs=[pl.BlockSpec((SC_PACK_ROWS, x_hbm.shape[1]), lambda i: (i, 0))],
                out_specs=[pl.BlockSpec((SC_PACK_ROWS, o_hbm.shape[1]), lambda i: (i, 0))],
                core_axis_name=("core", "subcore"), dimension_semantics=(pltpu.PARALLEL,),
            )(x_hbm, o_hbm)

    return pack(w1, w2)


def _moe_kernel(te_ref, first_ref, nact_ref, xs_ref, w1_ref, b1g_ref, b1l_ref, w2_ref, b2_ref, y_ref,
                wg_ref, wl_ref, w2b_ref):
    i = pl.program_id(0)
    live = i < nact_ref[0]

    @pl.when(jnp.logical_and(live, first_ref[i] == 1))
    def _():
        half = D_FF // 2
        for dst, words in ((wg_ref, w1_ref[0, :, :half]), (wl_ref, w1_ref[0, :, half:]), (w2b_ref, w2_ref[0])):
            lo, hi = _unpack_words(words)
            dst[:, :half] = lo.astype(BF16)
            dst[:, half:] = hi.astype(BF16)

    @pl.when(live)
    def _():
        chunks = []
        for j in range(2):
            lo, hi = _unpack_words(xs_ref[j])
            chunks += [lo.astype(BF16), hi.astype(BF16)]
        x = jnp.concatenate(chunks, axis=1)
        y = b2_ref[0]
        for n in range(D_FF // FF_SLAB):
            units = slice(n * FF_SLAB, (n + 1) * FF_SLAB)
            hg = _dot(x, wg_ref[:, units]) + b1g_ref[0, :, units]
            hl = _dot(x, wl_ref[:, units]) + b1l_ref[0, :, units]
            glu = jnp.minimum(hg, SWIGLU_LIMIT)
            lin = jnp.clip(hl, -SWIGLU_LIMIT, SWIGLU_LIMIT)
            a = glu * jax.nn.sigmoid(SWIGLU_ALPHA * glu) * (lin + 1.0)
            y = y + _dot(a.astype(BF16), w2b_ref[units, :])
        halves = _pack_rows(y)
        y_ref[0] = halves[0]
        y_ref[1] = halves[1]


def _moe_routed(tile_expert, tile_first, n_active, xs, w1, b1g, b1l, w2, b2):
    n_rows = xs.shape[1]
    n_tiles = n_rows // MOE_TILE
    live = lambda i, na: jnp.minimum(i, na[0] - 1)
    rows_spec = pl.BlockSpec((2, MOE_TILE, PACK_W), lambda i, te, tf, na: (0, live(i, na), 0))
    wsp = lambda a: pl.BlockSpec((1,) + a.shape[1:], lambda i, te, tf, na: (te[live(i, na)], 0, 0))
    grid_spec = pltpu.PrefetchScalarGridSpec(
        num_scalar_prefetch=3,
        grid=(n_tiles,),
        in_specs=[rows_spec, wsp(w1), wsp(b1g), wsp(b1l), wsp(w2), wsp(b2)],
        out_specs=rows_spec,
        scratch_shapes=[pltpu.VMEM((D_MODEL, D_FF), BF16), pltpu.VMEM((D_MODEL, D_FF), BF16),
                        pltpu.VMEM((D_FF, D_MODEL), BF16)],
    )
    return pl.pallas_call(
        _moe_kernel,
        grid_spec=grid_spec,
        out_shape=jax.ShapeDtypeStruct(xs.shape, jnp.uint32),
        compiler_params=_params(("arbitrary",)),
        name="moe_routed",
    )(tile_expert, tile_first, n_active, xs, w1, b1g, b1l, w2, b2)


def _final_kernel(x1_ref, yg_ref, w4_ref, gt_ref, gf_ref, *rest):
    o_ref = rest[-1]
    w4 = w4_ref[...]
    cols = []
    for j in range(2):
        lo_acc = hi_acc = None
        for k in range(TOP_K):
            lo, hi = _unpack_words(yg_ref[j, k])
            wk = w4[:, k:k + 1]
            lo_acc = wk * lo if lo_acc is None else lo_acc + wk * lo
            hi_acc = wk * hi if hi_acc is None else hi_acc + wk * hi
        cols += [lo_acc, hi_acc]
    moe = jnp.concatenate(cols, axis=1)
    x2 = x1_ref[...] + gt_ref[...] * moe
    o_ref[...] = _rms(x2, gf_ref[...])


def _final(x1, yg, w4, gt, gf, tm, first_tile, prev_out):
    s = x1.shape[0]
    n_tiles = yg.shape[2] // tm
    row = lambda n: pl.BlockSpec((tm, n), lambda i: (i + first_tile, 0))
    in_specs = [row(D_MODEL), pl.BlockSpec((2, TOP_K, tm, PACK_W), lambda i: (0, 0, i, 0)), row(TOP_K),
                _const_spec(gt.shape), _const_spec(gf.shape)]
    args = [x1, yg, w4, gt, gf]
    aliases = {}
    if prev_out is not None:
        in_specs.append(pl.BlockSpec(memory_space=pl.ANY))
        args.append(prev_out)
        aliases = {len(args) - 1: 0}
    return pl.pallas_call(
        _final_kernel,
        grid=(n_tiles,),
        in_specs=in_specs,
        out_specs=row(D_MODEL),
        out_shape=jax.ShapeDtypeStruct((s, D_MODEL), F32),
        input_output_aliases=aliases,
        compiler_params=_params(("arbitrary",)),
        name="combine_final_norm",
    )(*args)


def _routing_tables(e4, r4, counts, s):
    n_rows = TOP_K * s + N_EXPERTS * MOE_TILE
    cnt = counts.reshape(N_EXPERTS).astype(jnp.int32)
    padded = ((cnt + MOE_TILE - 1) // MOE_TILE) * MOE_TILE
    ends = jnp.cumsum(padded)
    starts = ends - padded
    pos = jnp.take(starts, e4, axis=0) + r4
    tile_start = jnp.arange(n_rows // MOE_TILE, dtype=jnp.int32) * MOE_TILE
    tile_expert = jnp.minimum(jnp.sum(tile_start[:, None] >= ends[None, :], axis=1), N_EXPERTS - 1).astype(jnp.int32)
    tile_first = jnp.concatenate([jnp.ones((1,), jnp.int32), (tile_expert[1:] != tile_expert[:-1]).astype(jnp.int32)])
    n_active = (ends[-1:] // MOE_TILE).astype(jnp.int32)
    return pos, tile_expert, tile_first, n_active, n_rows


def kernel(x, c, w_ada, b_ada, g_mix, w_in, b_forget, sinks, rel_bias, w_proj_a, w_proj_b, w_out, g_ffn,
           w_router, b_router, w_e1, b_e1, w_e2, b_e2, g_final):
    b, s, d = x.shape
    assert b == 1 and d == D_MODEL and w_ada.shape[0] == 1
    assert s <= 256 * RANK_RADIX
    x2 = x.reshape(s, d)
    tm = min(512, s)

    mod = _ada(c.reshape(d, 1), w_ada[0], b_ada)
    sh_m, sc_m, gt_m, sh_f, sc_f, gt_f = [mod[:, k * d:(k + 1) * d] for k in range(N_MOD)]

    w = w_in[0]
    o_ka, o_va, o_b = SWA_Q, SWA_Q + SWA_KV, SWA_Q + 2 * SWA_KV
    o_f = o_b + 3 * FOX_W
    o_g = o_f + FOX_HEADS
    dup = lambda m: jnp.concatenate([m[:, :HEAD_DIM], m[:, :HEAD_DIM], m[:, HEAD_DIM:], m[:, HEAD_DIM:]], axis=1)
    wa = jnp.concatenate([w[:, :SWA_Q], dup(w[:, o_ka:o_va])], axis=1).astype(BF16)
    pad = jnp.zeros((d, LANES - FOX_HEADS), F32)
    wb = jnp.concatenate([w[:, o_b:o_b + 2 * FOX_W], w[:, o_f:o_g], pad], axis=1).astype(BF16)
    wvt = jnp.concatenate([w[:, o_b + 2 * FOX_W:o_f], w[:, o_va:o_b]], axis=1).T.astype(BF16)
    wg = w[:, o_g:].astype(BF16)
    qa, ka2, vat, qb, kb, vbt, sga, sgb, ca, qn2, kn2, c_first, c_last = _inproj(
        x2, g_mix, sh_m, sc_m, wa, wb, wvt, wg, b_forget, tm)

    ya = _swa(rel_bias.reshape(-1), sinks[0], jnp.asarray(_t5_buckets_np().T), qa, ka2, vat)
    jlo, base = _fox_schedule(c_first, c_last, qn2, kn2)
    yb = _fox(jlo, base, qb, kb, ca, vbt)

    x1, u2p, e4, r4, w4, counts = _outproj(
        x2, ya, yb, sga, sgb, w_proj_a[0].astype(BF16), w_proj_b[0].astype(BF16), w_out[0].astype(BF16),
        gt_m, g_ffn, sh_f, sc_f, w_router[0].astype(BF16), b_router, tm)

    pos, tile_expert, tile_first, n_active, n_rows = _routing_tables(e4, r4, counts, s)
    pos2 = jnp.concatenate([pos, pos + n_rows], axis=1)
    w1p, w2p = _sc_pack_weights(w_e1[0].reshape(N_EXPERTS * d, 2 * D_FF), w_e2[0].reshape(N_EXPERTS * D_FF, d))
    pos2, w1p, w2p = lax.optimization_barrier((pos2, w1p, w2p))
    xs = _sc_dispatch(u2p.reshape(2 * s, PACK_W), pos2, 2 * n_rows).reshape(2, n_rows, PACK_W)

    b1 = b_e1[0].reshape(N_EXPERTS, D_FF, 2)
    b1g = b1[:, None, :, 0]
    b1l = b1[:, None, :, 1]
    ys = _moe_routed(tile_expert, tile_first, n_active, xs, w1p.reshape(N_EXPERTS, d, D_FF), b1g, b1l,
                     w2p.reshape(N_EXPERTS, D_FF, d // 2), b_e2[0][:, None, :])

    gather_idx = pos2.reshape(TOP_K, 2, s).transpose(1, 0, 2)
    sc_rows = s // COMBINE_CHUNKS
    out = None
    for ci in range(COMBINE_CHUNKS):
        idx = gather_idx[:, :, ci * sc_rows:(ci + 1) * sc_rows].reshape(1, -1)
        yg = _sc_gather(ys.reshape(2 * n_rows, PACK_W), idx).reshape(2, TOP_K, sc_rows, PACK_W)
        out = _final(x1, yg, w4, gt_f, g_final.reshape(1, d), tm, ci * (sc_rows // tm), out)
    return out.reshape(b, s, d)
```

```python
import dataclasses
import functools
import math

import numpy as np
import jax
import jax.numpy as jnp
from jax import lax
from jax.experimental import pallas as pl
from jax.experimental.pallas import tpu as pltpu
from jax.experimental.pallas import tpu_sc as plsc

D_MODEL = 1024
HEAD_DIM = 64
SWA_HEADS = 8
SWA_KV_HEADS = 2
WINDOW = 128
FOX_HEADS = 8
BLOCK = 128
REL_BUCKETS = 32
REL_MAX_DIST = WINDOW
N_EXPERTS = 32
TOP_K = 4
D_FF = D_MODEL
SWIGLU_LIMIT = 7.0
SWIGLU_ALPHA = 1.702
RMS_EPS = 1e-5
N_MOD = 6

SWA_Q = SWA_HEADS * HEAD_DIM
SWA_KV = SWA_KV_HEADS * HEAD_DIM
FOX_W = FOX_HEADS * HEAD_DIM
LANES = 128
N_PAIRS = FOX_HEADS // 2
NEG_BIG = -1e30
LOG2E = math.log2(math.e)
FOX_TILE = 256
SWA_BLOCKS = 4
FOX_QSUB = 2
FOX_CHUNK = 4
N_SPLIT = 3
SKIP_LOG2 = 127.0
NORM_SLACK = 1.01
PACK_W = 256
MOE_TILE = 512
SC_WINDOW = 128
RANK_RADIX = 128
FF_SLAB = 256
COMBINE_CHUNKS = 2
SC_LANES = 16
SC_PACK_ROWS = 8
VMEM_LIMIT = 56 * 1024 * 1024

F32 = jnp.float32
BF16 = jnp.bfloat16
HIGHEST = lax.Precision.HIGHEST


def _dot(a, b):
    return jnp.dot(a, b, preferred_element_type=F32)


def _dot_nt(a, b, precision=None):
    return lax.dot_general(a, b, (((1,), (1,)), ((), ())), preferred_element_type=F32, precision=precision)


def _const_spec(shape):
    nd = len(shape)
    return pl.BlockSpec(shape, lambda *_: (0,) * nd)


def _params(sem):
    return pltpu.CompilerParams(dimension_semantics=sem, vmem_limit_bytes=VMEM_LIMIT)


def _ada_kernel(c_ref, w_ref, b_ref, o_ref):
    c = c_ref[...]
    act = c * jax.nn.sigmoid(c)
    o_ref[...] = jnp.sum(act * w_ref[...], axis=0, keepdims=True) + b_ref[...]


def _ada(c_col, w_ada, b_ada):
    n = w_ada.shape[1]
    tn = 1024
    return pl.pallas_call(
        _ada_kernel,
        grid=(n // tn,),
        in_specs=[_const_spec((D_MODEL, 1)),
                  pl.BlockSpec((D_MODEL, tn), lambda j: (0, j)),
                  pl.BlockSpec((1, tn), lambda j: (0, j))],
        out_specs=pl.BlockSpec((1, tn), lambda j: (0, j)),
        out_shape=jax.ShapeDtypeStruct((1, n), F32),
        compiler_params=_params(("arbitrary",)),
        name="ada_mod",
    )(c_col, w_ada, b_ada)


def _split_bf16(v):
    parts = []
    for _ in range(N_SPLIT):
        p = v.astype(BF16)
        v = v - p.astype(F32)
        parts.append(p)
    return parts


def _inproj_kernel(x_ref, g_ref, sh_ref, sc_ref, wa_ref, wb_ref, wvt_ref, wg_ref, bf_ref, hind_ref, place_ref,
                   qa_ref, kva_ref, vat_ref, qb_ref, kb_ref, vbt_ref, sga_ref, sgb_ref, ca_ref, qn_ref, kn_ref,
                   cf_ref, cl_ref, carry_ref):
    i = pl.program_id(0)
    tm = x_ref.shape[0]
    t = FOX_TILE
    nsub = tm // t

    @pl.when(i == 0)
    def _():
        carry_ref[...] = jnp.zeros_like(carry_ref)

    xf = x_ref[...]
    ms = jnp.mean(xf * xf, axis=-1, keepdims=True)
    y = xf * lax.rsqrt(ms + RMS_EPS) * g_ref[...]
    u = y * (1.0 + sc_ref[...]) + sh_ref[...]
    ub = u.astype(BF16)

    za = _dot(ub, wa_ref[...])
    qa_ref[...] = (za[:, :SWA_Q] * (HEAD_DIM ** -0.5 * LOG2E)).astype(BF16)
    kva_ref[...] = za[:, SWA_Q:].astype(BF16)

    zb = _dot(ub, wb_ref[...])
    qb = (zb[:, :FOX_W] * (HEAD_DIM ** -0.5 * LOG2E)).astype(BF16)
    kb = zb[:, FOX_W:2 * FOX_W].astype(BF16)
    qb_ref[...] = qb
    kb_ref[...] = kb
    vt = _dot_nt(wvt_ref[...], ub).astype(BF16)
    vbt_ref[...] = vt[:FOX_W]
    vat_ref[...] = vt[FOX_W:]

    def tile_norm_max(z, o_ref):
        zf = z.astype(F32)
        n2 = _dot((zf * zf).astype(BF16), hind_ref[...])
        for sb in range(nsub):
            o_ref[sb] = jnp.max(n2[sb * t:(sb + 1) * t], axis=0, keepdims=True)

    tile_norm_max(qb, qn_ref)
    tile_norm_max(kb, kn_ref)

    zg = _dot(ub, wg_ref[...])
    sg = jax.nn.sigmoid(zg)
    sga_ref[...] = sg[:, :D_MODEL].astype(BF16)
    sgb_ref[...] = sg[:, D_MODEL:].astype(BF16)

    fb = zb[:, 2 * FOX_W:2 * FOX_W + FOX_HEADS] + bf_ref[...]
    lf = jnp.minimum(fb, 0.0) - jnp.log1p(jnp.exp(-jnp.abs(fb)))
    r = lax.broadcasted_iota(jnp.int32, (t, t), 0)
    cc = lax.broadcasted_iota(jnp.int32, (t, t), 1)
    lower = (cc <= r).astype(BF16)
    carry = carry_ref[...]
    for sb in range(nsub):
        rows = slice(sb * t, (sb + 1) * t)
        local = sum(_dot(lower, p) for p in _split_bf16(lf[rows]))
        cf_ref[sb] = (local[0:1, :] + carry) * LOG2E
        carry = carry + local[t - 1:t, :]
        cl_ref[sb] = carry * LOG2E
        aug = sum(_dot(p, place_ref[k]) for k, p in enumerate(_split_bf16(local * LOG2E)))
        ca_ref[rows, :] = aug.astype(BF16)
    carry_ref[...] = carry


def _inproj(x2, g, sh, sc, wa, wb, wvt, wg, bfor, tm):
    s = x2.shape[0]
    nsub = tm // FOX_TILE
    row = lambda n: pl.BlockSpec((tm, n), lambda i: (i, 0))
    hind = np.zeros((FOX_W, FOX_HEADS), np.float32)
    hind[np.arange(FOX_W), np.arange(FOX_W) // HEAD_DIM] = 1.0
    place = np.zeros((N_SPLIT, FOX_HEADS, N_PAIRS * LANES), np.float32)
    for k in range(N_SPLIT):
        for h in range(FOX_HEADS):
            place[k, h, (h // 2) * LANES + N_SPLIT * (h % 2) + k] = 1.0
    hind = jnp.asarray(hind, BF16)
    place = jnp.asarray(place, BF16)
    out_shape = [
        jax.ShapeDtypeStruct((s, SWA_Q), BF16),
        jax.ShapeDtypeStruct((s, 2 * LANES), BF16),
        jax.ShapeDtypeStruct((SWA_KV, s), BF16),
        jax.ShapeDtypeStruct((s, FOX_W), BF16),
        jax.ShapeDtypeStruct((s, FOX_W), BF16),
        jax.ShapeDtypeStruct((FOX_W, s), BF16),
        jax.ShapeDtypeStruct((s, D_MODEL), BF16),
        jax.ShapeDtypeStruct((s, D_MODEL), BF16),
        jax.ShapeDtypeStruct((s, N_PAIRS * LANES), BF16),
        jax.ShapeDtypeStruct((s // FOX_TILE, 1, FOX_HEADS), F32),
        jax.ShapeDtypeStruct((s // FOX_TILE, 1, FOX_HEADS), F32),
        jax.ShapeDtypeStruct((s // FOX_TILE, 1, FOX_HEADS), F32),
        jax.ShapeDtypeStruct((s // FOX_TILE, 1, FOX_HEADS), F32),
    ]
    stat = pl.BlockSpec((nsub, 1, FOX_HEADS), lambda i: (i, 0, 0))
    out_specs = [row(SWA_Q), row(2 * LANES), pl.BlockSpec((SWA_KV, tm), lambda i: (0, i)),
                 row(FOX_W), row(FOX_W), pl.BlockSpec((FOX_W, tm), lambda i: (0, i)),
                 row(D_MODEL), row(D_MODEL), row(N_PAIRS * LANES), stat, stat, stat, stat]
    consts = [g, sh, sc, wa, wb, wvt, wg, bfor, hind, place]
    return pl.pallas_call(
        _inproj_kernel,
        grid=(s // tm,),
        in_specs=[row(D_MODEL)] + [_const_spec(a.shape) for a in consts],
        out_specs=out_specs,
        out_shape=out_shape,
        scratch_shapes=[pltpu.VMEM((1, FOX_HEADS), F32)],
        compiler_params=_params(("arbitrary",)),
        name="in_proj",
    )(x2, *consts)


def _t5_buckets_np():
    qi = np.arange(BLOCK)[:, None]
    kj = np.arange(2 * BLOCK)[None, :]
    dist = BLOCK + qi - kj
    n = np.maximum(dist, 0)
    max_exact = REL_BUCKETS // 2
    nf = np.maximum(n, 1).astype(np.float32)
    large = max_exact + (np.log(nf / np.float32(max_exact)) / np.float32(math.log(REL_MAX_DIST / max_exact))
                         * np.float32(REL_BUCKETS - max_exact)).astype(np.int32)
    large = np.minimum(large, REL_BUCKETS - 1)
    bucket = np.where(n < max_exact, n, large).astype(np.int32)
    band = (dist >= 0) & (dist < WINDOW)
    return np.where(band, bucket, -1).astype(np.int32)


def _swa_kernel(rel_ref, sink_ref, bkt_ref, q_ref, kc_ref, kp_ref, vc_ref, vp_ref, o_ref, bias_ref):
    n = pl.program_id(0)

    @pl.when(n == 0)
    def _():
        bkt = bkt_ref[...]
        prev = lax.broadcasted_iota(jnp.int32, bkt.shape, 0) < BLOCK
        for h in range(SWA_HEADS):
            acc = jnp.full(bkt.shape, NEG_BIG, F32)
            for b in range(REL_BUCKETS):
                acc = jnp.where(bkt == b, rel_ref[b * SWA_HEADS + h] * LOG2E, acc)
            cols = slice((h % 2) * BLOCK, (h % 2 + 1) * BLOCK)
            bias_ref[0, h // 2, :, cols] = acc
            bias_ref[1, h // 2, :, cols] = jnp.where(prev, NEG_BIG, acc)

    lane = lax.broadcasted_iota(jnp.int32, (BLOCK, LANES), 1)
    col2 = lax.broadcasted_iota(jnp.int32, (1, 2 * BLOCK), 1)
    k_all = jnp.concatenate([kp_ref[...], kc_ref[...]], axis=0)
    v_all = jnp.concatenate([vp_ref[...], vc_ref[...]], axis=1)
    work = [(b, p) for b in range(SWA_BLOCKS) for p in range(SWA_HEADS // 2)]
    scores = []
    for b, p in work:
        qp = q_ref[b * BLOCK:(b + 1) * BLOCK, p * LANES:(p + 1) * LANES]
        zero = jnp.zeros_like(qp)
        qs = jnp.concatenate([jnp.where(lane < HEAD_DIM, qp, zero), jnp.where(lane >= HEAD_DIM, qp, zero)], axis=0)
        g = p // 2
        scores.append(_dot_nt(k_all[b * BLOCK:(b + 2) * BLOCK, g * LANES:(g + 1) * LANES], qs))
    weights = []
    for (b, p), s in zip(work, scores):
        first = jnp.where(n == 0, 1, 0) if b == 0 else 0
        s = s + bias_ref[first, p]
        sink = jnp.where(col2 < BLOCK, sink_ref[2 * p], sink_ref[2 * p + 1]) * LOG2E
        m = jnp.maximum(jnp.max(s, axis=0, keepdims=True), sink)
        e = jnp.exp2(s - m)
        denom = jnp.sum(e, axis=0, keepdims=True) + jnp.exp2(sink - m)
        weights.append((e.astype(BF16), denom))
    outs = [[] for _ in range(SWA_BLOCKS)]
    for (b, p), (e, denom) in zip(work, weights):
        g = p // 2
        o = _dot(v_all[g * HEAD_DIM:(g + 1) * HEAD_DIM, b * BLOCK:(b + 2) * BLOCK], e) / denom
        outs[b] += [o[:, :BLOCK], o[:, BLOCK:]]
    for b in range(SWA_BLOCKS):
        o_ref[b * BLOCK:(b + 1) * BLOCK, :] = jnp.concatenate(outs[b], axis=0).T.astype(BF16)


def _swa(rel_flat, sinks, bkt_t, qa, ka2, vat):
    s = qa.shape[0]
    rows = SWA_BLOCKS * BLOCK
    smem = pl.BlockSpec(memory_space=pltpu.SMEM)
    prev = lambda n: jnp.maximum(n * SWA_BLOCKS - 1, 0)
    return pl.pallas_call(
        _swa_kernel,
        grid=(s // rows,),
        in_specs=[smem, smem, _const_spec(bkt_t.shape),
                  pl.BlockSpec((rows, SWA_Q), lambda n: (n, 0)),
                  pl.BlockSpec((rows, 2 * LANES), lambda n: (n, 0)),
                  pl.BlockSpec((BLOCK, 2 * LANES), lambda n: (prev(n), 0)),
                  pl.BlockSpec((SWA_KV, rows), lambda n: (0, n)),
                  pl.BlockSpec((SWA_KV, BLOCK), lambda n: (0, prev(n)))],
        out_specs=pl.BlockSpec((rows, SWA_Q), lambda n: (n, 0)),
        out_shape=jax.ShapeDtypeStruct((s, SWA_Q), BF16),
        scratch_shapes=[pltpu.VMEM((2, SWA_HEADS // 2, 2 * BLOCK, 2 * BLOCK), F32)],
        compiler_params=_params(("arbitrary",)),
        name="swa_attn",
    )(rel_flat, sinks, bkt_t, qa, ka2, ka2, vat, vat)


def _fox_kernel(jlo_ref, base_ref, q_ref, k_ref, ca_ref, vt_ref, o_ref):
    p = pl.program_id(0)
    i = pl.program_id(1)
    t = FOX_TILE
    tq = FOX_QSUB * t
    first_diag = i * FOX_QSUB
    lane = lax.broadcasted_iota(jnp.int32, (tq, LANES), 1)
    q = q_ref[...]
    wq = []
    for hh in range(2):
        in_head = (lane >= hh * HEAD_DIM) & (lane < (hh + 1) * HEAD_DIM)
        qm = jnp.where(in_head, q, jnp.zeros_like(q))
        sel = (lane >= N_SPLIT * hh) & (lane < N_SPLIT * (hh + 1))
        aug = jnp.where(sel, -1.0, 0.0).astype(BF16)
        wq.append(jnp.concatenate([qm, aug], axis=1))
    key = lax.broadcasted_iota(jnp.int32, (t, tq), 0)
    qry = lax.broadcasted_iota(jnp.int32, (t, tq), 1)
    causal = [key + d * t <= qry for d in range(FOX_QSUB)]

    def step(j, carry, heads, nsub, diagonal=False):
        start = pl.multiple_of(j * t, t)
        rows = nsub * t
        lhs = jnp.concatenate([k_ref[pl.ds(start, rows), :], ca_ref[pl.ds(start, rows), :]], axis=1)
        scores = [_dot_nt(lhs, wq[hh]) for hh in heads]
        mid = []
        for hh, s, (m, l, acc) in zip(heads, scores, carry):
            h = 2 * p + hh
            parts = [s[k * t:(k + 1) * t] for k in range(nsub)]
            if diagonal:
                for d in range(FOX_QSUB):
                    k = nsub - FOX_QSUB + d
                    parts[k] = jnp.where(causal[d], parts[k], NEG_BIG)
            djs = [base_ref[h, first_diag] - base_ref[h, j + k] for k in range(nsub)]
            m_new = m
            for part, dj in zip(parts, djs):
                m_new = jnp.maximum(m_new, jnp.max(part, axis=0, keepdims=True) + dj)
            alpha = jnp.exp2(m - m_new)
            es = [jnp.exp2(part + (dj - m_new)) for part, dj in zip(parts, djs)]
            l = alpha * l
            for e in es:
                l = l + jnp.sum(e, axis=0, keepdims=True)
            e_all = es[0] if nsub == 1 else jnp.concatenate(es, axis=0)
            mid.append((m_new, l, alpha, acc, e_all.astype(BF16)))
        new = []
        for hh, (m_new, l, alpha, acc, e_all) in zip(heads, mid):
            vt = vt_ref[hh * HEAD_DIM:(hh + 1) * HEAD_DIM, pl.ds(start, rows)]
            new.append((m_new, l, alpha * acc + _dot(vt, e_all)))
        return tuple(new)

    def run_alone(lo, hi, carry, heads):
        n_full = (hi - lo) // FOX_CHUNK
        carry = lax.fori_loop(0, n_full, lambda n, c: step(lo + FOX_CHUNK * n, c, heads, FOX_CHUNK), carry)
        rest = lo + FOX_CHUNK * n_full
        tails = [lambda c: c] + [functools.partial(lambda c, k: step(rest, c, heads, k), k=k)
                                 for k in range(1, FOX_CHUNK)]
        return lax.switch(hi - rest, tails, carry)

    def run_to_diagonal(lo, carry, heads):
        end = first_diag + FOX_QSUB
        count = end - lo
        last = (count - FOX_QSUB) % FOX_CHUNK + FOX_QSUB
        n_full = (count - last) // FOX_CHUNK
        carry = lax.fori_loop(0, n_full, lambda n, c: step(lo + FOX_CHUNK * n, c, heads, FOX_CHUNK), carry)
        tails = [functools.partial(lambda c, k: step(end - k, c, heads, k, diagonal=True), k=k)
                 for k in range(FOX_QSUB, FOX_QSUB + FOX_CHUNK)]
        return lax.switch(last - FOX_QSUB, tails, carry)

    lo0 = jlo_ref[2 * p, i]
    lo1 = jlo_ref[2 * p + 1, i]
    lo_both = jnp.maximum(lo0, lo1)
    init = (jnp.full((1, tq), NEG_BIG, F32), jnp.zeros((1, tq), F32), jnp.zeros((HEAD_DIM, tq), F32))
    (c0,) = run_alone(lo0, lo_both, (init,), (0,))
    (c1,) = run_alone(lo1, lo_both, (init,), (1,))
    carry = run_to_diagonal(lo_both, (c0, c1), (0, 1))
    ot = jnp.concatenate([carry[0][2] / carry[0][1], carry[1][2] / carry[1][1]], axis=0)
    o_ref[...] = ot.T.astype(BF16)


def _fox_schedule(c_first, c_last, qn2, kn2):
    nt = qn2.shape[0]
    t = FOX_TILE
    bq = jnp.sqrt(qn2.reshape(nt, FOX_HEADS)) * NORM_SLACK
    bk = jnp.sqrt(kn2.reshape(nt, FOX_HEADS)) * NORM_SLACK
    c_first = c_first.reshape(nt, FOX_HEADS)
    c_last = c_last.reshape(nt, FOX_HEADS)
    upper = (bq[:, None, :] * bk[None, :, :] + (bq * bk)[:, None, :] + c_first[:, None, :] - c_last[None, :, :])
    ii = jnp.arange(nt)[:, None, None]
    jj = jnp.arange(nt)[None, :, None]
    needed = (jj <= ii) & ((upper >= -SKIP_LOG2) | (jj == ii))
    jlo = jnp.min(jnp.where(needed, jj, nt), axis=1)
    jlo = jnp.min(jlo.reshape(nt // FOX_QSUB, FOX_QSUB, FOX_HEADS), axis=1).T.astype(jnp.int32)
    base = jnp.concatenate([jnp.zeros((1, FOX_HEADS), F32), c_last[:-1]], axis=0).T
    return jlo, base


def _fox(jlo, base, qb, kb, ca, vbt):
    s = qb.shape[0]
    t = FOX_QSUB * FOX_TILE
    grid_spec = pltpu.PrefetchScalarGridSpec(
        num_scalar_prefetch=2,
        grid=(N_PAIRS, s // t),
        in_specs=[pl.BlockSpec((t, LANES), lambda p, i, *_: (i, p)),
                  pl.BlockSpec((s, LANES), lambda p, i, *_: (0, p)),
                  pl.BlockSpec((s, LANES), lambda p, i, *_: (0, p)),
                  pl.BlockSpec((LANES, s), lambda p, i, *_: (p, 0))],
        out_specs=pl.BlockSpec((t, LANES), lambda p, i, *_: (i, p)),
    )
    return pl.pallas_call(
        _fox_kernel,
        grid_spec=grid_spec,
        out_shape=jax.ShapeDtypeStruct((s, FOX_W), BF16),
        compiler_params=_params(("arbitrary", "arbitrary")),
        name="fox_attn",
    )(jlo, base, qb, kb, ca, vbt)


def _rms(x, g):
    return x * lax.rsqrt(jnp.mean(x * x, axis=-1, keepdims=True) + RMS_EPS) * g


def _pack_rows(v):
    halves = []
    for j in range(2):
        lo = v[:, (2 * j) * PACK_W:(2 * j + 1) * PACK_W].astype(BF16).astype(F32)
        hi = v[:, (2 * j + 1) * PACK_W:(2 * j + 2) * PACK_W].astype(BF16).astype(F32)
        lo_bits = lax.bitcast_convert_type(lo, jnp.uint32)
        hi_bits = lax.bitcast_convert_type(hi, jnp.uint32)
        halves.append(hi_bits | (lo_bits >> 16))
    return halves


def _unpack_words(w):
    lo = lax.bitcast_convert_type(w << 16, F32)
    hi = lax.bitcast_convert_type(w & jnp.uint32(0xFFFF0000), F32)
    return lo, hi


def _outproj_kernel(x_ref, ya_ref, yb_ref, sga_ref, sgb_ref, pa_ref, pb_ref, wo_ref, gt_ref, g_ref, sh_ref, sc_ref,
                    wr_ref, br_ref, x1_ref, u2p_ref, e4_ref, r4_ref, w4_ref, cnt_ref, carry_ref):
    i = pl.program_id(0)
    tm = x_ref.shape[0]

    @pl.when(i == 0)
    def _():
        carry_ref[...] = jnp.zeros_like(carry_ref)

    merged = (sga_ref[...].astype(F32) * _dot(ya_ref[...], pa_ref[...])
              + sgb_ref[...].astype(F32) * _dot(yb_ref[...], pb_ref[...]))
    x1 = x_ref[...] + gt_ref[...] * _dot(merged.astype(BF16), wo_ref[...])
    x1_ref[...] = x1
    u2 = _rms(x1, g_ref[...]) * (1.0 + sc_ref[...]) + sh_ref[...]
    halves = _pack_rows(u2)
    u2p_ref[0] = halves[0]
    u2p_ref[1] = halves[1]

    logits = _dot(u2.astype(BF16), wr_ref[...]) + br_ref[...]
    eidx = lax.broadcasted_iota(jnp.int32, logits.shape, 1)
    work = logits
    sel = jnp.zeros(logits.shape, jnp.bool_)
    picks, vals = [], []
    for k in range(TOP_K):
        m = jnp.max(work, axis=-1, keepdims=True)
        first = jnp.min(jnp.where(work == m, eidx, N_EXPERTS), axis=-1, keepdims=True)
        hit = eidx == first
        sel = sel | hit
        work = jnp.where(hit, -jnp.inf, work)
        picks.append(first)
        vals.append(m)
    exps = [jnp.exp(v - vals[0]) for v in vals]
    denom = exps[0] + exps[1] + exps[2] + exps[3]

    r = lax.broadcasted_iota(jnp.int32, (tm, tm), 0)
    cc = lax.broadcasted_iota(jnp.int32, (tm, tm), 1)
    before = (cc < r).astype(BF16)
    chosen = sel.astype(BF16)
    rank = _dot(before, chosen) + carry_ref[...]
    cnt = carry_ref[...] + jnp.sum(chosen.astype(F32), axis=0, keepdims=True)
    carry_ref[...] = cnt
    cnt_ref[...] = cnt
    slot = lax.broadcasted_iota(jnp.int32, (tm, 4 * TOP_K), 1)
    cols = jnp.zeros((tm, 4 * TOP_K), F32)
    for k in range(TOP_K):
        rk = jnp.sum(jnp.where(eidx == picks[k], rank, 0.0), axis=-1, keepdims=True)
        hi = jnp.floor(rk * (1.0 / RANK_RADIX))
        for j, col in ((k, picks[k].astype(F32)), (TOP_K + k, hi), (2 * TOP_K + k, rk - hi * RANK_RADIX)):
            cols = jnp.where(slot == j, col, cols)
        w4_ref[:, k:k + 1] = exps[k] / denom
    eye = (lax.broadcasted_iota(jnp.int32, (4 * TOP_K, 4 * TOP_K), 0)
           == lax.broadcasted_iota(jnp.int32, (4 * TOP_K, 4 * TOP_K), 1)).astype(BF16)
    flipped = _dot_nt(eye, cols.astype(BF16))
    e4_ref[...] = flipped[:TOP_K].astype(jnp.int32)
    r4_ref[...] = (flipped[TOP_K:2 * TOP_K] * RANK_RADIX + flipped[2 * TOP_K:3 * TOP_K]).astype(jnp.int32)


def _outproj(x2, ya, yb, sga, sgb, pa, pb, wo, gt, g, sh, sc, wr, br, tm):
    s = x2.shape[0]
    row = lambda n: pl.BlockSpec((tm, n), lambda i: (i, 0))
    consts = [pa, pb, wo, gt, g, sh, sc, wr, br]
    return pl.pallas_call(
        _outproj_kernel,
        grid=(s // tm,),
        in_specs=[row(D_MODEL), row(SWA_Q), row(FOX_W), row(D_MODEL), row(D_MODEL)] + [_const_spec(a.shape) for a in consts],
        out_specs=[row(D_MODEL), pl.BlockSpec((2, tm, PACK_W), lambda i: (0, i, 0)),
                   pl.BlockSpec((TOP_K, tm), lambda i: (0, i)), pl.BlockSpec((TOP_K, tm), lambda i: (0, i)), row(TOP_K),
                   _const_spec((1, N_EXPERTS))],
        out_shape=[jax.ShapeDtypeStruct((s, D_MODEL), F32),
                   jax.ShapeDtypeStruct((2, s, PACK_W), jnp.uint32),
                   jax.ShapeDtypeStruct((TOP_K, s), jnp.int32),
                   jax.ShapeDtypeStruct((TOP_K, s), jnp.int32),
                   jax.ShapeDtypeStruct((s, TOP_K), F32),
                   jax.ShapeDtypeStruct((1, N_EXPERTS), F32)],
        scratch_shapes=[pltpu.VMEM((1, N_EXPERTS), F32)],
        compiler_params=_params(("arbitrary",)),
        name="out_proj_router",
    )(x2, ya, yb, sga, sgb, *consts)


def _sc_mesh():
    return plsc.VectorSubcoreMesh(core_axis_name="core", subcore_axis_name="subcore")


def _sc_dispatch(rows, idx, n_out):
    n, width = rows.shape

    @functools.partial(pl.kernel, out_type=jax.ShapeDtypeStruct((n_out, width), rows.dtype), mesh=_sc_mesh(),
                       scratch_types=[])
    def dispatch(x_hbm, i_hbm, o_hbm):
        def body(x_vmem, i_vmem):
            for k in range(TOP_K):
                pltpu.sync_copy(x_vmem, o_hbm.at[i_vmem.at[k]])

        pltpu.emit_pipeline(
            body, grid=(n // SC_WINDOW,),
            in_specs=[pl.BlockSpec((SC_WINDOW, width), lambda i: (i, 0)),
                      pl.BlockSpec((TOP_K, SC_WINDOW), lambda i: (0, i))],
            out_specs=[], core_axis_name=("core", "subcore"), dimension_semantics=(pltpu.PARALLEL,),
        )(x_hbm, i_hbm)

    return dispatch(rows, idx)


def _sc_gather(table, idx):
    n = idx.shape[1]
    width = table.shape[1]

    @functools.partial(pl.kernel, out_type=jax.ShapeDtypeStruct((n, width), table.dtype), mesh=_sc_mesh(),
                       scratch_types=[])
    def gather(t_hbm, i_hbm, o_hbm):
        def body(i_vmem, o_vmem):
            pltpu.sync_copy(t_hbm.at[i_vmem.at[0]], o_vmem)

        pltpu.emit_pipeline(
            body, grid=(n // SC_WINDOW,),
            in_specs=[pl.BlockSpec((1, SC_WINDOW), lambda i: (0, i))],
            out_specs=[pl.BlockSpec((SC_WINDOW, width), lambda i: (i, 0))],
            core_axis_name=("core", "subcore"), dimension_semantics=(pltpu.PARALLEL,),
        )(i_hbm, o_hbm)

    return gather(table, idx)


def _sc_pack_weights(w1, w2):
    r1, r2 = w1.shape[0], w2.shape[0]
    half = D_FF // 2

    def bf16_bits(v):
        u = plsc.bitcast(v, jnp.uint32)
        return (u + jnp.uint32(0x7FFF) + ((u >> 16) & jnp.uint32(1))) >> 16

    cp = pltpu.CompilerParams()
    if "needs_layout_passes" in pltpu.CompilerParams.__dataclass_fields__:
        cp = dataclasses.replace(cp, needs_layout_passes=False)

    n_in = w1.size + w2.size
    cost = pl.CostEstimate(flops=8 * n_in, transcendentals=0, bytes_accessed=6 * n_in)
    @functools.partial(
        pl.kernel, mesh=_sc_mesh(), scratch_types=[], compiler_params=cp, cost_estimate=cost,
        out_type=(jax.ShapeDtypeStruct((r1, D_FF), jnp.uint32), jax.ShapeDtypeStruct((r2, D_MODEL // 2), jnp.uint32)))
    def pack(w1_hbm, w2_hbm, o1_hbm, o2_hbm):
        lanes = lax.iota(jnp.int32, SC_LANES)

        def body1(x_vmem, o_vmem):
            @pl.loop(0, SC_PACK_ROWS)
            def _(r):
                rr = jnp.full((SC_LANES,), r, jnp.int32)

                @pl.loop(0, half, step=SC_LANES)
                def _(c):
                    col = 2 * (c + lanes)
                    g0 = plsc.load_gather(x_vmem, [rr, col])
                    g1 = plsc.load_gather(x_vmem, [rr, col + 2 * half])
                    l0 = plsc.load_gather(x_vmem, [rr, col + 1])
                    l1 = plsc.load_gather(x_vmem, [rr, col + 2 * half + 1])
                    o_vmem[r, pl.ds(c, SC_LANES)] = bf16_bits(g0) | (bf16_bits(g1) << 16)
                    o_vmem[r, pl.ds(half + c, SC_LANES)] = bf16_bits(l0) | (bf16_bits(l1) << 16)

        def body2(x_vmem, o_vmem):
            @pl.loop(0, SC_PACK_ROWS)
            def _(r):
                @pl.loop(0, D_MODEL // 2, step=SC_LANES)
                def _(c):
                    lo = x_vmem[r, pl.ds(c, SC_LANES)]
                    hi = x_vmem[r, pl.ds(D_MODEL // 2 + c, SC_LANES)]
                    o_vmem[r, pl.ds(c, SC_LANES)] = bf16_bits(lo) | (bf16_bits(hi) << 16)

        for body, x_hbm, o_hbm, rows in ((body1, w1_hbm, o1_hbm, r1), (body2, w2_hbm, o2_hbm, r2)):
            pltpu.emit_pipeline(
                body, grid=(rows // SC_PACK_ROWS,),
                in_specs=[pl.BlockSpec((SC_PACK_ROWS, x_hbm.shape[1]), lambda i: (i, 0))],
                out_specs=[pl.BlockSpec((SC_PACK_ROWS, o_hbm.shape[1]), lambda i: (i, 0))],
                core_axis_name=("core", "subcore"), dimension_semantics=(pltpu.PARALLEL,),
            )(x_hbm, o_hbm)

    return pack(w1, w2)


def _moe_kernel(te_ref, first_ref, nact_ref, xs_ref, w1_ref, b1g_ref, b1l_ref, w2_ref, b2_ref, y_ref,
                wg_ref, wl_ref, w2b_ref):
    i = pl.program_id(0)
    live = i < nact_ref[0]

    @pl.when(jnp.logical_and(live, first_ref[i] == 1))
    def _():
        half = D_FF // 2
        for dst, words in ((wg_ref, w1_ref[0, :, :half]), (wl_ref, w1_ref[0, :, half:]), (w2b_ref, w2_ref[0])):
            lo, hi = _unpack_words(words)
            dst[:, :half] = lo.astype(BF16)
            dst[:, half:] = hi.astype(BF16)

    @pl.when(live)
    def _():
        chunks = []
        for j in range(2):
            lo, hi = _unpack_words(xs_ref[j])
            chunks += [lo.astype(BF16), hi.astype(BF16)]
        x = jnp.concatenate(chunks, axis=1)
        y = b2_ref[0]
        for n in range(D_FF // FF_SLAB):
            units = slice(n * FF_SLAB, (n + 1) * FF_SLAB)
            hg = _dot(x, wg_ref[:, units]) + b1g_ref[0, :, units]
            hl = _dot(x, wl_ref[:, units]) + b1l_ref[0, :, units]
            glu = jnp.minimum(hg, SWIGLU_LIMIT)
            lin = jnp.clip(hl, -SWIGLU_LIMIT, SWIGLU_LIMIT)
            a = glu * jax.nn.sigmoid(SWIGLU_ALPHA * glu) * (lin + 1.0)
            y = y + _dot(a.astype(BF16), w2b_ref[units, :])
        halves = _pack_rows(y)
        y_ref[0] = halves[0]
        y_ref[1] = halves[1]


def _moe_routed(tile_expert, tile_first, n_active, xs, w1, b1g, b1l, w2, b2):
    n_rows = xs.shape[1]
    n_tiles = n_rows // MOE_TILE
    live = lambda i, na: jnp.minimum(i, na[0] - 1)
    rows_spec = pl.BlockSpec((2, MOE_TILE, PACK_W), lambda i, te, tf, na: (0, live(i, na), 0))
    wsp = lambda a: pl.BlockSpec((1,) + a.shape[1:], lambda i, te, tf, na: (te[live(i, na)], 0, 0))
    grid_spec = pltpu.PrefetchScalarGridSpec(
        num_scalar_prefetch=3,
        grid=(n_tiles,),
        in_specs=[rows_spec, wsp(w1), wsp(b1g), wsp(b1l), wsp(w2), wsp(b2)],
        out_specs=rows_spec,
        scratch_shapes=[pltpu.VMEM((D_MODEL, D_FF), BF16), pltpu.VMEM((D_MODEL, D_FF), BF16),
                        pltpu.VMEM((D_FF, D_MODEL), BF16)],
    )
    return pl.pallas_call(
        _moe_kernel,
        grid_spec=grid_spec,
        out_shape=jax.ShapeDtypeStruct(xs.shape, jnp.uint32),
        compiler_params=_params(("arbitrary",)),
        name="moe_routed",
    )(tile_expert, tile_first, n_active, xs, w1, b1g, b1l, w2, b2)


def _final_kernel(x1_ref, yg_ref, w4_ref, gt_ref, gf_ref, *rest):
    o_ref = rest[-1]
    w4 = w4_ref[...]
    cols = []
    for j in range(2):
        lo_acc = hi_acc = None
        for k in range(TOP_K):
            lo, hi = _unpack_words(yg_ref[j, k])
            wk = w4[:, k:k + 1]
            lo_acc = wk * lo if lo_acc is None else lo_acc + wk * lo
            hi_acc = wk * hi if hi_acc is None else hi_acc + wk * hi
        cols += [lo_acc, hi_acc]
    moe = jnp.concatenate(cols, axis=1)
    x2 = x1_ref[...] + gt_ref[...] * moe
    o_ref[...] = _rms(x2, gf_ref[...])


def _final(x1, yg, w4, gt, gf, tm, first_tile, prev_out):
    s = x1.shape[0]
    n_tiles = yg.shape[2] // tm
    row = lambda n: pl.BlockSpec((tm, n), lambda i: (i + first_tile, 0))
    in_specs = [row(D_MODEL), pl.BlockSpec((2, TOP_K, tm, PACK_W), lambda i: (0, 0, i, 0)), row(TOP_K),
                _const_spec(gt.shape), _const_spec(gf.shape)]
    args = [x1, yg, w4, gt, gf]
    aliases = {}
    if prev_out is not None:
        in_specs.append(pl.BlockSpec(memory_space=pl.ANY))
        args.append(prev_out)
        aliases = {len(args) - 1: 0}
    return pl.pallas_call(
        _final_kernel,
        grid=(n_tiles,),
        in_specs=in_specs,
        out_specs=row(D_MODEL),
        out_shape=jax.ShapeDtypeStruct((s, D_MODEL), F32),
        input_output_aliases=aliases,
        compiler_params=_params(("arbitrary",)),
        name="combine_final_norm",
    )(*args)


def _routing_tables(e4, r4, counts, s):
    n_rows = TOP_K * s + N_EXPERTS * MOE_TILE
    cnt = counts.reshape(N_EXPERTS).astype(jnp.int32)
    padded = ((cnt + MOE_TILE - 1) // MOE_TILE) * MOE_TILE
    ends = jnp.cumsum(padded)
    starts = ends - padded
    pos = r4
    for e in range(N_EXPERTS):
        pos = pos + jnp.where(e4 == e, starts[e], 0)
    tile_start = jnp.arange(n_rows // MOE_TILE, dtype=jnp.int32) * MOE_TILE
    tile_expert = jnp.minimum(jnp.sum(tile_start[:, None] >= ends[None, :], axis=1), N_EXPERTS - 1).astype(jnp.int32)
    tile_first = jnp.concatenate([jnp.ones((1,), jnp.int32), (tile_expert[1:] != tile_expert[:-1]).astype(jnp.int32)])
    n_active = (ends[-1:] // MOE_TILE).astype(jnp.int32)
    return pos, tile_expert, tile_first, n_active, n_rows


def kernel(x, c, w_ada, b_ada, g_mix, w_in, b_forget, sinks, rel_bias, w_proj_a, w_proj_b, w_out, g_ffn,
           w_router, b_router, w_e1, b_e1, w_e2, b_e2, g_final):
    b, s, d = x.shape
    assert b == 1 and d == D_MODEL and w_ada.shape[0] == 1
    assert s <= 256 * RANK_RADIX
    x2 = x.reshape(s, d)
    tm = min(512, s)

    mod = _ada(c.reshape(d, 1), w_ada[0], b_ada)
    sh_m, sc_m, gt_m, sh_f, sc_f, gt_f = [mod[:, k * d:(k + 1) * d] for k in range(N_MOD)]

    w = w_in[0]
    o_ka, o_va, o_b = SWA_Q, SWA_Q + SWA_KV, SWA_Q + 2 * SWA_KV
    o_f = o_b + 3 * FOX_W
    o_g = o_f + FOX_HEADS
    dup = lambda m: jnp.concatenate([m[:, :HEAD_DIM], m[:, :HEAD_DIM], m[:, HEAD_DIM:], m[:, HEAD_DIM:]], axis=1)
    wa = jnp.concatenate([w[:, :SWA_Q], dup(w[:, o_ka:o_va])], axis=1).astype(BF16)
    pad = jnp.zeros((d, LANES - FOX_HEADS), F32)
    wb = jnp.concatenate([w[:, o_b:o_b + 2 * FOX_W], w[:, o_f:o_g], pad], axis=1).astype(BF16)
    wvt = jnp.concatenate([w[:, o_b + 2 * FOX_W:o_f], w[:, o_va:o_b]], axis=1).T.astype(BF16)
    wg = w[:, o_g:].astype(BF16)
    qa, ka2, vat, qb, kb, vbt, sga, sgb, ca, qn2, kn2, c_first, c_last = _inproj(
        x2, g_mix, sh_m, sc_m, wa, wb, wvt, wg, b_forget, tm)

    ya = _swa(rel_bias.reshape(-1), sinks[0], jnp.asarray(_t5_buckets_np().T), qa, ka2, vat)
    jlo, base = _fox_schedule(c_first, c_last, qn2, kn2)
    yb = _fox(jlo, base, qb, kb, ca, vbt)

    x1, u2p, e4, r4, w4, counts = _outproj(
        x2, ya, yb, sga, sgb, w_proj_a[0].astype(BF16), w_proj_b[0].astype(BF16), w_out[0].astype(BF16),
        gt_m, g_ffn, sh_f, sc_f, w_router[0].astype(BF16), b_router, tm)

    pos, tile_expert, tile_first, n_active, n_rows = _routing_tables(e4, r4, counts, s)
    pos2 = jnp.concatenate([pos, pos + n_rows], axis=1)
    w1p, w2p = _sc_pack_weights(w_e1[0].reshape(N_EXPERTS * d, 2 * D_FF), w_e2[0].reshape(N_EXPERTS * D_FF, d))
    pos2, w1p, w2p = lax.optimization_barrier((pos2, w1p, w2p))
    xs = _sc_dispatch(u2p.reshape(2 * s, PACK_W), pos2, 2 * n_rows).reshape(2, n_rows, PACK_W)

    b1 = b_e1[0].reshape(N_EXPERTS, D_FF, 2)
    b1g = b1[:, None, :, 0]
    b1l = b1[:, None, :, 1]
    ys = _moe_routed(tile_expert, tile_first, n_active, xs, w1p.reshape(N_EXPERTS, d, D_FF), b1g, b1l,
                     w2p.reshape(N_EXPERTS, D_FF, d // 2), b_e2[0][:, None, :])

    gather_idx = pos2.reshape(TOP_K, 2, s).transpose(1, 0, 2)
    sc_rows = s // COMBINE_CHUNKS
    out = None
    for ci in range(COMBINE_CHUNKS):
        idx = gather_idx[:, :, ci * sc_rows:(ci + 1) * sc_rows].reshape(1, -1)
        yg = _sc_gather(ys.reshape(2 * n_rows, PACK_W), idx).reshape(2, TOP_K, sc_rows, PACK_W)
        out = _final(x1, yg, w4, gt_f, g_final.reshape(1, d), tm, ci * (sc_rows // tm), out)
    return out.reshape(b, s, d)
```

```python
import dataclasses
import functools
import math

import numpy as np
import jax
import jax.numpy as jnp
from jax import lax
from jax.experimental import pallas as pl
from jax.experimental.pallas import tpu as pltpu
from jax.experimental.pallas import tpu_sc as plsc

D_MODEL = 1024
HEAD_DIM = 64
SWA_HEADS = 8
SWA_KV_HEADS = 2
WINDOW = 128
FOX_HEADS = 8
BLOCK = 128
REL_BUCKETS = 32
REL_MAX_DIST = WINDOW
N_EXPERTS = 32
TOP_K = 4
D_FF = D_MODEL
SWIGLU_LIMIT = 7.0
SWIGLU_ALPHA = 1.702
RMS_EPS = 1e-5
N_MOD = 6

SWA_Q = SWA_HEADS * HEAD_DIM
SWA_KV = SWA_KV_HEADS * HEAD_DIM
FOX_W = FOX_HEADS * HEAD_DIM
LANES = 128
N_PAIRS = FOX_HEADS // 2
NEG_BIG = -1e30
LOG2E = math.log2(math.e)
FOX_TILE = 256
SWA_BLOCKS = 4
FOX_QSUB = 2
FOX_CHUNK = 4
FOX_CHUNK_ALONE = 8
N_SPLIT = 3
SKIP_LOG2 = 127.0
NORM_SLACK = 1.01
PACK_W = 256
MOE_TILE = 512
SC_WINDOW = 128
RANK_RADIX = 128
FF_SLAB = 512
COMBINE_CHUNKS = 2
SC_LANES = 16
SC_PACK_ROWS = 8
VMEM_LIMIT = 56 * 1024 * 1024

F32 = jnp.float32
BF16 = jnp.bfloat16
HIGHEST = lax.Precision.HIGHEST


def _dot(a, b):
    return jnp.dot(a, b, preferred_element_type=F32)


def _dot_nt(a, b, precision=None):
    return lax.dot_general(a, b, (((1,), (1,)), ((), ())), preferred_element_type=F32, precision=precision)


def _const_spec(shape):
    nd = len(shape)
    return pl.BlockSpec(shape, lambda *_: (0,) * nd)


def _params(sem):
    return pltpu.CompilerParams(dimension_semantics=sem, vmem_limit_bytes=VMEM_LIMIT)


def _ada_kernel(c_ref, w_ref, b_ref, o_ref):
    c = c_ref[...]
    act = c * jax.nn.sigmoid(c)
    o_ref[...] = jnp.sum(act * w_ref[...], axis=0, keepdims=True) + b_ref[...]


def _ada(c_col, w_ada, b_ada):
    n = w_ada.shape[1]
    tn = 1024
    return pl.pallas_call(
        _ada_kernel,
        grid=(n // tn,),
        in_specs=[_const_spec((D_MODEL, 1)),
                  pl.BlockSpec((D_MODEL, tn), lambda j: (0, j)),
                  pl.BlockSpec((1, tn), lambda j: (0, j))],
        out_specs=pl.BlockSpec((1, tn), lambda j: (0, j)),
        out_shape=jax.ShapeDtypeStruct((1, n), F32),
        compiler_params=_params(("arbitrary",)),
        name="ada_mod",
    )(c_col, w_ada, b_ada)


def _split_bf16(v):
    parts = []
    for _ in range(N_SPLIT):
        p = v.astype(BF16)
        v = v - p.astype(F32)
        parts.append(p)
    return parts


def _inproj_kernel(x_ref, g_ref, sh_ref, sc_ref, wa_ref, wb_ref, wvt_ref, wg_ref, bf_ref, hind_ref, place_ref,
                   qa_ref, kva_ref, vat_ref, qb_ref, kb_ref, vbt_ref, sga_ref, sgb_ref, ca_ref, qn_ref, kn_ref,
                   cf_ref, cl_ref, carry_ref):
    i = pl.program_id(0)
    tm = x_ref.shape[0]
    t = FOX_TILE
    nsub = tm // t

    @pl.when(i == 0)
    def _():
        carry_ref[...] = jnp.zeros_like(carry_ref)

    xf = x_ref[...]
    ms = jnp.mean(xf * xf, axis=-1, keepdims=True)
    y = xf * lax.rsqrt(ms + RMS_EPS) * g_ref[...]
    u = y * (1.0 + sc_ref[...]) + sh_ref[...]
    ub = u.astype(BF16)

    za = _dot(ub, wa_ref[...])
    qa_ref[...] = (za[:, :SWA_Q] * (HEAD_DIM ** -0.5 * LOG2E)).astype(BF16)
    kva_ref[...] = za[:, SWA_Q:].astype(BF16)

    zb = _dot(ub, wb_ref[...])
    qb = (zb[:, :FOX_W] * (HEAD_DIM ** -0.5 * LOG2E)).astype(BF16)
    kb = zb[:, FOX_W:2 * FOX_W].astype(BF16)
    qb_ref[...] = qb
    kb_ref[...] = kb
    vt = _dot_nt(wvt_ref[...], ub).astype(BF16)
    vbt_ref[...] = vt[:FOX_W]
    vat_ref[...] = vt[FOX_W:]

    def tile_norm_max(z, o_ref):
        zf = z.astype(F32)
        n2 = _dot((zf * zf).astype(BF16), hind_ref[...])
        for sb in range(nsub):
            o_ref[sb] = jnp.max(n2[sb * t:(sb + 1) * t], axis=0, keepdims=True)

    tile_norm_max(qb, qn_ref)
    tile_norm_max(kb, kn_ref)

    zg = _dot(ub, wg_ref[...])
    sg = jax.nn.sigmoid(zg)
    sga_ref[...] = sg[:, :D_MODEL].astype(BF16)
    sgb_ref[...] = sg[:, D_MODEL:].astype(BF16)

    fb = zb[:, 2 * FOX_W:2 * FOX_W + FOX_HEADS] + bf_ref[...]
    lf = jnp.minimum(fb, 0.0) - jnp.log1p(jnp.exp(-jnp.abs(fb)))
    r = lax.broadcasted_iota(jnp.int32, (t, t), 0)
    cc = lax.broadcasted_iota(jnp.int32, (t, t), 1)
    lower = (cc <= r).astype(BF16)
    carry = carry_ref[...]
    for sb in range(nsub):
        rows = slice(sb * t, (sb + 1) * t)
        local = sum(_dot(lower, p) for p in _split_bf16(lf[rows]))
        cf_ref[sb] = (local[0:1, :] + carry) * LOG2E
        carry = carry + local[t - 1:t, :]
        cl_ref[sb] = carry * LOG2E
        aug = sum(_dot(p, place_ref[k]) for k, p in enumerate(_split_bf16(local * LOG2E)))
        ca_ref[rows, :] = aug.astype(BF16)
    carry_ref[...] = carry


def _inproj(x2, g, sh, sc, wa, wb, wvt, wg, bfor, tm):
    s = x2.shape[0]
    nsub = tm // FOX_TILE
    row = lambda n: pl.BlockSpec((tm, n), lambda i: (i, 0))
    hind = np.zeros((FOX_W, FOX_HEADS), np.float32)
    hind[np.arange(FOX_W), np.arange(FOX_W) // HEAD_DIM] = 1.0
    place = np.zeros((N_SPLIT, FOX_HEADS, N_PAIRS * LANES), np.float32)
    for k in range(N_SPLIT):
        for h in range(FOX_HEADS):
            place[k, h, (h // 2) * LANES + N_SPLIT * (h % 2) + k] = 1.0
    hind = jnp.asarray(hind, BF16)
    place = jnp.asarray(place, BF16)
    out_shape = [
        jax.ShapeDtypeStruct((s, SWA_Q), BF16),
        jax.ShapeDtypeStruct((s, 2 * LANES), BF16),
        jax.ShapeDtypeStruct((SWA_KV, s), BF16),
        jax.ShapeDtypeStruct((s, FOX_W), BF16),
        jax.ShapeDtypeStruct((s, FOX_W), BF16),
        jax.ShapeDtypeStruct((FOX_W, s), BF16),
        jax.ShapeDtypeStruct((s, D_MODEL), BF16),
        jax.ShapeDtypeStruct((s, D_MODEL), BF16),
        jax.ShapeDtypeStruct((s, N_PAIRS * LANES), BF16),
        jax.ShapeDtypeStruct((s // FOX_TILE, 1, FOX_HEADS), F32),
        jax.ShapeDtypeStruct((s // FOX_TILE, 1, FOX_HEADS), F32),
        jax.ShapeDtypeStruct((s // FOX_TILE, 1, FOX_HEADS), F32),
        jax.ShapeDtypeStruct((s // FOX_TILE, 1, FOX_HEADS), F32),
    ]
    stat = pl.BlockSpec((nsub, 1, FOX_HEADS), lambda i: (i, 0, 0))
    out_specs = [row(SWA_Q), row(2 * LANES), pl.BlockSpec((SWA_KV, tm), lambda i: (0, i)),
                 row(FOX_W), row(FOX_W), pl.BlockSpec((FOX_W, tm), lambda i: (0, i)),
                 row(D_MODEL), row(D_MODEL), row(N_PAIRS * LANES), stat, stat, stat, stat]
    consts = [g, sh, sc, wa, wb, wvt, wg, bfor, hind, place]
    return pl.pallas_call(
        _inproj_kernel,
        grid=(s // tm,),
        in_specs=[row(D_MODEL)] + [_const_spec(a.shape) for a in consts],
        out_specs=out_specs,
        out_shape=out_shape,
        scratch_shapes=[pltpu.VMEM((1, FOX_HEADS), F32)],
        compiler_params=_params(("arbitrary",)),
        name="in_proj",
    )(x2, *consts)


def _t5_buckets_np():
    qi = np.arange(BLOCK)[:, None]
    kj = np.arange(2 * BLOCK)[None, :]
    dist = BLOCK + qi - kj
    n = np.maximum(dist, 0)
    max_exact = REL_BUCKETS // 2
    nf = np.maximum(n, 1).astype(np.float32)
    large = max_exact + (np.log(nf / np.float32(max_exact)) / np.float32(math.log(REL_MAX_DIST / max_exact))
                         * np.float32(REL_BUCKETS - max_exact)).astype(np.int32)
    large = np.minimum(large, REL_BUCKETS - 1)
    bucket = np.where(n < max_exact, n, large).astype(np.int32)
    band = (dist >= 0) & (dist < WINDOW)
    return np.where(band, bucket, -1).astype(np.int32)


def _swa_kernel(rel_ref, sink_ref, bkt_ref, q_ref, kc_ref, kp_ref, vc_ref, vp_ref, o_ref, bias_ref):
    n = pl.program_id(0)

    @pl.when(n == 0)
    def _():
        bkt = bkt_ref[...]
        prev = lax.broadcasted_iota(jnp.int32, bkt.shape, 0) < BLOCK
        for h in range(SWA_HEADS):
            acc = jnp.full(bkt.shape, NEG_BIG, F32)
            for b in range(REL_BUCKETS):
                acc = jnp.where(bkt == b, rel_ref[b * SWA_HEADS + h] * LOG2E, acc)
            cols = slice((h % 2) * BLOCK, (h % 2 + 1) * BLOCK)
            bias_ref[0, h // 2, :, cols] = acc
            bias_ref[1, h // 2, :, cols] = jnp.where(prev, NEG_BIG, acc)

    lane = lax.broadcasted_iota(jnp.int32, (BLOCK, LANES), 1)
    col2 = lax.broadcasted_iota(jnp.int32, (1, 2 * BLOCK), 1)
    k_all = jnp.concatenate([kp_ref[...], kc_ref[...]], axis=0)
    v_all = jnp.concatenate([vp_ref[...], vc_ref[...]], axis=1)
    work = [(b, p) for b in range(SWA_BLOCKS) for p in range(SWA_HEADS // 2)]
    scores = []
    for b, p in work:
        qp = q_ref[b * BLOCK:(b + 1) * BLOCK, p * LANES:(p + 1) * LANES]
        zero = jnp.zeros_like(qp)
        qs = jnp.concatenate([jnp.where(lane < HEAD_DIM, qp, zero), jnp.where(lane >= HEAD_DIM, qp, zero)], axis=0)
        g = p // 2
        scores.append(_dot_nt(k_all[b * BLOCK:(b + 2) * BLOCK, g * LANES:(g + 1) * LANES], qs))
    weights = []
    for (b, p), s in zip(work, scores):
        first = jnp.where(n == 0, 1, 0) if b == 0 else 0
        s = s + bias_ref[first, p]
        sink = jnp.where(col2 < BLOCK, sink_ref[2 * p], sink_ref[2 * p + 1]) * LOG2E
        m = jnp.maximum(jnp.max(s, axis=0, keepdims=True), sink)
        e = jnp.exp2(s - m)
        denom = jnp.sum(e, axis=0, keepdims=True) + jnp.exp2(sink - m)
        weights.append((e.astype(BF16), denom))
    outs = [[] for _ in range(SWA_BLOCKS)]
    for (b, p), (e, denom) in zip(work, weights):
        g = p // 2
        o = _dot(v_all[g * HEAD_DIM:(g + 1) * HEAD_DIM, b * BLOCK:(b + 2) * BLOCK], e) / denom
        outs[b] += [o[:, :BLOCK], o[:, BLOCK:]]
    for b in range(SWA_BLOCKS):
        o_ref[b * BLOCK:(b + 1) * BLOCK, :] = jnp.concatenate(outs[b], axis=0).T.astype(BF16)


def _swa(rel_flat, sinks, bkt_t, qa, ka2, vat):
    s = qa.shape[0]
    rows = SWA_BLOCKS * BLOCK
    smem = pl.BlockSpec(memory_space=pltpu.SMEM)
    prev = lambda n: jnp.maximum(n * SWA_BLOCKS - 1, 0)
    return pl.pallas_call(
        _swa_kernel,
        grid=(s // rows,),
        in_specs=[smem, smem, _const_spec(bkt_t.shape),
                  pl.BlockSpec((rows, SWA_Q), lambda n: (n, 0)),
                  pl.BlockSpec((rows, 2 * LANES), lambda n: (n, 0)),
                  pl.BlockSpec((BLOCK, 2 * LANES), lambda n: (prev(n), 0)),
                  pl.BlockSpec((SWA_KV, rows), lambda n: (0, n)),
                  pl.BlockSpec((SWA_KV, BLOCK), lambda n: (0, prev(n)))],
        out_specs=pl.BlockSpec((rows, SWA_Q), lambda n: (n, 0)),
        out_shape=jax.ShapeDtypeStruct((s, SWA_Q), BF16),
        scratch_shapes=[pltpu.VMEM((2, SWA_HEADS // 2, 2 * BLOCK, 2 * BLOCK), F32)],
        compiler_params=_params(("arbitrary",)),
        name="swa_attn",
    )(rel_flat, sinks, bkt_t, qa, ka2, ka2, vat, vat)


def _fox_kernel(jlo_ref, base_ref, q_ref, k_ref, ca_ref, vt_ref, o_ref):
    p = pl.program_id(0)
    i = pl.program_id(1)
    t = FOX_TILE
    tq = FOX_QSUB * t
    first_diag = i * FOX_QSUB
    lane = lax.broadcasted_iota(jnp.int32, (tq, LANES), 1)
    q = q_ref[...]
    wq = []
    for hh in range(2):
        in_head = (lane >= hh * HEAD_DIM) & (lane < (hh + 1) * HEAD_DIM)
        qm = jnp.where(in_head, q, jnp.zeros_like(q))
        sel = (lane >= N_SPLIT * hh) & (lane < N_SPLIT * (hh + 1))
        aug = jnp.where(sel, -1.0, 0.0).astype(BF16)
        wq.append(jnp.concatenate([qm, aug], axis=1))
    key = lax.broadcasted_iota(jnp.int32, (t, tq), 0)
    qry = lax.broadcasted_iota(jnp.int32, (t, tq), 1)
    causal = [key + d * t <= qry for d in range(FOX_QSUB)]

    def step(j, carry, heads, nsub, diagonal=False):
        start = pl.multiple_of(j * t, t)
        rows = nsub * t
        lhs = jnp.concatenate([k_ref[pl.ds(start, rows), :], ca_ref[pl.ds(start, rows), :]], axis=1)
        scores = [_dot_nt(lhs, wq[hh]) for hh in heads]
        mid = []
        for hh, s, (m, l, acc) in zip(heads, scores, carry):
            h = 2 * p + hh
            parts = [s[k * t:(k + 1) * t] for k in range(nsub)]
            if diagonal:
                for d in range(FOX_QSUB):
                    k = nsub - FOX_QSUB + d
                    parts[k] = jnp.where(causal[d], parts[k], NEG_BIG)
            djs = [base_ref[h, first_diag] - base_ref[h, j + k] for k in range(nsub)]
            m_new = m
            for part, dj in zip(parts, djs):
                m_new = jnp.maximum(m_new, jnp.max(part, axis=0, keepdims=True) + dj)
            alpha = jnp.exp2(m - m_new)
            es = [jnp.exp2(part + (dj - m_new)) for part, dj in zip(parts, djs)]
            l = alpha * l
            for e in es:
                l = l + jnp.sum(e, axis=0, keepdims=True)
            e_all = es[0] if nsub == 1 else jnp.concatenate(es, axis=0)
            mid.append((m_new, l, alpha, acc, e_all.astype(BF16)))
        new = []
        for hh, (m_new, l, alpha, acc, e_all) in zip(heads, mid):
            vt = vt_ref[hh * HEAD_DIM:(hh + 1) * HEAD_DIM, pl.ds(start, rows)]
            new.append((m_new, l, alpha * acc + _dot(vt, e_all)))
        return tuple(new)

    def run_alone(lo, hi, carry, heads):
        n_long = (hi - lo) // FOX_CHUNK_ALONE
        carry = lax.fori_loop(0, n_long, lambda n, c: step(lo + FOX_CHUNK_ALONE * n, c, heads, FOX_CHUNK_ALONE), carry)
        mid = lo + FOX_CHUNK_ALONE * n_long
        n_full = (hi - mid) // FOX_CHUNK
        carry = lax.fori_loop(0, n_full, lambda n, c: step(mid + FOX_CHUNK * n, c, heads, FOX_CHUNK), carry)
        rest = mid + FOX_CHUNK * n_full
        tails = [lambda c: c] + [functools.partial(lambda c, k: step(rest, c, heads, k), k=k)
                                 for k in range(1, FOX_CHUNK)]
        return lax.switch(hi - rest, tails, carry)

    def run_to_diagonal(lo, carry, heads):
        end = first_diag + FOX_QSUB
        count = end - lo
        last = (count - FOX_QSUB) % FOX_CHUNK + FOX_QSUB
        n_full = (count - last) // FOX_CHUNK
        carry = lax.fori_loop(0, n_full, lambda n, c: step(lo + FOX_CHUNK * n, c, heads, FOX_CHUNK), carry)
        tails = [functools.partial(lambda c, k: step(end - k, c, heads, k, diagonal=True), k=k)
                 for k in range(FOX_QSUB, FOX_QSUB + FOX_CHUNK)]
        return lax.switch(last - FOX_QSUB, tails, carry)

    lo0 = jlo_ref[2 * p, i]
    lo1 = jlo_ref[2 * p + 1, i]
    lo_both = jnp.maximum(lo0, lo1)
    init = (jnp.full((1, tq), NEG_BIG, F32), jnp.zeros((1, tq), F32), jnp.zeros((HEAD_DIM, tq), F32))
    (c0,) = run_alone(lo0, lo_both, (init,), (0,))
    (c1,) = run_alone(lo1, lo_both, (init,), (1,))
    carry = run_to_diagonal(lo_both, (c0, c1), (0, 1))
    ot = jnp.concatenate([carry[0][2] / carry[0][1], carry[1][2] / carry[1][1]], axis=0)
    o_ref[...] = ot.T.astype(BF16)


def _fox_schedule(c_first, c_last, qn2, kn2):
    nt = qn2.shape[0]
    t = FOX_TILE
    bq = jnp.sqrt(qn2.reshape(nt, FOX_HEADS)) * NORM_SLACK
    bk = jnp.sqrt(kn2.reshape(nt, FOX_HEADS)) * NORM_SLACK
    c_first = c_first.reshape(nt, FOX_HEADS)
    c_last = c_last.reshape(nt, FOX_HEADS)
    upper = (bq[:, None, :] * bk[None, :, :] + (bq * bk)[:, None, :] + c_first[:, None, :] - c_last[None, :, :])
    ii = jnp.arange(nt)[:, None, None]
    jj = jnp.arange(nt)[None, :, None]
    needed = (jj <= ii) & ((upper >= -SKIP_LOG2) | (jj == ii))
    jlo = jnp.min(jnp.where(needed, jj, nt), axis=1)
    jlo = jnp.min(jlo.reshape(nt // FOX_QSUB, FOX_QSUB, FOX_HEADS), axis=1).T.astype(jnp.int32)
    base = jnp.concatenate([jnp.zeros((1, FOX_HEADS), F32), c_last[:-1]], axis=0).T
    return jlo, base


def _fox(jlo, base, qb, kb, ca, vbt):
    s = qb.shape[0]
    t = FOX_QSUB * FOX_TILE
    grid_spec = pltpu.PrefetchScalarGridSpec(
        num_scalar_prefetch=2,
        grid=(N_PAIRS, s // t),
        in_specs=[pl.BlockSpec((t, LANES), lambda p, i, *_: (i, p)),
                  pl.BlockSpec((s, LANES), lambda p, i, *_: (0, p)),
                  pl.BlockSpec((s, LANES), lambda p, i, *_: (0, p)),
                  pl.BlockSpec((LANES, s), lambda p, i, *_: (p, 0))],
        out_specs=pl.BlockSpec((t, LANES), lambda p, i, *_: (i, p)),
    )
    return pl.pallas_call(
        _fox_kernel,
        grid_spec=grid_spec,
        out_shape=jax.ShapeDtypeStruct((s, FOX_W), BF16),
        compiler_params=_params(("arbitrary", "arbitrary")),
        name="fox_attn",
    )(jlo, base, qb, kb, ca, vbt)


def _rms(x, g):
    return x * lax.rsqrt(jnp.mean(x * x, axis=-1, keepdims=True) + RMS_EPS) * g


def _pack_rows(v):
    halves = []
    for j in range(2):
        lo = v[:, (2 * j) * PACK_W:(2 * j + 1) * PACK_W].astype(BF16).astype(F32)
        hi = v[:, (2 * j + 1) * PACK_W:(2 * j + 2) * PACK_W].astype(BF16).astype(F32)
        lo_bits = lax.bitcast_convert_type(lo, jnp.uint32)
        hi_bits = lax.bitcast_convert_type(hi, jnp.uint32)
        halves.append(hi_bits | (lo_bits >> 16))
    return halves


def _unpack_words(w):
    lo = lax.bitcast_convert_type(w << 16, F32)
    hi = lax.bitcast_convert_type(w & jnp.uint32(0xFFFF0000), F32)
    return lo, hi


def _outproj_kernel(x_ref, ya_ref, yb_ref, sga_ref, sgb_ref, pa_ref, pb_ref, wo_ref, gt_ref, g_ref, sh_ref, sc_ref,
                    wr_ref, br_ref, x1_ref, u2p_ref, e4_ref, r4_ref, w4_ref, cnt_ref, carry_ref):
    i = pl.program_id(0)
    tm = x_ref.shape[0]

    @pl.when(i == 0)
    def _():
        carry_ref[...] = jnp.zeros_like(carry_ref)

    merged = (sga_ref[...].astype(F32) * _dot(ya_ref[...], pa_ref[...])
              + sgb_ref[...].astype(F32) * _dot(yb_ref[...], pb_ref[...]))
    x1 = x_ref[...] + gt_ref[...] * _dot(merged.astype(BF16), wo_ref[...])
    x1_ref[...] = x1
    u2 = _rms(x1, g_ref[...]) * (1.0 + sc_ref[...]) + sh_ref[...]
    halves = _pack_rows(u2)
    u2p_ref[0] = halves[0]
    u2p_ref[1] = halves[1]

    logits = _dot(u2.astype(BF16), wr_ref[...]) + br_ref[...]
    eidx = lax.broadcasted_iota(jnp.int32, logits.shape, 1)
    work = logits
    sel = jnp.zeros(logits.shape, jnp.bool_)
    picks, vals = [], []
    for k in range(TOP_K):
        m = jnp.max(work, axis=-1, keepdims=True)
        first = jnp.min(jnp.where(work == m, eidx, N_EXPERTS), axis=-1, keepdims=True)
        hit = eidx == first
        sel = sel | hit
        work = jnp.where(hit, -jnp.inf, work)
        picks.append(first)
        vals.append(m)
    exps = [jnp.exp(v - vals[0]) for v in vals]
    denom = exps[0] + exps[1] + exps[2] + exps[3]

    r = lax.broadcasted_iota(jnp.int32, (tm, tm), 0)
    cc = lax.broadcasted_iota(jnp.int32, (tm, tm), 1)
    before = (cc < r).astype(BF16)
    chosen = sel.astype(BF16)
    rank = _dot(before, chosen) + carry_ref[...]
    cnt = carry_ref[...] + jnp.sum(chosen.astype(F32), axis=0, keepdims=True)
    carry_ref[...] = cnt
    cnt_ref[...] = cnt
    slot = lax.broadcasted_iota(jnp.int32, (tm, 4 * TOP_K), 1)
    cols = jnp.zeros((tm, 4 * TOP_K), F32)
    for k in range(TOP_K):
        rk = jnp.sum(jnp.where(eidx == picks[k], rank, 0.0), axis=-1, keepdims=True)
        hi = jnp.floor(rk * (1.0 / RANK_RADIX))
        for j, col in ((k, picks[k].astype(F32)), (TOP_K + k, hi), (2 * TOP_K + k, rk - hi * RANK_RADIX)):
            cols = jnp.where(slot == j, col, cols)
        w4_ref[:, k:k + 1] = exps[k] / denom
    eye = (lax.broadcasted_iota(jnp.int32, (4 * TOP_K, 4 * TOP_K), 0)
           == lax.broadcasted_iota(jnp.int32, (4 * TOP_K, 4 * TOP_K), 1)).astype(BF16)
    flipped = _dot_nt(eye, cols.astype(BF16))
    e4_ref[...] = flipped[:TOP_K].astype(jnp.int32)
    r4_ref[...] = (flipped[TOP_K:2 * TOP_K] * RANK_RADIX + flipped[2 * TOP_K:3 * TOP_K]).astype(jnp.int32)


def _outproj(x2, ya, yb, sga, sgb, pa, pb, wo, gt, g, sh, sc, wr, br, tm):
    s = x2.shape[0]
    row = lambda n: pl.BlockSpec((tm, n), lambda i: (i, 0))
    consts = [pa, pb, wo, gt, g, sh, sc, wr, br]
    return pl.pallas_call(
        _outproj_kernel,
        grid=(s // tm,),
        in_specs=[row(D_MODEL), row(SWA_Q), row(FOX_W), row(D_MODEL), row(D_MODEL)] + [_const_spec(a.shape) for a in consts],
        out_specs=[row(D_MODEL), pl.BlockSpec((2, tm, PACK_W), lambda i: (0, i, 0)),
                   pl.BlockSpec((TOP_K, tm), lambda i: (0, i)), pl.BlockSpec((TOP_K, tm), lambda i: (0, i)), row(TOP_K),
                   _const_spec((1, N_EXPERTS))],
        out_shape=[jax.ShapeDtypeStruct((s, D_MODEL), F32),
                   jax.ShapeDtypeStruct((2, s, PACK_W), jnp.uint32),
                   jax.ShapeDtypeStruct((TOP_K, s), jnp.int32),
                   jax.ShapeDtypeStruct((TOP_K, s), jnp.int32),
                   jax.ShapeDtypeStruct((s, TOP_K), F32),
                   jax.ShapeDtypeStruct((1, N_EXPERTS), F32)],
        scratch_shapes=[pltpu.VMEM((1, N_EXPERTS), F32)],
        compiler_params=_params(("arbitrary",)),
        name="out_proj_router",
    )(x2, ya, yb, sga, sgb, *consts)


def _sc_mesh():
    return plsc.VectorSubcoreMesh(core_axis_name="core", subcore_axis_name="subcore")


def _sc_dispatch(rows, idx, n_out):
    n, width = rows.shape

    @functools.partial(pl.kernel, out_type=jax.ShapeDtypeStruct((n_out, width), rows.dtype), mesh=_sc_mesh(),
                       scratch_types=[])
    def dispatch(x_hbm, i_hbm, o_hbm):
        def body(x_vmem, i_vmem):
            for k in range(TOP_K):
                pltpu.sync_copy(x_vmem, o_hbm.at[i_vmem.at[k]])

        pltpu.emit_pipeline(
            body, grid=(n // SC_WINDOW,),
            in_specs=[pl.BlockSpec((SC_WINDOW, width), lambda i: (i, 0)),
                      pl.BlockSpec((TOP_K, SC_WINDOW), lambda i: (0, i))],
            out_specs=[], core_axis_name=("core", "subcore"), dimension_semantics=(pltpu.PARALLEL,),
        )(x_hbm, i_hbm)

    return dispatch(rows, idx)


def _sc_gather(table, idx):
    n = idx.shape[1]
    width = table.shape[1]

    @functools.partial(pl.kernel, out_type=jax.ShapeDtypeStruct((n, width), table.dtype), mesh=_sc_mesh(),
                       scratch_types=[])
    def gather(t_hbm, i_hbm, o_hbm):
        def body(i_vmem, o_vmem):
            pltpu.sync_copy(t_hbm.at[i_vmem.at[0]], o_vmem)

        pltpu.emit_pipeline(
            body, grid=(n // SC_WINDOW,),
            in_specs=[pl.BlockSpec((1, SC_WINDOW), lambda i: (0, i))],
            out_specs=[pl.BlockSpec((SC_WINDOW, width), lambda i: (i, 0))],
            core_axis_name=("core", "subcore"), dimension_semantics=(pltpu.PARALLEL,),
        )(i_hbm, o_hbm)

    return gather(table, idx)


def _sc_pack_weights(w1, w2):
    r1, r2 = w1.shape[0], w2.shape[0]
    half = D_FF // 2

    def bf16_bits(v):
        u = plsc.bitcast(v, jnp.uint32)
        return (u + jnp.uint32(0x7FFF) + ((u >> 16) & jnp.uint32(1))) >> 16

    cp = pltpu.CompilerParams()
    if "needs_layout_passes" in pltpu.CompilerParams.__dataclass_fields__:
        cp = dataclasses.replace(cp, needs_layout_passes=False)

    n_in = w1.size + w2.size
    cost = pl.CostEstimate(flops=8 * n_in, transcendentals=0, bytes_accessed=6 * n_in)
    @functools.partial(
        pl.kernel, mesh=_sc_mesh(), scratch_types=[], compiler_params=cp, cost_estimate=cost,
        out_type=(jax.ShapeDtypeStruct((r1, D_FF), jnp.uint32), jax.ShapeDtypeStruct((r2, D_MODEL // 2), jnp.uint32)))
    def pack(w1_hbm, w2_hbm, o1_hbm, o2_hbm):
        lanes = lax.iota(jnp.int32, SC_LANES)

        def body1(x_vmem, o_vmem):
            @pl.loop(0, SC_PACK_ROWS)
            def _(r):
                rr = jnp.full((SC_LANES,), r, jnp.int32)

                @pl.loop(0, half, step=SC_LANES)
                def _(c):
                    col = 2 * (c + lanes)
                    g0 = plsc.load_gather(x_vmem, [rr, col])
                    g1 = plsc.load_gather(x_vmem, [rr, col + 2 * half])
                    l0 = plsc.load_gather(x_vmem, [rr, col + 1])
                    l1 = plsc.load_gather(x_vmem, [rr, col + 2 * half + 1])
                    o_vmem[r, pl.ds(c, SC_LANES)] = bf16_bits(g0) | (bf16_bits(g1) << 16)
                    o_vmem[r, pl.ds(half + c, SC_LANES)] = bf16_bits(l0) | (bf16_bits(l1) << 16)

        def body2(x_vmem, o_vmem):
            @pl.loop(0, SC_PACK_ROWS)
            def _(r):
                @pl.loop(0, D_MODEL // 2, step=SC_LANES)
                def _(c):
                    lo = x_vmem[r, pl.ds(c, SC_LANES)]
                    hi = x_vmem[r, pl.ds(D_MODEL // 2 + c, SC_LANES)]
                    o_vmem[r, pl.ds(c, SC_LANES)] = bf16_bits(lo) | (bf16_bits(hi) << 16)

        for body, x_hbm, o_hbm, rows in ((body1, w1_hbm, o1_hbm, r1), (body2, w2_hbm, o2_hbm, r2)):
            pltpu.emit_pipeline(
                body, grid=(rows // SC_PACK_ROWS,),
                in_specs=[pl.BlockSpec((SC_PACK_ROWS, x_hbm.shape[1]), lambda i: (i, 0))],
                out_specs=[pl.BlockSpec((SC_PACK_ROWS, o_hbm.shape[1]), lambda i: (i, 0))],
                core_axis_name=("core", "subcore"), dimension_semantics=(pltpu.PARALLEL,),
            )(x_hbm, o_hbm)

    return pack(w1, w2)


def _moe_kernel(te_ref, first_ref, nact_ref, xs_ref, w1_ref, b1g_ref, b1l_ref, w2_ref, b2_ref, y_ref,
                wg_ref, wl_ref, w2b_ref):
    i = pl.program_id(0)
    live = i < nact_ref[0]

    @pl.when(jnp.logical_and(live, first_ref[i] == 1))
    def _():
        half = D_FF // 2
        for dst, words in ((wg_ref, w1_ref[0, :, :half]), (wl_ref, w1_ref[0, :, half:]), (w2b_ref, w2_ref[0])):
            lo, hi = _unpack_words(words)
            dst[:, :half] = lo.astype(BF16)
            dst[:, half:] = hi.astype(BF16)

    @pl.when(live)
    def _():
        chunks = []
        for j in range(2):
            lo, hi = _unpack_words(xs_ref[j])
            chunks += [lo.astype(BF16), hi.astype(BF16)]
        x = jnp.concatenate(chunks, axis=1)
        y = b2_ref[0]
        for n in range(D_FF // FF_SLAB):
            units = slice(n * FF_SLAB, (n + 1) * FF_SLAB)
            hg = _dot(x, wg_ref[:, units]) + b1g_ref[0, :, units]
            hl = _dot(x, wl_ref[:, units]) + b1l_ref[0, :, units]
            glu = jnp.minimum(hg, SWIGLU_LIMIT)
            lin = jnp.clip(hl, -SWIGLU_LIMIT, SWIGLU_LIMIT)
            a = glu * jax.nn.sigmoid(SWIGLU_ALPHA * glu) * (lin + 1.0)
            y = y + _dot(a.astype(BF16), w2b_ref[units, :])
        halves = _pack_rows(y)
        y_ref[0] = halves[0]
        y_ref[1] = halves[1]


def _moe_routed(tile_expert, tile_first, n_active, xs, w1, b1g, b1l, w2, b2):
    n_rows = xs.shape[1]
    n_tiles = n_rows // MOE_TILE
    live = lambda i, na: jnp.minimum(i, na[0] - 1)
    rows_spec = pl.BlockSpec((2, MOE_TILE, PACK_W), lambda i, te, tf, na: (0, live(i, na), 0))
    wsp = lambda a: pl.BlockSpec((1,) + a.shape[1:], lambda i, te, tf, na: (te[live(i, na)], 0, 0))
    grid_spec = pltpu.PrefetchScalarGridSpec(
        num_scalar_prefetch=3,
        grid=(n_tiles,),
        in_specs=[rows_spec, wsp(w1), wsp(b1g), wsp(b1l), wsp(w2), wsp(b2)],
        out_specs=rows_spec,
        scratch_shapes=[pltpu.VMEM((D_MODEL, D_FF), BF16), pltpu.VMEM((D_MODEL, D_FF), BF16),
                        pltpu.VMEM((D_FF, D_MODEL), BF16)],
    )
    return pl.pallas_call(
        _moe_kernel,
        grid_spec=grid_spec,
        out_shape=jax.ShapeDtypeStruct(xs.shape, jnp.uint32),
        compiler_params=_params(("arbitrary",)),
        name="moe_routed",
    )(tile_expert, tile_first, n_active, xs, w1, b1g, b1l, w2, b2)


def _final_kernel(x1_ref, yg_ref, w4_ref, gt_ref, gf_ref, *rest):
    o_ref = rest[-1]
    w4 = w4_ref[...]
    cols = []
    for j in range(2):
        lo_acc = hi_acc = None
        for k in range(TOP_K):
            lo, hi = _unpack_words(yg_ref[j, k])
            wk = w4[:, k:k + 1]
            lo_acc = wk * lo if lo_acc is None else lo_acc + wk * lo
            hi_acc = wk * hi if hi_acc is None else hi_acc + wk * hi
        cols += [lo_acc, hi_acc]
    moe = jnp.concatenate(cols, axis=1)
    x2 = x1_ref[...] + gt_ref[...] * moe
    o_ref[...] = _rms(x2, gf_ref[...])


def _final(x1, yg, w4, gt, gf, tm, first_tile, prev_out):
    s = x1.shape[0]
    n_tiles = yg.shape[2] // tm
    row = lambda n: pl.BlockSpec((tm, n), lambda i: (i + first_tile, 0))
    in_specs = [row(D_MODEL), pl.BlockSpec((2, TOP_K, tm, PACK_W), lambda i: (0, 0, i, 0)), row(TOP_K),
                _const_spec(gt.shape), _const_spec(gf.shape)]
    args = [x1, yg, w4, gt, gf]
    aliases = {}
    if prev_out is not None:
        in_specs.append(pl.BlockSpec(memory_space=pl.ANY))
        args.append(prev_out)
        aliases = {len(args) - 1: 0}
    return pl.pallas_call(
        _final_kernel,
        grid=(n_tiles,),
        in_specs=in_specs,
        out_specs=row(D_MODEL),
        out_shape=jax.ShapeDtypeStruct((s, D_MODEL), F32),
        input_output_aliases=aliases,
        compiler_params=_params(("arbitrary",)),
        name="combine_final_norm",
    )(*args)


def _routing_tables(e4, r4, counts, s):
    n_rows = TOP_K * s + N_EXPERTS * MOE_TILE
    cnt = counts.reshape(N_EXPERTS).astype(jnp.int32)
    padded = ((cnt + MOE_TILE - 1) // MOE_TILE) * MOE_TILE
    ends = jnp.cumsum(padded)
    starts = ends - padded
    pos = r4
    for e in range(N_EXPERTS):
        pos = pos + jnp.where(e4 == e, starts[e], 0)
    tile_start = jnp.arange(n_rows // MOE_TILE, dtype=jnp.int32) * MOE_TILE
    tile_expert = jnp.minimum(jnp.sum(tile_start[:, None] >= ends[None, :], axis=1), N_EXPERTS - 1).astype(jnp.int32)
    tile_first = jnp.concatenate([jnp.ones((1,), jnp.int32), (tile_expert[1:] != tile_expert[:-1]).astype(jnp.int32)])
    n_active = (ends[-1:] // MOE_TILE).astype(jnp.int32)
    return pos, tile_expert, tile_first, n_active, n_rows


def kernel(x, c, w_ada, b_ada, g_mix, w_in, b_forget, sinks, rel_bias, w_proj_a, w_proj_b, w_out, g_ffn,
           w_router, b_router, w_e1, b_e1, w_e2, b_e2, g_final):
    b, s, d = x.shape
    assert b == 1 and d == D_MODEL and w_ada.shape[0] == 1
    assert s <= 256 * RANK_RADIX
    x2 = x.reshape(s, d)
    tm = min(512, s)

    mod = _ada(c.reshape(d, 1), w_ada[0], b_ada)
    sh_m, sc_m, gt_m, sh_f, sc_f, gt_f = [mod[:, k * d:(k + 1) * d] for k in range(N_MOD)]

    w = w_in[0]
    o_ka, o_va, o_b = SWA_Q, SWA_Q + SWA_KV, SWA_Q + 2 * SWA_KV
    o_f = o_b + 3 * FOX_W
    o_g = o_f + FOX_HEADS
    dup = lambda m: jnp.concatenate([m[:, :HEAD_DIM], m[:, :HEAD_DIM], m[:, HEAD_DIM:], m[:, HEAD_DIM:]], axis=1)
    wa = jnp.concatenate([w[:, :SWA_Q], dup(w[:, o_ka:o_va])], axis=1).astype(BF16)
    pad = jnp.zeros((d, LANES - FOX_HEADS), F32)
    wb = jnp.concatenate([w[:, o_b:o_b + 2 * FOX_W], w[:, o_f:o_g], pad], axis=1).astype(BF16)
    wvt = jnp.concatenate([w[:, o_b + 2 * FOX_W:o_f], w[:, o_va:o_b]], axis=1).T.astype(BF16)
    wg = w[:, o_g:].astype(BF16)
    qa, ka2, vat, qb, kb, vbt, sga, sgb, ca, qn2, kn2, c_first, c_last = _inproj(
        x2, g_mix, sh_m, sc_m, wa, wb, wvt, wg, b_forget, tm)

    ya = _swa(rel_bias.reshape(-1), sinks[0], jnp.asarray(_t5_buckets_np().T), qa, ka2, vat)
    jlo, base = _fox_schedule(c_first, c_last, qn2, kn2)
    yb = _fox(jlo, base, qb, kb, ca, vbt)

    x1, u2p, e4, r4, w4, counts = _outproj(
        x2, ya, yb, sga, sgb, w_proj_a[0].astype(BF16), w_proj_b[0].astype(BF16), w_out[0].astype(BF16),
        gt_m, g_ffn, sh_f, sc_f, w_router[0].astype(BF16), b_router, tm)

    pos, tile_expert, tile_first, n_active, n_rows = _routing_tables(e4, r4, counts, s)
    pos2 = jnp.concatenate([pos, pos + n_rows], axis=1)
    w1p, w2p = _sc_pack_weights(w_e1[0].reshape(N_EXPERTS * d, 2 * D_FF), w_e2[0].reshape(N_EXPERTS * D_FF, d))
    pos2, w1p, w2p = lax.optimization_barrier((pos2, w1p, w2p))
    xs = _sc_dispatch(u2p.reshape(2 * s, PACK_W), pos2, 2 * n_rows).reshape(2, n_rows, PACK_W)

    b1 = b_e1[0].reshape(N_EXPERTS, D_FF, 2)
    b1g = b1[:, None, :, 0]
    b1l = b1[:, None, :, 1]
    ys = _moe_routed(tile_expert, tile_first, n_active, xs, w1p.reshape(N_EXPERTS, d, D_FF), b1g, b1l,
                     w2p.reshape(N_EXPERTS, D_FF, d // 2), b_e2[0][:, None, :])

    gather_idx = pos2.reshape(TOP_K, 2, s).transpose(1, 0, 2)
    sc_rows = s // COMBINE_CHUNKS
    out = None
    for ci in range(COMBINE_CHUNKS):
        idx = gather_idx[:, :, ci * sc_rows:(ci + 1) * sc_rows].reshape(1, -1)
        yg = _sc_gather(ys.reshape(2 * n_rows, PACK_W), idx).reshape(2, TOP_K, sc_rows, PACK_W)
        out = _final(x1, yg, w4, gt_f, g_final.reshape(1, d), tm, ci * (sc_rows // tm), out)
    return out.reshape(b, s, d)
```

```python
import dataclasses
import functools
import math

import numpy as np
import jax
import jax.numpy as jnp
from jax import lax
from jax.experimental import pallas as pl
from jax.experimental.pallas import tpu as pltpu
from jax.experimental.pallas import tpu_sc as plsc

D_MODEL = 1024
HEAD_DIM = 64
SWA_HEADS = 8
SWA_KV_HEADS = 2
WINDOW = 128
FOX_HEADS = 8
BLOCK = 128
REL_BUCKETS = 32
REL_MAX_DIST = WINDOW
N_EXPERTS = 32
TOP_K = 4
D_FF = D_MODEL
SWIGLU_LIMIT = 7.0
SWIGLU_ALPHA = 1.702
RMS_EPS = 1e-5
N_MOD = 6

SWA_Q = SWA_HEADS * HEAD_DIM
SWA_KV = SWA_KV_HEADS * HEAD_DIM
FOX_W = FOX_HEADS * HEAD_DIM
LANES = 128
N_PAIRS = FOX_HEADS // 2
NEG_BIG = -1e30
LOG2E = math.log2(math.e)
FOX_TILE = 256
SWA_BLOCKS = 4
FOX_QSUB = 2
FOX_CHUNK = 4
FOX_CHUNK_ALONE = 8
N_SPLIT = 3
SKIP_LOG2 = 127.0
NORM_SLACK = 1.01
PACK_W = 256
MOE_TILE = 512
SC_WINDOW = 128
RANK_RADIX = 128
FF_SLAB = 512
COMBINE_CHUNKS = 2
SC_LANES = 16
SC_PACK_ROWS = 8
VMEM_LIMIT = 56 * 1024 * 1024

F32 = jnp.float32
BF16 = jnp.bfloat16
HIGHEST = lax.Precision.HIGHEST


def _dot(a, b):
    return jnp.dot(a, b, preferred_element_type=F32)


def _dot_nt(a, b, precision=None):
    return lax.dot_general(a, b, (((1,), (1,)), ((), ())), preferred_element_type=F32, precision=precision)


def _const_spec(shape):
    nd = len(shape)
    return pl.BlockSpec(shape, lambda *_: (0,) * nd)


def _params(sem):
    return pltpu.CompilerParams(dimension_semantics=sem, vmem_limit_bytes=VMEM_LIMIT)


def _ada_kernel(c_ref, w_ref, b_ref, o_ref):
    c = c_ref[...]
    act = c * jax.nn.sigmoid(c)
    o_ref[...] = jnp.sum(act * w_ref[...], axis=0, keepdims=True) + b_ref[...]


def _ada(c_col, w_ada, b_ada):
    n = w_ada.shape[1]
    tn = 1024
    return pl.pallas_call(
        _ada_kernel,
        grid=(n // tn,),
        in_specs=[_const_spec((D_MODEL, 1)),
                  pl.BlockSpec((D_MODEL, tn), lambda j: (0, j)),
                  pl.BlockSpec((1, tn), lambda j: (0, j))],
        out_specs=pl.BlockSpec((1, tn), lambda j: (0, j)),
        out_shape=jax.ShapeDtypeStruct((1, n), F32),
        compiler_params=_params(("arbitrary",)),
        name="ada_mod",
    )(c_col, w_ada, b_ada)


def _split_bf16(v):
    parts = []
    for _ in range(N_SPLIT):
        p = v.astype(BF16)
        v = v - p.astype(F32)
        parts.append(p)
    return parts


def _inproj_kernel(x_ref, g_ref, sh_ref, sc_ref, wa_ref, wb_ref, wvt_ref, wg_ref, bf_ref, hind_ref, place_ref,
                   qa_ref, kva_ref, vat_ref, qb_ref, kb_ref, vbt_ref, sga_ref, sgb_ref, ca_ref, qn_ref, kn_ref,
                   cf_ref, cl_ref, dg_ref, carry_ref):
    i = pl.program_id(0)
    tm = x_ref.shape[0]
    t = FOX_TILE
    nsub = tm // t

    @pl.when(i == 0)
    def _():
        carry_ref[...] = jnp.zeros_like(carry_ref)

    xf = x_ref[...]
    ms = jnp.mean(xf * xf, axis=-1, keepdims=True)
    y = xf * lax.rsqrt(ms + RMS_EPS) * g_ref[...]
    u = y * (1.0 + sc_ref[...]) + sh_ref[...]
    ub = u.astype(BF16)

    za = _dot(ub, wa_ref[...])
    qa_ref[...] = (za[:, :SWA_Q] * (HEAD_DIM ** -0.5 * LOG2E)).astype(BF16)
    kva_ref[...] = za[:, SWA_Q:].astype(BF16)

    zb = _dot(ub, wb_ref[...])
    qb = (zb[:, :FOX_W] * (HEAD_DIM ** -0.5 * LOG2E)).astype(BF16)
    kb = zb[:, FOX_W:2 * FOX_W].astype(BF16)
    qb_ref[...] = qb
    kb_ref[...] = kb
    vt = _dot_nt(wvt_ref[...], ub).astype(BF16)
    vbt_ref[...] = vt[:FOX_W]
    vat_ref[...] = vt[FOX_W:]

    def tile_norm_max(z, o_ref):
        zf = z.astype(F32)
        n2 = _dot((zf * zf).astype(BF16), hind_ref[...])
        for sb in range(nsub):
            o_ref[sb] = jnp.max(n2[sb * t:(sb + 1) * t], axis=0, keepdims=True)

    tile_norm_max(qb, qn_ref)
    tile_norm_max(kb, kn_ref)
    diag = _dot((qb.astype(F32) * kb.astype(F32)).astype(BF16), hind_ref[...])
    for sb in range(nsub):
        dg_ref[sb] = jnp.min(diag[sb * t:(sb + 1) * t], axis=0, keepdims=True)

    zg = _dot(ub, wg_ref[...])
    sg = jax.nn.sigmoid(zg)
    sga_ref[...] = sg[:, :D_MODEL].astype(BF16)
    sgb_ref[...] = sg[:, D_MODEL:].astype(BF16)

    fb = zb[:, 2 * FOX_W:2 * FOX_W + FOX_HEADS] + bf_ref[...]
    lf = jnp.minimum(fb, 0.0) - jnp.log1p(jnp.exp(-jnp.abs(fb)))
    r = lax.broadcasted_iota(jnp.int32, (t, t), 0)
    cc = lax.broadcasted_iota(jnp.int32, (t, t), 1)
    lower = (cc <= r).astype(BF16)
    carry = carry_ref[...]
    for sb in range(nsub):
        rows = slice(sb * t, (sb + 1) * t)
        local = sum(_dot(lower, p) for p in _split_bf16(lf[rows]))
        cf_ref[sb] = (local[0:1, :] + carry) * LOG2E
        carry = carry + local[t - 1:t, :]
        cl_ref[sb] = carry * LOG2E
        aug = sum(_dot(p, place_ref[k]) for k, p in enumerate(_split_bf16(local * LOG2E)))
        ca_ref[rows, :] = aug.astype(BF16)
    carry_ref[...] = carry


def _inproj(x2, g, sh, sc, wa, wb, wvt, wg, bfor, tm):
    s = x2.shape[0]
    nsub = tm // FOX_TILE
    row = lambda n: pl.BlockSpec((tm, n), lambda i: (i, 0))
    hind = np.zeros((FOX_W, FOX_HEADS), np.float32)
    hind[np.arange(FOX_W), np.arange(FOX_W) // HEAD_DIM] = 1.0
    place = np.zeros((N_SPLIT, FOX_HEADS, N_PAIRS * LANES), np.float32)
    for k in range(N_SPLIT):
        for h in range(FOX_HEADS):
            place[k, h, (h // 2) * LANES + N_SPLIT * (h % 2) + k] = 1.0
    hind = jnp.asarray(hind, BF16)
    place = jnp.asarray(place, BF16)
    out_shape = [
        jax.ShapeDtypeStruct((s, SWA_Q), BF16),
        jax.ShapeDtypeStruct((s, 2 * LANES), BF16),
        jax.ShapeDtypeStruct((SWA_KV, s), BF16),
        jax.ShapeDtypeStruct((s, FOX_W), BF16),
        jax.ShapeDtypeStruct((s, FOX_W), BF16),
        jax.ShapeDtypeStruct((FOX_W, s), BF16),
        jax.ShapeDtypeStruct((s, D_MODEL), BF16),
        jax.ShapeDtypeStruct((s, D_MODEL), BF16),
        jax.ShapeDtypeStruct((s, N_PAIRS * LANES), BF16),
        jax.ShapeDtypeStruct((s // FOX_TILE, 1, FOX_HEADS), F32),
        jax.ShapeDtypeStruct((s // FOX_TILE, 1, FOX_HEADS), F32),
        jax.ShapeDtypeStruct((s // FOX_TILE, 1, FOX_HEADS), F32),
        jax.ShapeDtypeStruct((s // FOX_TILE, 1, FOX_HEADS), F32),
        jax.ShapeDtypeStruct((s // FOX_TILE, 1, FOX_HEADS), F32),
    ]
    stat = pl.BlockSpec((nsub, 1, FOX_HEADS), lambda i: (i, 0, 0))
    out_specs = [row(SWA_Q), row(2 * LANES), pl.BlockSpec((SWA_KV, tm), lambda i: (0, i)),
                 row(FOX_W), row(FOX_W), pl.BlockSpec((FOX_W, tm), lambda i: (0, i)),
                 row(D_MODEL), row(D_MODEL), row(N_PAIRS * LANES), stat, stat, stat, stat, stat]
    consts = [g, sh, sc, wa, wb, wvt, wg, bfor, hind, place]
    return pl.pallas_call(
        _inproj_kernel,
        grid=(s // tm,),
        in_specs=[row(D_MODEL)] + [_const_spec(a.shape) for a in consts],
        out_specs=out_specs,
        out_shape=out_shape,
        scratch_shapes=[pltpu.VMEM((1, FOX_HEADS), F32)],
        compiler_params=_params(("arbitrary",)),
        name="in_proj",
    )(x2, *consts)


def _t5_buckets_np():
    qi = np.arange(BLOCK)[:, None]
    kj = np.arange(2 * BLOCK)[None, :]
    dist = BLOCK + qi - kj
    n = np.maximum(dist, 0)
    max_exact = REL_BUCKETS // 2
    nf = np.maximum(n, 1).astype(np.float32)
    large = max_exact + (np.log(nf / np.float32(max_exact)) / np.float32(math.log(REL_MAX_DIST / max_exact))
                         * np.float32(REL_BUCKETS - max_exact)).astype(np.int32)
    large = np.minimum(large, REL_BUCKETS - 1)
    bucket = np.where(n < max_exact, n, large).astype(np.int32)
    band = (dist >= 0) & (dist < WINDOW)
    return np.where(band, bucket, -1).astype(np.int32)


def _swa_kernel(rel_ref, sink_ref, bkt_ref, q_ref, kc_ref, kp_ref, vc_ref, vp_ref, o_ref, bias_ref):
    n = pl.program_id(0)

    @pl.when(n == 0)
    def _():
        bkt = bkt_ref[...]
        prev = lax.broadcasted_iota(jnp.int32, bkt.shape, 0) < BLOCK
        for h in range(SWA_HEADS):
            acc = jnp.full(bkt.shape, NEG_BIG, F32)
            for b in range(REL_BUCKETS):
                acc = jnp.where(bkt == b, rel_ref[b * SWA_HEADS + h] * LOG2E, acc)
            cols = slice((h % 2) * BLOCK, (h % 2 + 1) * BLOCK)
            bias_ref[0, h // 2, :, cols] = acc
            bias_ref[1, h // 2, :, cols] = jnp.where(prev, NEG_BIG, acc)

    lane = lax.broadcasted_iota(jnp.int32, (BLOCK, LANES), 1)
    col2 = lax.broadcasted_iota(jnp.int32, (1, 2 * BLOCK), 1)
    k_all = jnp.concatenate([kp_ref[...], kc_ref[...]], axis=0)
    v_all = jnp.concatenate([vp_ref[...], vc_ref[...]], axis=1)
    work = [(b, p) for b in range(SWA_BLOCKS) for p in range(SWA_HEADS // 2)]
    scores = []
    for b, p in work:
        qp = q_ref[b * BLOCK:(b + 1) * BLOCK, p * LANES:(p + 1) * LANES]
        zero = jnp.zeros_like(qp)
        qs = jnp.concatenate([jnp.where(lane < HEAD_DIM, qp, zero), jnp.where(lane >= HEAD_DIM, qp, zero)], axis=0)
        g = p // 2
        scores.append(_dot_nt(k_all[b * BLOCK:(b + 2) * BLOCK, g * LANES:(g + 1) * LANES], qs))
    weights = []
    for (b, p), s in zip(work, scores):
        first = jnp.where(n == 0, 1, 0) if b == 0 else 0
        s = s + bias_ref[first, p]
        sink = jnp.where(col2 < BLOCK, sink_ref[2 * p], sink_ref[2 * p + 1]) * LOG2E
        m = jnp.maximum(jnp.max(s, axis=0, keepdims=True), sink)
        e = jnp.exp2(s - m)
        denom = jnp.sum(e, axis=0, keepdims=True) + jnp.exp2(sink - m)
        weights.append((e.astype(BF16), denom))
    outs = [[] for _ in range(SWA_BLOCKS)]
    for (b, p), (e, denom) in zip(work, weights):
        g = p // 2
        o = _dot(v_all[g * HEAD_DIM:(g + 1) * HEAD_DIM, b * BLOCK:(b + 2) * BLOCK], e) / denom
        outs[b] += [o[:, :BLOCK], o[:, BLOCK:]]
    for b in range(SWA_BLOCKS):
        o_ref[b * BLOCK:(b + 1) * BLOCK, :] = jnp.concatenate(outs[b], axis=0).T.astype(BF16)


def _swa(rel_flat, sinks, bkt_t, qa, ka2, vat):
    s = qa.shape[0]
    rows = SWA_BLOCKS * BLOCK
    smem = pl.BlockSpec(memory_space=pltpu.SMEM)
    prev = lambda n: jnp.maximum(n * SWA_BLOCKS - 1, 0)
    return pl.pallas_call(
        _swa_kernel,
        grid=(s // rows,),
        in_specs=[smem, smem, _const_spec(bkt_t.shape),
                  pl.BlockSpec((rows, SWA_Q), lambda n: (n, 0)),
                  pl.BlockSpec((rows, 2 * LANES), lambda n: (n, 0)),
                  pl.BlockSpec((BLOCK, 2 * LANES), lambda n: (prev(n), 0)),
                  pl.BlockSpec((SWA_KV, rows), lambda n: (0, n)),
                  pl.BlockSpec((SWA_KV, BLOCK), lambda n: (0, prev(n)))],
        out_specs=pl.BlockSpec((rows, SWA_Q), lambda n: (n, 0)),
        out_shape=jax.ShapeDtypeStruct((s, SWA_Q), BF16),
        scratch_shapes=[pltpu.VMEM((2, SWA_HEADS // 2, 2 * BLOCK, 2 * BLOCK), F32)],
        compiler_params=_params(("arbitrary",)),
        name="swa_attn",
    )(rel_flat, sinks, bkt_t, qa, ka2, ka2, vat, vat)


def _fox_kernel(jlo_ref, base_ref, q_ref, k_ref, ca_ref, vt_ref, o_ref):
    p = pl.program_id(0)
    i = pl.program_id(1)
    t = FOX_TILE
    tq = FOX_QSUB * t
    first_diag = i * FOX_QSUB
    lane = lax.broadcasted_iota(jnp.int32, (tq, LANES), 1)
    q = q_ref[...]
    wq = []
    for hh in range(2):
        in_head = (lane >= hh * HEAD_DIM) & (lane < (hh + 1) * HEAD_DIM)
        qm = jnp.where(in_head, q, jnp.zeros_like(q))
        sel = (lane >= N_SPLIT * hh) & (lane < N_SPLIT * (hh + 1))
        aug = jnp.where(sel, -1.0, 0.0).astype(BF16)
        wq.append(jnp.concatenate([qm, aug], axis=1))
    key = lax.broadcasted_iota(jnp.int32, (t, tq), 0)
    qry = lax.broadcasted_iota(jnp.int32, (t, tq), 1)
    causal = [key + d * t <= qry for d in range(FOX_QSUB)]

    def step(j, carry, heads, nsub, diagonal=False):
        start = pl.multiple_of(j * t, t)
        rows = nsub * t
        lhs = jnp.concatenate([k_ref[pl.ds(start, rows), :], ca_ref[pl.ds(start, rows), :]], axis=1)
        scores = [_dot_nt(lhs, wq[hh]) for hh in heads]
        mid = []
        for hh, s, (m, l, acc) in zip(heads, scores, carry):
            h = 2 * p + hh
            parts = [s[k * t:(k + 1) * t] for k in range(nsub)]
            if diagonal:
                for d in range(FOX_QSUB):
                    k = nsub - FOX_QSUB + d
                    parts[k] = jnp.where(causal[d], parts[k], NEG_BIG)
            djs = [base_ref[h, first_diag] - base_ref[h, j + k] for k in range(nsub)]
            m_new = m
            for part, dj in zip(parts, djs):
                m_new = jnp.maximum(m_new, jnp.max(part, axis=0, keepdims=True) + dj)
            alpha = jnp.exp2(m - m_new)
            es = [jnp.exp2(part + (dj - m_new)) for part, dj in zip(parts, djs)]
            l = alpha * l
            for e in es:
                l = l + jnp.sum(e, axis=0, keepdims=True)
            e_all = es[0] if nsub == 1 else jnp.concatenate(es, axis=0)
            mid.append((m_new, l, alpha, acc, e_all.astype(BF16)))
        new = []
        for hh, (m_new, l, alpha, acc, e_all) in zip(heads, mid):
            vt = vt_ref[hh * HEAD_DIM:(hh + 1) * HEAD_DIM, pl.ds(start, rows)]
            new.append((m_new, l, alpha * acc + _dot(vt, e_all)))
        return tuple(new)

    def run_alone(lo, hi, carry, heads):
        n_long = (hi - lo) // FOX_CHUNK_ALONE
        carry = lax.fori_loop(0, n_long, lambda n, c: step(lo + FOX_CHUNK_ALONE * n, c, heads, FOX_CHUNK_ALONE), carry)
        mid = lo + FOX_CHUNK_ALONE * n_long
        n_full = (hi - mid) // FOX_CHUNK
        carry = lax.fori_loop(0, n_full, lambda n, c: step(mid + FOX_CHUNK * n, c, heads, FOX_CHUNK), carry)
        rest = mid + FOX_CHUNK * n_full
        tails = [lambda c: c] + [functools.partial(lambda c, k: step(rest, c, heads, k), k=k)
                                 for k in range(1, FOX_CHUNK)]
        return lax.switch(hi - rest, tails, carry)

    def run_to_diagonal(lo, carry, heads):
        end = first_diag + FOX_QSUB
        count = end - lo
        last = (count - FOX_QSUB) % FOX_CHUNK + FOX_QSUB
        n_full = (count - last) // FOX_CHUNK
        carry = lax.fori_loop(0, n_full, lambda n, c: step(lo + FOX_CHUNK * n, c, heads, FOX_CHUNK), carry)
        tails = [functools.partial(lambda c, k: step(end - k, c, heads, k, diagonal=True), k=k)
                 for k in range(FOX_QSUB, FOX_QSUB + FOX_CHUNK)]
        return lax.switch(last - FOX_QSUB, tails, carry)

    lo0 = jlo_ref[2 * p, i]
    lo1 = jlo_ref[2 * p + 1, i]
    lo_both = jnp.maximum(lo0, lo1)
    init = (jnp.full((1, tq), NEG_BIG, F32), jnp.zeros((1, tq), F32), jnp.zeros((HEAD_DIM, tq), F32))
    (c0,) = run_alone(lo0, lo_both, (init,), (0,))
    (c1,) = run_alone(lo1, lo_both, (init,), (1,))
    carry = run_to_diagonal(lo_both, (c0, c1), (0, 1))
    ot = jnp.concatenate([carry[0][2] / carry[0][1], carry[1][2] / carry[1][1]], axis=0)
    o_ref[...] = ot.T.astype(BF16)


def _fox_schedule(c_first, c_last, qn2, kn2, diag_min):
    nt = qn2.shape[0]
    t = FOX_TILE
    bq = jnp.sqrt(qn2.reshape(nt, FOX_HEADS)) * NORM_SLACK
    bk = jnp.sqrt(kn2.reshape(nt, FOX_HEADS)) * NORM_SLACK
    c_first = c_first.reshape(nt, FOX_HEADS)
    c_last = c_last.reshape(nt, FOX_HEADS)
    diag_lo = diag_min.reshape(nt, FOX_HEADS) - (NORM_SLACK - 1.0) * bq * bk
    upper = bq[:, None, :] * bk[None, :, :] - diag_lo[:, None, :] + c_first[:, None, :] - c_last[None, :, :]
    ii = jnp.arange(nt)[:, None, None]
    jj = jnp.arange(nt)[None, :, None]
    needed = (jj <= ii) & ((upper >= -SKIP_LOG2) | (jj == ii))
    jlo = jnp.min(jnp.where(needed, jj, nt), axis=1)
    jlo = jnp.min(jlo.reshape(nt // FOX_QSUB, FOX_QSUB, FOX_HEADS), axis=1).T.astype(jnp.int32)
    base = jnp.concatenate([jnp.zeros((1, FOX_HEADS), F32), c_last[:-1]], axis=0).T
    return jlo, base


def _fox(jlo, base, qb, kb, ca, vbt):
    s = qb.shape[0]
    t = FOX_QSUB * FOX_TILE
    grid_spec = pltpu.PrefetchScalarGridSpec(
        num_scalar_prefetch=2,
        grid=(N_PAIRS, s // t),
        in_specs=[pl.BlockSpec((t, LANES), lambda p, i, *_: (i, p)),
                  pl.BlockSpec((s, LANES), lambda p, i, *_: (0, p)),
                  pl.BlockSpec((s, LANES), lambda p, i, *_: (0, p)),
                  pl.BlockSpec((LANES, s), lambda p, i, *_: (p, 0))],
        out_specs=pl.BlockSpec((t, LANES), lambda p, i, *_: (i, p)),
    )
    return pl.pallas_call(
        _fox_kernel,
        grid_spec=grid_spec,
        out_shape=jax.ShapeDtypeStruct((s, FOX_W), BF16),
        compiler_params=_params(("arbitrary", "arbitrary")),
        name="fox_attn",
    )(jlo, base, qb, kb, ca, vbt)


def _rms(x, g):
    return x * lax.rsqrt(jnp.mean(x * x, axis=-1, keepdims=True) + RMS_EPS) * g


def _pack_rows(v):
    halves = []
    for j in range(2):
        lo = v[:, (2 * j) * PACK_W:(2 * j + 1) * PACK_W].astype(BF16).astype(F32)
        hi = v[:, (2 * j + 1) * PACK_W:(2 * j + 2) * PACK_W].astype(BF16).astype(F32)
        lo_bits = lax.bitcast_convert_type(lo, jnp.uint32)
        hi_bits = lax.bitcast_convert_type(hi, jnp.uint32)
        halves.append(hi_bits | (lo_bits >> 16))
    return halves


def _unpack_words(w):
    lo = lax.bitcast_convert_type(w << 16, F32)
    hi = lax.bitcast_convert_type(w & jnp.uint32(0xFFFF0000), F32)
    return lo, hi


def _outproj_kernel(x_ref, ya_ref, yb_ref, sga_ref, sgb_ref, pa_ref, pb_ref, wo_ref, gt_ref, g_ref, sh_ref, sc_ref,
                    wr_ref, br_ref, x1_ref, u2p_ref, e4_ref, r4_ref, w4_ref, cnt_ref, carry_ref):
    i = pl.program_id(0)
    tm = x_ref.shape[0]

    @pl.when(i == 0)
    def _():
        carry_ref[...] = jnp.zeros_like(carry_ref)

    merged = (sga_ref[...].astype(F32) * _dot(ya_ref[...], pa_ref[...])
              + sgb_ref[...].astype(F32) * _dot(yb_ref[...], pb_ref[...]))
    x1 = x_ref[...] + gt_ref[...] * _dot(merged.astype(BF16), wo_ref[...])
    x1_ref[...] = x1
    u2 = _rms(x1, g_ref[...]) * (1.0 + sc_ref[...]) + sh_ref[...]
    halves = _pack_rows(u2)
    u2p_ref[0] = halves[0]
    u2p_ref[1] = halves[1]

    logits = _dot(u2.astype(BF16), wr_ref[...]) + br_ref[...]
    eidx = lax.broadcasted_iota(jnp.int32, logits.shape, 1)
    work = logits
    sel = jnp.zeros(logits.shape, jnp.bool_)
    picks, vals = [], []
    for k in range(TOP_K):
        m = jnp.max(work, axis=-1, keepdims=True)
        first = jnp.min(jnp.where(work == m, eidx, N_EXPERTS), axis=-1, keepdims=True)
        hit = eidx == first
        sel = sel | hit
        work = jnp.where(hit, -jnp.inf, work)
        picks.append(first)
        vals.append(m)
    exps = [jnp.exp(v - vals[0]) for v in vals]
    denom = exps[0] + exps[1] + exps[2] + exps[3]

    r = lax.broadcasted_iota(jnp.int32, (tm, tm), 0)
    cc = lax.broadcasted_iota(jnp.int32, (tm, tm), 1)
    before = (cc < r).astype(BF16)
    chosen = sel.astype(BF16)
    rank = _dot(before, chosen) + carry_ref[...]
    cnt = carry_ref[...] + jnp.sum(chosen.astype(F32), axis=0, keepdims=True)
    carry_ref[...] = cnt
    cnt_ref[...] = cnt
    slot = lax.broadcasted_iota(jnp.int32, (tm, 4 * TOP_K), 1)
    cols = jnp.zeros((tm, 4 * TOP_K), F32)
    for k in range(TOP_K):
        rk = jnp.sum(jnp.where(eidx == picks[k], rank, 0.0), axis=-1, keepdims=True)
        hi = jnp.floor(rk * (1.0 / RANK_RADIX))
        for j, col in ((k, picks[k].astype(F32)), (TOP_K + k, hi), (2 * TOP_K + k, rk - hi * RANK_RADIX)):
            cols = jnp.where(slot == j, col, cols)
        w4_ref[:, k:k + 1] = exps[k] / denom
    eye = (lax.broadcasted_iota(jnp.int32, (4 * TOP_K, 4 * TOP_K), 0)
           == lax.broadcasted_iota(jnp.int32, (4 * TOP_K, 4 * TOP_K), 1)).astype(BF16)
    flipped = _dot_nt(eye, cols.astype(BF16))
    e4_ref[...] = flipped[:TOP_K].astype(jnp.int32)
    r4_ref[...] = (flipped[TOP_K:2 * TOP_K] * RANK_RADIX + flipped[2 * TOP_K:3 * TOP_K]).astype(jnp.int32)


def _outproj(x2, ya, yb, sga, sgb, pa, pb, wo, gt, g, sh, sc, wr, br, tm):
    s = x2.shape[0]
    row = lambda n: pl.BlockSpec((tm, n), lambda i: (i, 0))
    consts = [pa, pb, wo, gt, g, sh, sc, wr, br]
    return pl.pallas_call(
        _outproj_kernel,
        grid=(s // tm,),
        in_specs=[row(D_MODEL), row(SWA_Q), row(FOX_W), row(D_MODEL), row(D_MODEL)] + [_const_spec(a.shape) for a in consts],
        out_specs=[row(D_MODEL), pl.BlockSpec((2, tm, PACK_W), lambda i: (0, i, 0)),
                   pl.BlockSpec((TOP_K, tm), lambda i: (0, i)), pl.BlockSpec((TOP_K, tm), lambda i: (0, i)), row(TOP_K),
                   _const_spec((1, N_EXPERTS))],
        out_shape=[jax.ShapeDtypeStruct((s, D_MODEL), F32),
                   jax.ShapeDtypeStruct((2, s, PACK_W), jnp.uint32),
                   jax.ShapeDtypeStruct((TOP_K, s), jnp.int32),
                   jax.ShapeDtypeStruct((TOP_K, s), jnp.int32),
                   jax.ShapeDtypeStruct((s, TOP_K), F32),
                   jax.ShapeDtypeStruct((1, N_EXPERTS), F32)],
        scratch_shapes=[pltpu.VMEM((1, N_EXPERTS), F32)],
        compiler_params=_params(("arbitrary",)),
        name="out_proj_router",
    )(x2, ya, yb, sga, sgb, *consts)


def _sc_mesh():
    return plsc.VectorSubcoreMesh(core_axis_name="core", subcore_axis_name="subcore")


def _sc_dispatch(rows, idx, n_out):
    n, width = rows.shape

    @functools.partial(pl.kernel, out_type=jax.ShapeDtypeStruct((n_out, width), rows.dtype), mesh=_sc_mesh(),
                       scratch_types=[])
    def dispatch(x_hbm, i_hbm, o_hbm):
        def body(x_vmem, i_vmem):
            for k in range(TOP_K):
                pltpu.sync_copy(x_vmem, o_hbm.at[i_vmem.at[k]])

        pltpu.emit_pipeline(
            body, grid=(n // SC_WINDOW,),
            in_specs=[pl.BlockSpec((SC_WINDOW, width), lambda i: (i, 0)),
                      pl.BlockSpec((TOP_K, SC_WINDOW), lambda i: (0, i))],
            out_specs=[], core_axis_name=("core", "subcore"), dimension_semantics=(pltpu.PARALLEL,),
        )(x_hbm, i_hbm)

    return dispatch(rows, idx)


def _sc_gather(table, idx):
    n = idx.shape[1]
    width = table.shape[1]

    @functools.partial(pl.kernel, out_type=jax.ShapeDtypeStruct((n, width), table.dtype), mesh=_sc_mesh(),
                       scratch_types=[])
    def gather(t_hbm, i_hbm, o_hbm):
        def body(i_vmem, o_vmem):
            pltpu.sync_copy(t_hbm.at[i_vmem.at[0]], o_vmem)

        pltpu.emit_pipeline(
            body, grid=(n // SC_WINDOW,),
            in_specs=[pl.BlockSpec((1, SC_WINDOW), lambda i: (0, i))],
            out_specs=[pl.BlockSpec((SC_WINDOW, width), lambda i: (i, 0))],
            core_axis_name=("core", "subcore"), dimension_semantics=(pltpu.PARALLEL,),
        )(i_hbm, o_hbm)

    return gather(table, idx)


def _sc_pack_weights(w1, w2):
    r1, r2 = w1.shape[0], w2.shape[0]
    half = D_FF // 2

    def bf16_bits(v):
        u = plsc.bitcast(v, jnp.uint32)
        return (u + jnp.uint32(0x7FFF) + ((u >> 16) & jnp.uint32(1))) >> 16

    cp = pltpu.CompilerParams()
    if "needs_layout_passes" in pltpu.CompilerParams.__dataclass_fields__:
        cp = dataclasses.replace(cp, needs_layout_passes=False)

    n_in = w1.size + w2.size
    cost = pl.CostEstimate(flops=8 * n_in, transcendentals=0, bytes_accessed=6 * n_in)
    @functools.partial(
        pl.kernel, mesh=_sc_mesh(), scratch_types=[], compiler_params=cp, cost_estimate=cost,
        out_type=(jax.ShapeDtypeStruct((r1, D_FF), jnp.uint32), jax.ShapeDtypeStruct((r2, D_MODEL // 2), jnp.uint32)))
    def pack(w1_hbm, w2_hbm, o1_hbm, o2_hbm):
        lanes = lax.iota(jnp.int32, SC_LANES)

        def body1(x_vmem, o_vmem):
            @pl.loop(0, SC_PACK_ROWS)
            def _(r):
                rr = jnp.full((SC_LANES,), r, jnp.int32)

                @pl.loop(0, half, step=SC_LANES)
                def _(c):
                    col = 2 * (c + lanes)
                    g0 = plsc.load_gather(x_vmem, [rr, col])
                    g1 = plsc.load_gather(x_vmem, [rr, col + 2 * half])
                    l0 = plsc.load_gather(x_vmem, [rr, col + 1])
                    l1 = plsc.load_gather(x_vmem, [rr, col + 2 * half + 1])
                    o_vmem[r, pl.ds(c, SC_LANES)] = bf16_bits(g0) | (bf16_bits(g1) << 16)
                    o_vmem[r, pl.ds(half + c, SC_LANES)] = bf16_bits(l0) | (bf16_bits(l1) << 16)

        def body2(x_vmem, o_vmem):
            @pl.loop(0, SC_PACK_ROWS)
            def _(r):
                @pl.loop(0, D_MODEL // 2, step=SC_LANES)
                def _(c):
                    lo = x_vmem[r, pl.ds(c, SC_LANES)]
                    hi = x_vmem[r, pl.ds(D_MODEL // 2 + c, SC_LANES)]
                    o_vmem[r, pl.ds(c, SC_LANES)] = bf16_bits(lo) | (bf16_bits(hi) << 16)

        for body, x_hbm, o_hbm, rows in ((body1, w1_hbm, o1_hbm, r1), (body2, w2_hbm, o2_hbm, r2)):
            pltpu.emit_pipeline(
                body, grid=(rows // SC_PACK_ROWS,),
                in_specs=[pl.BlockSpec((SC_PACK_ROWS, x_hbm.shape[1]), lambda i: (i, 0))],
                out_specs=[pl.BlockSpec((SC_PACK_ROWS, o_hbm.shape[1]), lambda i: (i, 0))],
                core_axis_name=("core", "subcore"), dimension_semantics=(pltpu.PARALLEL,),
            )(x_hbm, o_hbm)

    return pack(w1, w2)


def _moe_kernel(te_ref, first_ref, nact_ref, xs_ref, w1_ref, b1g_ref, b1l_ref, w2_ref, b2_ref, y_ref,
                wg_ref, wl_ref, w2b_ref):
    i = pl.program_id(0)
    live = i < nact_ref[0]

    @pl.when(jnp.logical_and(live, first_ref[i] == 1))
    def _():
        half = D_FF // 2
        for dst, words in ((wg_ref, w1_ref[0, :, :half]), (wl_ref, w1_ref[0, :, half:]), (w2b_ref, w2_ref[0])):
            lo, hi = _unpack_words(words)
            dst[:, :half] = lo.astype(BF16)
            dst[:, half:] = hi.astype(BF16)

    @pl.when(live)
    def _():
        chunks = []
        for j in range(2):
            lo, hi = _unpack_words(xs_ref[j])
            chunks += [lo.astype(BF16), hi.astype(BF16)]
        x = jnp.concatenate(chunks, axis=1)
        y = b2_ref[0]
        for n in range(D_FF // FF_SLAB):
            units = slice(n * FF_SLAB, (n + 1) * FF_SLAB)
            hg = _dot(x, wg_ref[:, units]) + b1g_ref[0, :, units]
            hl = _dot(x, wl_ref[:, units]) + b1l_ref[0, :, units]
            glu = jnp.minimum(hg, SWIGLU_LIMIT)
            lin = jnp.clip(hl, -SWIGLU_LIMIT, SWIGLU_LIMIT)
            a = glu * jax.nn.sigmoid(SWIGLU_ALPHA * glu) * (lin + 1.0)
            y = y + _dot(a.astype(BF16), w2b_ref[units, :])
        halves = _pack_rows(y)
        y_ref[0] = halves[0]
        y_ref[1] = halves[1]


def _moe_routed(tile_expert, tile_first, n_active, xs, w1, b1g, b1l, w2, b2):
    n_rows = xs.shape[1]
    n_tiles = n_rows // MOE_TILE
    live = lambda i, na: jnp.minimum(i, na[0] - 1)
    rows_spec = pl.BlockSpec((2, MOE_TILE, PACK_W), lambda i, te, tf, na: (0, live(i, na), 0))
    wsp = lambda a: pl.BlockSpec((1,) + a.shape[1:], lambda i, te, tf, na: (te[live(i, na)], 0, 0))
    grid_spec = pltpu.PrefetchScalarGridSpec(
        num_scalar_prefetch=3,
        grid=(n_tiles,),
        in_specs=[rows_spec, wsp(w1), wsp(b1g), wsp(b1l), wsp(w2), wsp(b2)],
        out_specs=rows_spec,
        scratch_shapes=[pltpu.VMEM((D_MODEL, D_FF), BF16), pltpu.VMEM((D_MODEL, D_FF), BF16),
                        pltpu.VMEM((D_FF, D_MODEL), BF16)],
    )
    return pl.pallas_call(
        _moe_kernel,
        grid_spec=grid_spec,
        out_shape=jax.ShapeDtypeStruct(xs.shape, jnp.uint32),
        compiler_params=_params(("arbitrary",)),
        name="moe_routed",
    )(tile_expert, tile_first, n_active, xs, w1, b1g, b1l, w2, b2)


def _final_kernel(x1_ref, yg_ref, w4_ref, gt_ref, gf_ref, *rest):
    o_ref = rest[-1]
    w4 = w4_ref[...]
    cols = []
    for j in range(2):
        lo_acc = hi_acc = None
        for k in range(TOP_K):
            lo, hi = _unpack_words(yg_ref[j, k])
            wk = w4[:, k:k + 1]
            lo_acc = wk * lo if lo_acc is None else lo_acc + wk * lo
            hi_acc = wk * hi if hi_acc is None else hi_acc + wk * hi
        cols += [lo_acc, hi_acc]
    moe = jnp.concatenate(cols, axis=1)
    x2 = x1_ref[...] + gt_ref[...] * moe
    o_ref[...] = _rms(x2, gf_ref[...])


def _final(x1, yg, w4, gt, gf, tm, first_tile, prev_out):
    s = x1.shape[0]
    n_tiles = yg.shape[2] // tm
    row = lambda n: pl.BlockSpec((tm, n), lambda i: (i + first_tile, 0))
    in_specs = [row(D_MODEL), pl.BlockSpec((2, TOP_K, tm, PACK_W), lambda i: (0, 0, i, 0)), row(TOP_K),
                _const_spec(gt.shape), _const_spec(gf.shape)]
    args = [x1, yg, w4, gt, gf]
    aliases = {}
    if prev_out is not None:
        in_specs.append(pl.BlockSpec(memory_space=pl.ANY))
        args.append(prev_out)
        aliases = {len(args) - 1: 0}
    return pl.pallas_call(
        _final_kernel,
        grid=(n_tiles,),
        in_specs=in_specs,
        out_specs=row(D_MODEL),
        out_shape=jax.ShapeDtypeStruct((s, D_MODEL), F32),
        input_output_aliases=aliases,
        compiler_params=_params(("arbitrary",)),
        name="combine_final_norm",
    )(*args)


def _routing_tables(e4, r4, counts, s):
    n_rows = TOP_K * s + N_EXPERTS * MOE_TILE
    cnt = counts.reshape(N_EXPERTS).astype(jnp.int32)
    padded = ((cnt + MOE_TILE - 1) // MOE_TILE) * MOE_TILE
    ends = jnp.cumsum(padded)
    starts = ends - padded
    pos = r4
    for e in range(N_EXPERTS):
        pos = pos + jnp.where(e4 == e, starts[e], 0)
    tile_start = jnp.arange(n_rows // MOE_TILE, dtype=jnp.int32) * MOE_TILE
    tile_expert = jnp.minimum(jnp.sum(tile_start[:, None] >= ends[None, :], axis=1), N_EXPERTS - 1).astype(jnp.int32)
    tile_first = jnp.concatenate([jnp.ones((1,), jnp.int32), (tile_expert[1:] != tile_expert[:-1]).astype(jnp.int32)])
    n_active = (ends[-1:] // MOE_TILE).astype(jnp.int32)
    return pos, tile_expert, tile_first, n_active, n_rows


def kernel(x, c, w_ada, b_ada, g_mix, w_in, b_forget, sinks, rel_bias, w_proj_a, w_proj_b, w_out, g_ffn,
           w_router, b_router, w_e1, b_e1, w_e2, b_e2, g_final):
    b, s, d = x.shape
    assert b == 1 and d == D_MODEL and w_ada.shape[0] == 1
    assert s <= 256 * RANK_RADIX
    x2 = x.reshape(s, d)
    tm = min(512, s)

    mod = _ada(c.reshape(d, 1), w_ada[0], b_ada)
    sh_m, sc_m, gt_m, sh_f, sc_f, gt_f = [mod[:, k * d:(k + 1) * d] for k in range(N_MOD)]

    w = w_in[0]
    o_ka, o_va, o_b = SWA_Q, SWA_Q + SWA_KV, SWA_Q + 2 * SWA_KV
    o_f = o_b + 3 * FOX_W
    o_g = o_f + FOX_HEADS
    dup = lambda m: jnp.concatenate([m[:, :HEAD_DIM], m[:, :HEAD_DIM], m[:, HEAD_DIM:], m[:, HEAD_DIM:]], axis=1)
    wa = jnp.concatenate([w[:, :SWA_Q], dup(w[:, o_ka:o_va])], axis=1).astype(BF16)
    pad = jnp.zeros((d, LANES - FOX_HEADS), F32)
    wb = jnp.concatenate([w[:, o_b:o_b + 2 * FOX_W], w[:, o_f:o_g], pad], axis=1).astype(BF16)
    wvt = jnp.concatenate([w[:, o_b + 2 * FOX_W:o_f], w[:, o_va:o_b]], axis=1).T.astype(BF16)
    wg = w[:, o_g:].astype(BF16)
    qa, ka2, vat, qb, kb, vbt, sga, sgb, ca, qn2, kn2, c_first, c_last, diag_min = _inproj(
        x2, g_mix, sh_m, sc_m, wa, wb, wvt, wg, b_forget, tm)

    ya = _swa(rel_bias.reshape(-1), sinks[0], jnp.asarray(_t5_buckets_np().T), qa, ka2, vat)
    jlo, base = _fox_schedule(c_first, c_last, qn2, kn2, diag_min)
    yb = _fox(jlo, base, qb, kb, ca, vbt)

    x1, u2p, e4, r4, w4, counts = _outproj(
        x2, ya, yb, sga, sgb, w_proj_a[0].astype(BF16), w_proj_b[0].astype(BF16), w_out[0].astype(BF16),
        gt_m, g_ffn, sh_f, sc_f, w_router[0].astype(BF16), b_router, tm)

    pos, tile_expert, tile_first, n_active, n_rows = _routing_tables(e4, r4, counts, s)
    pos2 = jnp.concatenate([pos, pos + n_rows], axis=1)
    w1p, w2p = _sc_pack_weights(w_e1[0].reshape(N_EXPERTS * d, 2 * D_FF), w_e2[0].reshape(N_EXPERTS * D_FF, d))
    pos2, w1p, w2p = lax.optimization_barrier((pos2, w1p, w2p))
    xs = _sc_dispatch(u2p.reshape(2 * s, PACK_W), pos2, 2 * n_rows).reshape(2, n_rows, PACK_W)

    b1 = b_e1[0].reshape(N_EXPERTS, D_FF, 2)
    b1g = b1[:, None, :, 0]
    b1l = b1[:, None, :, 1]
    ys = _moe_routed(tile_expert, tile_first, n_active, xs, w1p.reshape(N_EXPERTS, d, D_FF), b1g, b1l,
                     w2p.reshape(N_EXPERTS, D_FF, d // 2), b_e2[0][:, None, :])

    gather_idx = pos2.reshape(TOP_K, 2, s).transpose(1, 0, 2)
    sc_rows = s // COMBINE_CHUNKS
    out = None
    for ci in range(COMBINE_CHUNKS):
        idx = gather_idx[:, :, ci * sc_rows:(ci + 1) * sc_rows].reshape(1, -1)
        yg = _sc_gather(ys.reshape(2 * n_rows, PACK_W), idx).reshape(2, TOP_K, sc_rows, PACK_W)
        out = _final(x1, yg, w4, gt_f, g_final.reshape(1, d), tm, ci * (sc_rows // tm), out)
    return out.reshape(b, s, d)
```

```python
import dataclasses
import functools
import math

import numpy as np
import jax
import jax.numpy as jnp
from jax import lax
from jax.experimental import pallas as pl
from jax.experimental.pallas import tpu as pltpu
from jax.experimental.pallas import tpu_sc as plsc

D_MODEL = 1024
HEAD_DIM = 64
SWA_HEADS = 8
SWA_KV_HEADS = 2
WINDOW = 128
FOX_HEADS = 8
BLOCK = 128
REL_BUCKETS = 32
REL_MAX_DIST = WINDOW
N_EXPERTS = 32
TOP_K = 4
D_FF = D_MODEL
SWIGLU_LIMIT = 7.0
SWIGLU_ALPHA = 1.702
RMS_EPS = 1e-5
N_MOD = 6

SWA_Q = SWA_HEADS * HEAD_DIM
SWA_KV = SWA_KV_HEADS * HEAD_DIM
FOX_W = FOX_HEADS * HEAD_DIM
LANES = 128
N_PAIRS = FOX_HEADS // 2
NEG_BIG = -1e30
LOG2E = math.log2(math.e)
FOX_TILE = 256
SWA_BLOCKS = 4
FOX_QSUB = 2
FOX_CHUNK = 4
FOX_CHUNK_ALONE = 8
N_SPLIT = 3
SKIP_LOG2 = 127.0
NORM_SLACK = 1.01
PACK_W = 256
MOE_TILE = 512
SC_WINDOW = 128
RANK_RADIX = 128
COMBINE_CHUNKS = 2
SC_LANES = 16
SC_PACK_ROWS = 8
VMEM_LIMIT = 56 * 1024 * 1024

F32 = jnp.float32
BF16 = jnp.bfloat16
HIGHEST = lax.Precision.HIGHEST


def _dot(a, b):
    return jnp.dot(a, b, preferred_element_type=F32)


def _dot_nt(a, b, precision=None):
    return lax.dot_general(a, b, (((1,), (1,)), ((), ())), preferred_element_type=F32, precision=precision)


def _const_spec(shape):
    nd = len(shape)
    return pl.BlockSpec(shape, lambda *_: (0,) * nd)


def _params(sem):
    return pltpu.CompilerParams(dimension_semantics=sem, vmem_limit_bytes=VMEM_LIMIT)


def _ada_kernel(c_ref, w_ref, b_ref, o_ref):
    c = c_ref[...]
    act = c * jax.nn.sigmoid(c)
    o_ref[...] = jnp.sum(act * w_ref[...], axis=0, keepdims=True) + b_ref[...]


def _ada(c_col, w_ada, b_ada):
    n = w_ada.shape[1]
    tn = 1024
    return pl.pallas_call(
        _ada_kernel,
        grid=(n // tn,),
        in_specs=[_const_spec((D_MODEL, 1)),
                  pl.BlockSpec((D_MODEL, tn), lambda j: (0, j)),
                  pl.BlockSpec((1, tn), lambda j: (0, j))],
        out_specs=pl.BlockSpec((1, tn), lambda j: (0, j)),
        out_shape=jax.ShapeDtypeStruct((1, n), F32),
        compiler_params=_params(("arbitrary",)),
        name="ada_mod",
    )(c_col, w_ada, b_ada)


def _split_bf16(v):
    parts = []
    for _ in range(N_SPLIT):
        p = v.astype(BF16)
        v = v - p.astype(F32)
        parts.append(p)
    return parts


def _inproj_kernel(x_ref, g_ref, sh_ref, sc_ref, wa_ref, wb_ref, wvt_ref, wg_ref, bf_ref, hind_ref, place_ref,
                   qa_ref, kva_ref, vat_ref, qb_ref, kb_ref, vbt_ref, sga_ref, sgb_ref, ca_ref, qn_ref, kn_ref,
                   cf_ref, cl_ref, dg_ref, carry_ref):
    i = pl.program_id(0)
    tm = x_ref.shape[0]
    t = FOX_TILE
    nsub = tm // t

    @pl.when(i == 0)
    def _():
        carry_ref[...] = jnp.zeros_like(carry_ref)

    xf = x_ref[...]
    ms = jnp.mean(xf * xf, axis=-1, keepdims=True)
    y = xf * lax.rsqrt(ms + RMS_EPS) * g_ref[...]
    u = y * (1.0 + sc_ref[...]) + sh_ref[...]
    ub = u.astype(BF16)

    za = _dot(ub, wa_ref[...])
    qa_ref[...] = (za[:, :SWA_Q] * (HEAD_DIM ** -0.5 * LOG2E)).astype(BF16)
    kva_ref[...] = za[:, SWA_Q:].astype(BF16)

    zb = _dot(ub, wb_ref[...])
    qb = (zb[:, :FOX_W] * (HEAD_DIM ** -0.5 * LOG2E)).astype(BF16)
    kb = zb[:, FOX_W:2 * FOX_W].astype(BF16)
    qb_ref[...] = qb
    kb_ref[...] = kb
    vt = _dot_nt(wvt_ref[...], ub).astype(BF16)
    vbt_ref[...] = vt[:FOX_W]
    vat_ref[...] = vt[FOX_W:]

    def tile_norm_max(z, o_ref):
        zf = z.astype(F32)
        n2 = _dot((zf * zf).astype(BF16), hind_ref[...])
        for sb in range(nsub):
            o_ref[sb] = jnp.max(n2[sb * t:(sb + 1) * t], axis=0, keepdims=True)

    tile_norm_max(qb, qn_ref)
    tile_norm_max(kb, kn_ref)
    diag = _dot((qb.astype(F32) * kb.astype(F32)).astype(BF16), hind_ref[...])
    for sb in range(nsub):
        dg_ref[sb] = jnp.min(diag[sb * t:(sb + 1) * t], axis=0, keepdims=True)

    zg = _dot(ub, wg_ref[...])
    sg = jax.nn.sigmoid(zg)
    sga_ref[...] = sg[:, :D_MODEL].astype(BF16)
    sgb_ref[...] = sg[:, D_MODEL:].astype(BF16)

    fb = zb[:, 2 * FOX_W:2 * FOX_W + FOX_HEADS] + bf_ref[...]
    lf = jnp.minimum(fb, 0.0) - jnp.log1p(jnp.exp(-jnp.abs(fb)))
    r = lax.broadcasted_iota(jnp.int32, (t, t), 0)
    cc = lax.broadcasted_iota(jnp.int32, (t, t), 1)
    lower = (cc <= r).astype(BF16)
    carry = carry_ref[...]
    for sb in range(nsub):
        rows = slice(sb * t, (sb + 1) * t)
        local = sum(_dot(lower, p) for p in _split_bf16(lf[rows]))
        cf_ref[sb] = (local[0:1, :] + carry) * LOG2E
        carry = carry + local[t - 1:t, :]
        cl_ref[sb] = carry * LOG2E
        aug = sum(_dot(p, place_ref[k]) for k, p in enumerate(_split_bf16(local * LOG2E)))
        ca_ref[rows, :] = aug.astype(BF16)
    carry_ref[...] = carry


def _inproj(x2, g, sh, sc, wa, wb, wvt, wg, bfor, tm):
    s = x2.shape[0]
    nsub = tm // FOX_TILE
    row = lambda n: pl.BlockSpec((tm, n), lambda i: (i, 0))
    hind = np.zeros((FOX_W, FOX_HEADS), np.float32)
    hind[np.arange(FOX_W), np.arange(FOX_W) // HEAD_DIM] = 1.0
    place = np.zeros((N_SPLIT, FOX_HEADS, LANES), np.float32)
    for k in range(N_SPLIT):
        for h in range(FOX_HEADS):
            place[k, h, N_SPLIT * h + k] = 1.0
    hind = jnp.asarray(hind, BF16)
    place = jnp.asarray(place, BF16)
    out_shape = [
        jax.ShapeDtypeStruct((s, SWA_Q), BF16),
        jax.ShapeDtypeStruct((s, 2 * LANES), BF16),
        jax.ShapeDtypeStruct((SWA_KV, s), BF16),
        jax.ShapeDtypeStruct((s, FOX_W), BF16),
        jax.ShapeDtypeStruct((s, FOX_W), BF16),
        jax.ShapeDtypeStruct((FOX_W, s), BF16),
        jax.ShapeDtypeStruct((s, D_MODEL), BF16),
        jax.ShapeDtypeStruct((s, D_MODEL), BF16),
        jax.ShapeDtypeStruct((s, LANES), BF16),
        jax.ShapeDtypeStruct((s // FOX_TILE, 1, FOX_HEADS), F32),
        jax.ShapeDtypeStruct((s // FOX_TILE, 1, FOX_HEADS), F32),
        jax.ShapeDtypeStruct((s // FOX_TILE, 1, FOX_HEADS), F32),
        jax.ShapeDtypeStruct((s // FOX_TILE, 1, FOX_HEADS), F32),
        jax.ShapeDtypeStruct((s // FOX_TILE, 1, FOX_HEADS), F32),
    ]
    stat = pl.BlockSpec((nsub, 1, FOX_HEADS), lambda i: (i, 0, 0))
    out_specs = [row(SWA_Q), row(2 * LANES), pl.BlockSpec((SWA_KV, tm), lambda i: (0, i)),
                 row(FOX_W), row(FOX_W), pl.BlockSpec((FOX_W, tm), lambda i: (0, i)),
                 row(D_MODEL), row(D_MODEL), row(LANES), stat, stat, stat, stat, stat]
    consts = [g, sh, sc, wa, wb, wvt, wg, bfor, hind, place]
    return pl.pallas_call(
        _inproj_kernel,
        grid=(s // tm,),
        in_specs=[row(D_MODEL)] + [_const_spec(a.shape) for a in consts],
        out_specs=out_specs,
        out_shape=out_shape,
        scratch_shapes=[pltpu.VMEM((1, FOX_HEADS), F32)],
        compiler_params=_params(("arbitrary",)),
        name="in_proj",
    )(x2, *consts)


def _t5_buckets_np():
    qi = np.arange(BLOCK)[:, None]
    kj = np.arange(2 * BLOCK)[None, :]
    dist = BLOCK + qi - kj
    n = np.maximum(dist, 0)
    max_exact = REL_BUCKETS // 2
    nf = np.maximum(n, 1).astype(np.float32)
    large = max_exact + (np.log(nf / np.float32(max_exact)) / np.float32(math.log(REL_MAX_DIST / max_exact))
                         * np.float32(REL_BUCKETS - max_exact)).astype(np.int32)
    large = np.minimum(large, REL_BUCKETS - 1)
    bucket = np.where(n < max_exact, n, large).astype(np.int32)
    band = (dist >= 0) & (dist < WINDOW)
    return np.where(band, bucket, -1).astype(np.int32)


def _swa_kernel(rel_ref, sink_ref, bkt_ref, q_ref, kc_ref, kp_ref, vc_ref, vp_ref, o_ref, bias_ref):
    n = pl.program_id(0)

    @pl.when(n == 0)
    def _():
        bkt = bkt_ref[...]
        prev = lax.broadcasted_iota(jnp.int32, bkt.shape, 0) < BLOCK
        for h in range(SWA_HEADS):
            acc = jnp.full(bkt.shape, NEG_BIG, F32)
            for b in range(REL_BUCKETS):
                acc = jnp.where(bkt == b, rel_ref[b * SWA_HEADS + h] * LOG2E, acc)
            cols = slice((h % 2) * BLOCK, (h % 2 + 1) * BLOCK)
            bias_ref[0, h // 2, :, cols] = acc
            bias_ref[1, h // 2, :, cols] = jnp.where(prev, NEG_BIG, acc)

    lane = lax.broadcasted_iota(jnp.int32, (BLOCK, LANES), 1)
    col2 = lax.broadcasted_iota(jnp.int32, (1, 2 * BLOCK), 1)
    k_all = jnp.concatenate([kp_ref[...], kc_ref[...]], axis=0)
    v_all = jnp.concatenate([vp_ref[...], vc_ref[...]], axis=1)
    work = [(b, p) for b in range(SWA_BLOCKS) for p in range(SWA_HEADS // 2)]
    scores = []
    for b, p in work:
        qp = q_ref[b * BLOCK:(b + 1) * BLOCK, p * LANES:(p + 1) * LANES]
        zero = jnp.zeros_like(qp)
        qs = jnp.concatenate([jnp.where(lane < HEAD_DIM, qp, zero), jnp.where(lane >= HEAD_DIM, qp, zero)], axis=0)
        g = p // 2
        scores.append(_dot_nt(k_all[b * BLOCK:(b + 2) * BLOCK, g * LANES:(g + 1) * LANES], qs))
    weights = []
    for (b, p), s in zip(work, scores):
        first = jnp.where(n == 0, 1, 0) if b == 0 else 0
        s = s + bias_ref[first, p]
        sink = jnp.where(col2 < BLOCK, sink_ref[2 * p], sink_ref[2 * p + 1]) * LOG2E
        m = jnp.maximum(jnp.max(s, axis=0, keepdims=True), sink)
        e = jnp.exp2(s - m)
        denom = jnp.sum(e, axis=0, keepdims=True) + jnp.exp2(sink - m)
        weights.append((e.astype(BF16), denom))
    outs = [[] for _ in range(SWA_BLOCKS)]
    for (b, p), (e, denom) in zip(work, weights):
        g = p // 2
        o = _dot(v_all[g * HEAD_DIM:(g + 1) * HEAD_DIM, b * BLOCK:(b + 2) * BLOCK], e) / denom
        outs[b] += [o[:, :BLOCK], o[:, BLOCK:]]
    for b in range(SWA_BLOCKS):
        o_ref[b * BLOCK:(b + 1) * BLOCK, :] = jnp.concatenate(outs[b], axis=0).T.astype(BF16)


def _swa(rel_flat, sinks, bkt_t, qa, ka2, vat):
    s = qa.shape[0]
    rows = SWA_BLOCKS * BLOCK
    smem = pl.BlockSpec(memory_space=pltpu.SMEM)
    prev = lambda n: jnp.maximum(n * SWA_BLOCKS - 1, 0)
    return pl.pallas_call(
        _swa_kernel,
        grid=(s // rows,),
        in_specs=[smem, smem, _const_spec(bkt_t.shape),
                  pl.BlockSpec((rows, SWA_Q), lambda n: (n, 0)),
                  pl.BlockSpec((rows, 2 * LANES), lambda n: (n, 0)),
                  pl.BlockSpec((BLOCK, 2 * LANES), lambda n: (prev(n), 0)),
                  pl.BlockSpec((SWA_KV, rows), lambda n: (0, n)),
                  pl.BlockSpec((SWA_KV, BLOCK), lambda n: (0, prev(n)))],
        out_specs=pl.BlockSpec((rows, SWA_Q), lambda n: (n, 0)),
        out_shape=jax.ShapeDtypeStruct((s, SWA_Q), BF16),
        scratch_shapes=[pltpu.VMEM((2, SWA_HEADS // 2, 2 * BLOCK, 2 * BLOCK), F32)],
        compiler_params=_params(("arbitrary",)),
        name="swa_attn",
    )(rel_flat, sinks, bkt_t, qa, ka2, ka2, vat, vat)


def _fox_kernel(jlo_ref, base_ref, q_ref, k_ref, ca_ref, vt_ref, o_ref):
    p = pl.program_id(0)
    i = pl.program_id(1)
    t = FOX_TILE
    tq = FOX_QSUB * t
    first_diag = i * FOX_QSUB
    lane = lax.broadcasted_iota(jnp.int32, (tq, LANES), 1)
    q = q_ref[...]
    wq = []
    for hh in range(2):
        in_head = (lane >= hh * HEAD_DIM) & (lane < (hh + 1) * HEAD_DIM)
        qm = jnp.where(in_head, q, jnp.zeros_like(q))
        first_term = N_SPLIT * (2 * p + hh)
        sel = (lane >= first_term) & (lane < first_term + N_SPLIT)
        aug = jnp.where(sel, -1.0, 0.0).astype(BF16)
        wq.append(jnp.concatenate([qm, aug], axis=1))
    key = lax.broadcasted_iota(jnp.int32, (t, tq), 0)
    qry = lax.broadcasted_iota(jnp.int32, (t, tq), 1)
    causal = [key + d * t <= qry for d in range(FOX_QSUB)]

    def step(j, carry, heads, nsub, diagonal=False):
        start = pl.multiple_of(j * t, t)
        rows = nsub * t
        lhs = jnp.concatenate([k_ref[pl.ds(start, rows), :], ca_ref[pl.ds(start, rows), :]], axis=1)
        scores = [_dot_nt(lhs, wq[hh]) for hh in heads]
        mid = []
        for hh, s, (m, l, acc) in zip(heads, scores, carry):
            h = 2 * p + hh
            parts = [s[k * t:(k + 1) * t] for k in range(nsub)]
            if diagonal:
                for d in range(FOX_QSUB):
                    k = nsub - FOX_QSUB + d
                    parts[k] = jnp.where(causal[d], parts[k], NEG_BIG)
            djs = [base_ref[h, first_diag] - base_ref[h, j + k] for k in range(nsub)]
            m_new = m
            for part, dj in zip(parts, djs):
                m_new = jnp.maximum(m_new, jnp.max(part, axis=0, keepdims=True) + dj)
            alpha = jnp.exp2(m - m_new)
            es = [jnp.exp2(part + (dj - m_new)) for part, dj in zip(parts, djs)]
            l = alpha * l
            for e in es:
                l = l + jnp.sum(e, axis=0, keepdims=True)
            e_all = es[0] if nsub == 1 else jnp.concatenate(es, axis=0)
            mid.append((m_new, l, alpha, acc, e_all.astype(BF16)))
        new = []
        for hh, (m_new, l, alpha, acc, e_all) in zip(heads, mid):
            vt = vt_ref[hh * HEAD_DIM:(hh + 1) * HEAD_DIM, pl.ds(start, rows)]
            new.append((m_new, l, alpha * acc + _dot(vt, e_all)))
        return tuple(new)

    def run_alone(lo, hi, carry, heads):
        n_long = (hi - lo) // FOX_CHUNK_ALONE
        carry = lax.fori_loop(0, n_long, lambda n, c: step(lo + FOX_CHUNK_ALONE * n, c, heads, FOX_CHUNK_ALONE), carry)
        mid = lo + FOX_CHUNK_ALONE * n_long
        n_full = (hi - mid) // FOX_CHUNK
        carry = lax.fori_loop(0, n_full, lambda n, c: step(mid + FOX_CHUNK * n, c, heads, FOX_CHUNK), carry)
        rest = mid + FOX_CHUNK * n_full
        tails = [lambda c: c] + [functools.partial(lambda c, k: step(rest, c, heads, k), k=k)
                                 for k in range(1, FOX_CHUNK)]
        return lax.switch(hi - rest, tails, carry)

    def run_to_diagonal(lo, carry, heads):
        end = first_diag + FOX_QSUB
        count = end - lo
        last = (count - FOX_QSUB) % FOX_CHUNK + FOX_QSUB
        n_full = (count - last) // FOX_CHUNK
        carry = lax.fori_loop(0, n_full, lambda n, c: step(lo + FOX_CHUNK * n, c, heads, FOX_CHUNK), carry)
        tails = [functools.partial(lambda c, k: step(end - k, c, heads, k, diagonal=True), k=k)
                 for k in range(FOX_QSUB, FOX_QSUB + FOX_CHUNK)]
        return lax.switch(last - FOX_QSUB, tails, carry)

    lo0 = jlo_ref[2 * p, i]
    lo1 = jlo_ref[2 * p + 1, i]
    lo_both = jnp.maximum(lo0, lo1)
    init = (jnp.full((1, tq), NEG_BIG, F32), jnp.zeros((1, tq), F32), jnp.zeros((HEAD_DIM, tq), F32))
    (c0,) = run_alone(lo0, lo_both, (init,), (0,))
    (c1,) = run_alone(lo1, lo_both, (init,), (1,))
    carry = run_to_diagonal(lo_both, (c0, c1), (0, 1))
    ot = jnp.concatenate([carry[0][2] / carry[0][1], carry[1][2] / carry[1][1]], axis=0)
    o_ref[...] = ot.T.astype(BF16)


def _fox_schedule(c_first, c_last, qn2, kn2, diag_min):
    nt = qn2.shape[0]
    t = FOX_TILE
    bq = jnp.sqrt(qn2.reshape(nt, FOX_HEADS)) * NORM_SLACK
    bk = jnp.sqrt(kn2.reshape(nt, FOX_HEADS)) * NORM_SLACK
    c_first = c_first.reshape(nt, FOX_HEADS)
    c_last = c_last.reshape(nt, FOX_HEADS)
    diag_lo = diag_min.reshape(nt, FOX_HEADS) - (NORM_SLACK - 1.0) * bq * bk
    upper = bq[:, None, :] * bk[None, :, :] - diag_lo[:, None, :] + c_first[:, None, :] - c_last[None, :, :]
    ii = jnp.arange(nt)[:, None, None]
    jj = jnp.arange(nt)[None, :, None]
    needed = (jj <= ii) & ((upper >= -SKIP_LOG2) | (jj == ii))
    jlo = jnp.min(jnp.where(needed, jj, nt), axis=1)
    jlo = jnp.min(jlo.reshape(nt // FOX_QSUB, FOX_QSUB, FOX_HEADS), axis=1).T.astype(jnp.int32)
    base = jnp.concatenate([jnp.zeros((1, FOX_HEADS), F32), c_last[:-1]], axis=0).T
    return jlo, base


def _fox(jlo, base, qb, kb, ca, vbt):
    s = qb.shape[0]
    t = FOX_QSUB * FOX_TILE
    grid_spec = pltpu.PrefetchScalarGridSpec(
        num_scalar_prefetch=2,
        grid=(N_PAIRS, s // t),
        in_specs=[pl.BlockSpec((t, LANES), lambda p, i, *_: (i, p)),
                  pl.BlockSpec((s, LANES), lambda p, i, *_: (0, p)),
                  pl.BlockSpec((s, LANES), lambda p, i, *_: (0, 0)),
                  pl.BlockSpec((LANES, s), lambda p, i, *_: (p, 0))],
        out_specs=pl.BlockSpec((t, LANES), lambda p, i, *_: (i, p)),
    )
    return pl.pallas_call(
        _fox_kernel,
        grid_spec=grid_spec,
        out_shape=jax.ShapeDtypeStruct((s, FOX_W), BF16),
        compiler_params=_params(("arbitrary", "arbitrary")),
        name="fox_attn",
    )(jlo, base, qb, kb, ca, vbt)


def _rms(x, g):
    return x * lax.rsqrt(jnp.mean(x * x, axis=-1, keepdims=True) + RMS_EPS) * g


def _pack_rows(v):
    halves = []
    for j in range(2):
        lo = v[:, (2 * j) * PACK_W:(2 * j + 1) * PACK_W].astype(BF16).astype(F32)
        hi = v[:, (2 * j + 1) * PACK_W:(2 * j + 2) * PACK_W].astype(BF16).astype(F32)
        lo_bits = lax.bitcast_convert_type(lo, jnp.uint32)
        hi_bits = lax.bitcast_convert_type(hi, jnp.uint32)
        halves.append(hi_bits | (lo_bits >> 16))
    return halves


def _unpack_words(w):
    lo = lax.bitcast_convert_type(w << 16, F32)
    hi = lax.bitcast_convert_type(w & jnp.uint32(0xFFFF0000), F32)
    return lo, hi


def _outproj_kernel(x_ref, ya_ref, yb_ref, sga_ref, sgb_ref, pa_ref, pb_ref, wo_ref, gt_ref, g_ref, sh_ref, sc_ref,
                    wr_ref, br_ref, x1_ref, u2p_ref, e4_ref, r4_ref, w4_ref, cnt_ref, carry_ref):
    i = pl.program_id(0)
    tm = x_ref.shape[0]

    @pl.when(i == 0)
    def _():
        carry_ref[...] = jnp.zeros_like(carry_ref)

    merged = (sga_ref[...].astype(F32) * _dot(ya_ref[...], pa_ref[...])
              + sgb_ref[...].astype(F32) * _dot(yb_ref[...], pb_ref[...]))
    x1 = x_ref[...] + gt_ref[...] * _dot(merged.astype(BF16), wo_ref[...])
    x1_ref[...] = x1
    u2 = _rms(x1, g_ref[...]) * (1.0 + sc_ref[...]) + sh_ref[...]
    halves = _pack_rows(u2)
    u2p_ref[0] = halves[0]
    u2p_ref[1] = halves[1]

    logits = _dot(u2.astype(BF16), wr_ref[...]) + br_ref[...]
    eidx = lax.broadcasted_iota(jnp.int32, logits.shape, 1)
    work = logits
    sel = jnp.zeros(logits.shape, jnp.bool_)
    picks, vals = [], []
    for k in range(TOP_K):
        m = jnp.max(work, axis=-1, keepdims=True)
        first = jnp.min(jnp.where(work == m, eidx, N_EXPERTS), axis=-1, keepdims=True)
        hit = eidx == first
        sel = sel | hit
        work = jnp.where(hit, -jnp.inf, work)
        picks.append(first)
        vals.append(m)
    exps = [jnp.exp(v - vals[0]) for v in vals]
    denom = exps[0] + exps[1] + exps[2] + exps[3]

    r = lax.broadcasted_iota(jnp.int32, (tm, tm), 0)
    cc = lax.broadcasted_iota(jnp.int32, (tm, tm), 1)
    before = (cc < r).astype(BF16)
    chosen = sel.astype(BF16)
    rank = _dot(before, chosen) + carry_ref[...]
    cnt = carry_ref[...] + jnp.sum(chosen.astype(F32), axis=0, keepdims=True)
    carry_ref[...] = cnt
    cnt_ref[...] = cnt
    slot = lax.broadcasted_iota(jnp.int32, (tm, 4 * TOP_K), 1)
    cols = jnp.zeros((tm, 4 * TOP_K), F32)
    for k in range(TOP_K):
        rk = jnp.sum(jnp.where(eidx == picks[k], rank, 0.0), axis=-1, keepdims=True)
        hi = jnp.floor(rk * (1.0 / RANK_RADIX))
        for j, col in ((k, picks[k].astype(F32)), (TOP_K + k, hi), (2 * TOP_K + k, rk - hi * RANK_RADIX)):
            cols = jnp.where(slot == j, col, cols)
        w4_ref[:, k:k + 1] = exps[k] / denom
    eye = (lax.broadcasted_iota(jnp.int32, (4 * TOP_K, 4 * TOP_K), 0)
           == lax.broadcasted_iota(jnp.int32, (4 * TOP_K, 4 * TOP_K), 1)).astype(BF16)
    flipped = _dot_nt(eye, cols.astype(BF16))
    e4_ref[...] = flipped[:TOP_K].astype(jnp.int32)
    r4_ref[...] = (flipped[TOP_K:2 * TOP_K] * RANK_RADIX + flipped[2 * TOP_K:3 * TOP_K]).astype(jnp.int32)


def _outproj(x2, ya, yb, sga, sgb, pa, pb, wo, gt, g, sh, sc, wr, br, tm):
    s = x2.shape[0]
    row = lambda n: pl.BlockSpec((tm, n), lambda i: (i, 0))
    consts = [pa, pb, wo, gt, g, sh, sc, wr, br]
    return pl.pallas_call(
        _outproj_kernel,
        grid=(s // tm,),
        in_specs=[row(D_MODEL), row(SWA_Q), row(FOX_W), row(D_MODEL), row(D_MODEL)] + [_const_spec(a.shape) for a in consts],
        out_specs=[row(D_MODEL), pl.BlockSpec((2, tm, PACK_W), lambda i: (0, i, 0)),
                   pl.BlockSpec((TOP_K, tm), lambda i: (0, i)), pl.BlockSpec((TOP_K, tm), lambda i: (0, i)), row(TOP_K),
                   _const_spec((1, N_EXPERTS))],
        out_shape=[jax.ShapeDtypeStruct((s, D_MODEL), F32),
                   jax.ShapeDtypeStruct((2, s, PACK_W), jnp.uint32),
                   jax.ShapeDtypeStruct((TOP_K, s), jnp.int32),
                   jax.ShapeDtypeStruct((TOP_K, s), jnp.int32),
                   jax.ShapeDtypeStruct((s, TOP_K), F32),
                   jax.ShapeDtypeStruct((1, N_EXPERTS), F32)],
        scratch_shapes=[pltpu.VMEM((1, N_EXPERTS), F32)],
        compiler_params=_params(("arbitrary",)),
        name="out_proj_router",
    )(x2, ya, yb, sga, sgb, *consts)


def _sc_mesh():
    return plsc.VectorSubcoreMesh(core_axis_name="core", subcore_axis_name="subcore")


def _sc_dispatch(rows, idx, n_out):
    n, width = rows.shape

    @functools.partial(pl.kernel, out_type=jax.ShapeDtypeStruct((n_out, width), rows.dtype), mesh=_sc_mesh(),
                       scratch_types=[])
    def dispatch(x_hbm, i_hbm, o_hbm):
        def body(x_vmem, i_vmem):
            for k in range(TOP_K):
                pltpu.sync_copy(x_vmem, o_hbm.at[i_vmem.at[k]])

        pltpu.emit_pipeline(
            body, grid=(n // SC_WINDOW,),
            in_specs=[pl.BlockSpec((SC_WINDOW, width), lambda i: (i, 0)),
                      pl.BlockSpec((TOP_K, SC_WINDOW), lambda i: (0, i))],
            out_specs=[], core_axis_name=("core", "subcore"), dimension_semantics=(pltpu.PARALLEL,),
        )(x_hbm, i_hbm)

    return dispatch(rows, idx)


def _sc_gather(table, idx):
    n = idx.shape[1]
    width = table.shape[1]

    @functools.partial(pl.kernel, out_type=jax.ShapeDtypeStruct((n, width), table.dtype), mesh=_sc_mesh(),
                       scratch_types=[])
    def gather(t_hbm, i_hbm, o_hbm):
        def body(i_vmem, o_vmem):
            pltpu.sync_copy(t_hbm.at[i_vmem.at[0]], o_vmem)

        pltpu.emit_pipeline(
            body, grid=(n // SC_WINDOW,),
            in_specs=[pl.BlockSpec((1, SC_WINDOW), lambda i: (0, i))],
            out_specs=[pl.BlockSpec((SC_WINDOW, width), lambda i: (i, 0))],
            core_axis_name=("core", "subcore"), dimension_semantics=(pltpu.PARALLEL,),
        )(i_hbm, o_hbm)

    return gather(table, idx)


def _sc_pack_weights(w1, w2):
    r1, r2 = w1.shape[0], w2.shape[0]
    half = D_FF // 2

    def bf16_bits(v):
        u = plsc.bitcast(v, jnp.uint32)
        return (u + jnp.uint32(0x7FFF) + ((u >> 16) & jnp.uint32(1))) >> 16

    cp = pltpu.CompilerParams()
    if "needs_layout_passes" in pltpu.CompilerParams.__dataclass_fields__:
        cp = dataclasses.replace(cp, needs_layout_passes=False)

    n_in = w1.size + w2.size
    cost = pl.CostEstimate(flops=8 * n_in, transcendentals=0, bytes_accessed=6 * n_in)
    @functools.partial(
        pl.kernel, mesh=_sc_mesh(), scratch_types=[], compiler_params=cp, cost_estimate=cost,
        out_type=(jax.ShapeDtypeStruct((r1, D_FF), jnp.uint32), jax.ShapeDtypeStruct((r2, D_MODEL // 2), jnp.uint32)))
    def pack(w1_hbm, w2_hbm, o1_hbm, o2_hbm):
        lanes = lax.iota(jnp.int32, SC_LANES)

        def body1(x_vmem, o_vmem):
            @pl.loop(0, SC_PACK_ROWS)
            def _(r):
                rr = jnp.full((SC_LANES,), r, jnp.int32)

                @pl.loop(0, half, step=SC_LANES)
                def _(c):
                    col = 2 * (c + lanes)
                    g0 = plsc.load_gather(x_vmem, [rr, col])
                    g1 = plsc.load_gather(x_vmem, [rr, col + 2 * half])
                    l0 = plsc.load_gather(x_vmem, [rr, col + 1])
                    l1 = plsc.load_gather(x_vmem, [rr, col + 2 * half + 1])
                    o_vmem[r, pl.ds(c, SC_LANES)] = bf16_bits(g0) | (bf16_bits(g1) << 16)
                    o_vmem[r, pl.ds(half + c, SC_LANES)] = bf16_bits(l0) | (bf16_bits(l1) << 16)

        def body2(x_vmem, o_vmem):
            @pl.loop(0, SC_PACK_ROWS)
            def _(r):
                @pl.loop(0, D_MODEL // 2, step=SC_LANES)
                def _(c):
                    lo = x_vmem[r, pl.ds(c, SC_LANES)]
                    hi = x_vmem[r, pl.ds(D_MODEL // 2 + c, SC_LANES)]
                    o_vmem[r, pl.ds(c, SC_LANES)] = bf16_bits(lo) | (bf16_bits(hi) << 16)

        for body, x_hbm, o_hbm, rows in ((body1, w1_hbm, o1_hbm, r1), (body2, w2_hbm, o2_hbm, r2)):
            pltpu.emit_pipeline(
                body, grid=(rows // SC_PACK_ROWS,),
                in_specs=[pl.BlockSpec((SC_PACK_ROWS, x_hbm.shape[1]), lambda i: (i, 0))],
                out_specs=[pl.BlockSpec((SC_PACK_ROWS, o_hbm.shape[1]), lambda i: (i, 0))],
                core_axis_name=("core", "subcore"), dimension_semantics=(pltpu.PARALLEL,),
            )(x_hbm, o_hbm)

    return pack(w1, w2)


def _moe_kernel(te_ref, first_ref, nact_ref, xs_ref, w1_ref, b1g_ref, b1l_ref, w2_ref, b2_ref, y_ref,
                wg_ref, wl_ref, w2b_ref):
    i = pl.program_id(0)
    live = i < nact_ref[0]

    @pl.when(jnp.logical_and(live, first_ref[i] == 1))
    def _():
        half = D_FF // 2
        for dst, words in ((wg_ref, w1_ref[0, :, :half]), (wl_ref, w1_ref[0, :, half:]), (w2b_ref, w2_ref[0])):
            lo, hi = _unpack_words(words)
            dst[:, :half] = lo.astype(BF16)
            dst[:, half:] = hi.astype(BF16)

    @pl.when(live)
    def _():
        chunks = []
        for j in range(2):
            lo, hi = _unpack_words(xs_ref[j])
            chunks += [lo.astype(BF16), hi.astype(BF16)]
        x = jnp.concatenate(chunks, axis=1)
        hg = _dot(x, wg_ref[...]) + b1g_ref[0]
        hl = _dot(x, wl_ref[...]) + b1l_ref[0]
        glu = jnp.minimum(hg, SWIGLU_LIMIT)
        lin = jnp.clip(hl, -SWIGLU_LIMIT, SWIGLU_LIMIT)
        a = glu * jax.nn.sigmoid(SWIGLU_ALPHA * glu) * (lin + 1.0)
        y = _dot(a.astype(BF16), w2b_ref[...]) + b2_ref[0]
        halves = _pack_rows(y)
        y_ref[0] = halves[0]
        y_ref[1] = halves[1]


def _moe_routed(tile_expert, tile_first, n_active, xs, w1, b1g, b1l, w2, b2):
    n_rows = xs.shape[1]
    n_tiles = n_rows // MOE_TILE
    live = lambda i, na: jnp.minimum(i, na[0] - 1)
    rows_spec = pl.BlockSpec((2, MOE_TILE, PACK_W), lambda i, te, tf, na: (0, live(i, na), 0))
    wsp = lambda a: pl.BlockSpec((1,) + a.shape[1:], lambda i, te, tf, na: (te[live(i, na)], 0, 0))
    grid_spec = pltpu.PrefetchScalarGridSpec(
        num_scalar_prefetch=3,
        grid=(n_tiles,),
        in_specs=[rows_spec, wsp(w1), wsp(b1g), wsp(b1l), wsp(w2), wsp(b2)],
        out_specs=rows_spec,
        scratch_shapes=[pltpu.VMEM((D_MODEL, D_FF), BF16), pltpu.VMEM((D_MODEL, D_FF), BF16),
                        pltpu.VMEM((D_FF, D_MODEL), BF16)],
    )
    return pl.pallas_call(
        _moe_kernel,
        grid_spec=grid_spec,
        out_shape=jax.ShapeDtypeStruct(xs.shape, jnp.uint32),
        compiler_params=_params(("arbitrary",)),
        name="moe_routed",
    )(tile_expert, tile_first, n_active, xs, w1, b1g, b1l, w2, b2)


def _final_kernel(x1_ref, yg_ref, w4_ref, gt_ref, gf_ref, *rest):
    o_ref = rest[-1]
    w4 = w4_ref[...]
    cols = []
    for j in range(2):
        lo_acc = hi_acc = None
        for k in range(TOP_K):
            lo, hi = _unpack_words(yg_ref[j, k])
            wk = w4[:, k:k + 1]
            lo_acc = wk * lo if lo_acc is None else lo_acc + wk * lo
            hi_acc = wk * hi if hi_acc is None else hi_acc + wk * hi
        cols += [lo_acc, hi_acc]
    moe = jnp.concatenate(cols, axis=1)
    x2 = x1_ref[...] + gt_ref[...] * moe
    o_ref[...] = _rms(x2, gf_ref[...])


def _final(x1, yg, w4, gt, gf, tm, first_tile, prev_out):
    s = x1.shape[0]
    n_tiles = yg.shape[2] // tm
    row = lambda n: pl.BlockSpec((tm, n), lambda i: (i + first_tile, 0))
    in_specs = [row(D_MODEL), pl.BlockSpec((2, TOP_K, tm, PACK_W), lambda i: (0, 0, i, 0)), row(TOP_K),
                _const_spec(gt.shape), _const_spec(gf.shape)]
    args = [x1, yg, w4, gt, gf]
    aliases = {}
    if prev_out is not None:
        in_specs.append(pl.BlockSpec(memory_space=pl.ANY))
        args.append(prev_out)
        aliases = {len(args) - 1: 0}
    return pl.pallas_call(
        _final_kernel,
        grid=(n_tiles,),
        in_specs=in_specs,
        out_specs=row(D_MODEL),
        out_shape=jax.ShapeDtypeStruct((s, D_MODEL), F32),
        input_output_aliases=aliases,
        compiler_params=_params(("arbitrary",)),
        name="combine_final_norm",
    )(*args)


def _routing_tables(e4, r4, counts, s):
    n_rows = TOP_K * s + N_EXPERTS * MOE_TILE
    cnt = counts.reshape(N_EXPERTS).astype(jnp.int32)
    padded = ((cnt + MOE_TILE - 1) // MOE_TILE) * MOE_TILE
    ends = jnp.cumsum(padded)
    starts = ends - padded
    pos = r4
    for e in range(N_EXPERTS):
        pos = pos + jnp.where(e4 == e, starts[e], 0)
    tile_start = jnp.arange(n_rows // MOE_TILE, dtype=jnp.int32) * MOE_TILE
    tile_expert = jnp.minimum(jnp.sum(tile_start[:, None] >= ends[None, :], axis=1), N_EXPERTS - 1).astype(jnp.int32)
    tile_first = jnp.concatenate([jnp.ones((1,), jnp.int32), (tile_expert[1:] != tile_expert[:-1]).astype(jnp.int32)])
    n_active = (ends[-1:] // MOE_TILE).astype(jnp.int32)
    return pos, tile_expert, tile_first, n_active, n_rows


def kernel(x, c, w_ada, b_ada, g_mix, w_in, b_forget, sinks, rel_bias, w_proj_a, w_proj_b, w_out, g_ffn,
           w_router, b_router, w_e1, b_e1, w_e2, b_e2, g_final):
    b, s, d = x.shape
    assert b == 1 and d == D_MODEL and w_ada.shape[0] == 1
    assert s <= 256 * RANK_RADIX
    x2 = x.reshape(s, d)
    tm = min(512, s)

    mod = _ada(c.reshape(d, 1), w_ada[0], b_ada)
    sh_m, sc_m, gt_m, sh_f, sc_f, gt_f = [mod[:, k * d:(k + 1) * d] for k in range(N_MOD)]

    w = w_in[0]
    o_ka, o_va, o_b = SWA_Q, SWA_Q + SWA_KV, SWA_Q + 2 * SWA_KV
    o_f = o_b + 3 * FOX_W
    o_g = o_f + FOX_HEADS
    dup = lambda m: jnp.concatenate([m[:, :HEAD_DIM], m[:, :HEAD_DIM], m[:, HEAD_DIM:], m[:, HEAD_DIM:]], axis=1)
    wa = jnp.concatenate([w[:, :SWA_Q], dup(w[:, o_ka:o_va])], axis=1).astype(BF16)
    pad = jnp.zeros((d, LANES - FOX_HEADS), F32)
    wb = jnp.concatenate([w[:, o_b:o_b + 2 * FOX_W], w[:, o_f:o_g], pad], axis=1).astype(BF16)
    wvt = jnp.concatenate([w[:, o_b + 2 * FOX_W:o_f], w[:, o_va:o_b]], axis=1).T.astype(BF16)
    wg = w[:, o_g:].astype(BF16)
    qa, ka2, vat, qb, kb, vbt, sga, sgb, ca, qn2, kn2, c_first, c_last, diag_min = _inproj(
        x2, g_mix, sh_m, sc_m, wa, wb, wvt, wg, b_forget, tm)

    ya = _swa(rel_bias.reshape(-1), sinks[0], jnp.asarray(_t5_buckets_np().T), qa, ka2, vat)
    jlo, base = _fox_schedule(c_first, c_last, qn2, kn2, diag_min)
    yb = _fox(jlo, base, qb, kb, ca, vbt)

    x1, u2p, e4, r4, w4, counts = _outproj(
        x2, ya, yb, sga, sgb, w_proj_a[0].astype(BF16), w_proj_b[0].astype(BF16), w_out[0].astype(BF16),
        gt_m, g_ffn, sh_f, sc_f, w_router[0].astype(BF16), b_router, tm)

    pos, tile_expert, tile_first, n_active, n_rows = _routing_tables(e4, r4, counts, s)
    pos2 = jnp.concatenate([pos, pos + n_rows], axis=1)
    w1p, w2p = _sc_pack_weights(w_e1[0].reshape(N_EXPERTS * d, 2 * D_FF), w_e2[0].reshape(N_EXPERTS * D_FF, d))
    pos2, w1p, w2p = lax.optimization_barrier((pos2, w1p, w2p))
    xs = _sc_dispatch(u2p.reshape(2 * s, PACK_W), pos2, 2 * n_rows).reshape(2, n_rows, PACK_W)

    b1 = b_e1[0].reshape(N_EXPERTS, D_FF, 2)
    b1g = b1[:, None, :, 0]
    b1l = b1[:, None, :, 1]
    ys = _moe_routed(tile_expert, tile_first, n_active, xs, w1p.reshape(N_EXPERTS, d, D_FF), b1g, b1l,
                     w2p.reshape(N_EXPERTS, D_FF, d // 2), b_e2[0][:, None, :])

    gather_idx = pos2.reshape(TOP_K, 2, s).transpose(1, 0, 2)
    sc_rows = s // COMBINE_CHUNKS
    out = None
    for ci in range(COMBINE_CHUNKS):
        idx = gather_idx[:, :, ci * sc_rows:(ci + 1) * sc_rows].reshape(1, -1)
        yg = _sc_gather(ys.reshape(2 * n_rows, PACK_W), idx).reshape(2, TOP_K, sc_rows, PACK_W)
        out = _final(x1, yg, w4, gt_f, g_final.reshape(1, d), tm, ci * (sc_rows // tm), out)
    return out.reshape(b, s, d)
```

```python
import dataclasses
import functools
import math

import numpy as np
import jax
import jax.numpy as jnp
from jax import lax
from jax.experimental import pallas as pl
from jax.experimental.pallas import tpu as pltpu
from jax.experimental.pallas import tpu_sc as plsc

D_MODEL = 1024
HEAD_DIM = 64
SWA_HEADS = 8
SWA_KV_HEADS = 2
WINDOW = 128
FOX_HEADS = 8
BLOCK = 128
REL_BUCKETS = 32
REL_MAX_DIST = WINDOW
N_EXPERTS = 32
TOP_K = 4
D_FF = D_MODEL
SWIGLU_LIMIT = 7.0
SWIGLU_ALPHA = 1.702
RMS_EPS = 1e-5
N_MOD = 6

SWA_Q = SWA_HEADS * HEAD_DIM
SWA_KV = SWA_KV_HEADS * HEAD_DIM
FOX_W = FOX_HEADS * HEAD_DIM
LANES = 128
N_PAIRS = FOX_HEADS // 2
NEG_BIG = -1e30
LOG2E = math.log2(math.e)
FOX_TILE = 256
SWA_BLOCKS = 4
FOX_QSUB = 2
FOX_CHUNK = 4
FOX_CHUNK_ALONE = 8
N_SPLIT = 3
SKIP_LOG2 = 127.0
NORM_SLACK = 1.01
PACK_W = 256
MOE_TILE = 512
SC_WINDOW = 128
RANK_RADIX = 128
SC_LANES = 16
SC_PACK_ROWS = 8
VMEM_LIMIT = 56 * 1024 * 1024

F32 = jnp.float32
BF16 = jnp.bfloat16
HIGHEST = lax.Precision.HIGHEST


def _dot(a, b):
    return jnp.dot(a, b, preferred_element_type=F32)


def _dot_nt(a, b, precision=None):
    return lax.dot_general(a, b, (((1,), (1,)), ((), ())), preferred_element_type=F32, precision=precision)


def _const_spec(shape):
    nd = len(shape)
    return pl.BlockSpec(shape, lambda *_: (0,) * nd)


def _params(sem):
    return pltpu.CompilerParams(dimension_semantics=sem, vmem_limit_bytes=VMEM_LIMIT)


def _ada_kernel(c_ref, w_ref, b_ref, o_ref):
    c = c_ref[...]
    act = c * jax.nn.sigmoid(c)
    o_ref[...] = jnp.sum(act * w_ref[...], axis=0, keepdims=True) + b_ref[...]


def _ada(c_col, w_ada, b_ada):
    n = w_ada.shape[1]
    tn = 1024
    return pl.pallas_call(
        _ada_kernel,
        grid=(n // tn,),
        in_specs=[_const_spec((D_MODEL, 1)),
                  pl.BlockSpec((D_MODEL, tn), lambda j: (0, j)),
                  pl.BlockSpec((1, tn), lambda j: (0, j))],
        out_specs=pl.BlockSpec((1, tn), lambda j: (0, j)),
        out_shape=jax.ShapeDtypeStruct((1, n), F32),
        compiler_params=_params(("arbitrary",)),
        name="ada_mod",
    )(c_col, w_ada, b_ada)


def _split_bf16(v):
    parts = []
    for _ in range(N_SPLIT):
        p = v.astype(BF16)
        v = v - p.astype(F32)
        parts.append(p)
    return parts


def _inproj_kernel(x_ref, g_ref, sh_ref, sc_ref, wa_ref, wb_ref, wvt_ref, wg_ref, bf_ref, hind_ref, place_ref,
                   qa_ref, kva_ref, vat_ref, qb_ref, kb_ref, vbt_ref, sga_ref, sgb_ref, ca_ref, qn_ref, kn_ref,
                   cf_ref, cl_ref, dg_ref, carry_ref):
    i = pl.program_id(0)
    tm = x_ref.shape[0]
    t = FOX_TILE
    nsub = tm // t

    @pl.when(i == 0)
    def _():
        carry_ref[...] = jnp.zeros_like(carry_ref)

    xf = x_ref[...]
    ms = jnp.mean(xf * xf, axis=-1, keepdims=True)
    y = xf * lax.rsqrt(ms + RMS_EPS) * g_ref[...]
    u = y * (1.0 + sc_ref[...]) + sh_ref[...]
    ub = u.astype(BF16)

    za = _dot(ub, wa_ref[...])
    qa_ref[...] = (za[:, :SWA_Q] * (HEAD_DIM ** -0.5 * LOG2E)).astype(BF16)
    kva_ref[...] = za[:, SWA_Q:].astype(BF16)

    zb = _dot(ub, wb_ref[...])
    qb = (zb[:, :FOX_W] * (HEAD_DIM ** -0.5 * LOG2E)).astype(BF16)
    kb = zb[:, FOX_W:2 * FOX_W].astype(BF16)
    qb_ref[...] = qb
    kb_ref[...] = kb
    vt = _dot_nt(wvt_ref[...], ub).astype(BF16)
    vbt_ref[...] = vt[:FOX_W]
    vat_ref[...] = vt[FOX_W:]

    def tile_norm_max(z, o_ref):
        zf = z.astype(F32)
        n2 = _dot((zf * zf).astype(BF16), hind_ref[...])
        for sb in range(nsub):
            o_ref[sb] = jnp.max(n2[sb * t:(sb + 1) * t], axis=0, keepdims=True)

    tile_norm_max(qb, qn_ref)
    tile_norm_max(kb, kn_ref)
    diag = _dot((qb.astype(F32) * kb.astype(F32)).astype(BF16), hind_ref[...])
    for sb in range(nsub):
        dg_ref[sb] = jnp.min(diag[sb * t:(sb + 1) * t], axis=0, keepdims=True)

    zg = _dot(ub, wg_ref[...])
    sg = jax.nn.sigmoid(zg)
    sga_ref[...] = sg[:, :D_MODEL].astype(BF16)
    sgb_ref[...] = sg[:, D_MODEL:].astype(BF16)

    fb = zb[:, 2 * FOX_W:2 * FOX_W + FOX_HEADS] + bf_ref[...]
    lf = jnp.minimum(fb, 0.0) - jnp.log1p(jnp.exp(-jnp.abs(fb)))
    r = lax.broadcasted_iota(jnp.int32, (t, t), 0)
    cc = lax.broadcasted_iota(jnp.int32, (t, t), 1)
    lower = (cc <= r).astype(BF16)
    carry = carry_ref[...]
    for sb in range(nsub):
        rows = slice(sb * t, (sb + 1) * t)
        local = sum(_dot(lower, p) for p in _split_bf16(lf[rows]))
        cf_ref[sb] = (local[0:1, :] + carry) * LOG2E
        carry = carry + local[t - 1:t, :]
        cl_ref[sb] = carry * LOG2E
        aug = sum(_dot(p, place_ref[k]) for k, p in enumerate(_split_bf16(local * LOG2E)))
        ca_ref[rows, :] = aug.astype(BF16)
    carry_ref[...] = carry


def _inproj(x2, g, sh, sc, wa, wb, wvt, wg, bfor, tm):
    s = x2.shape[0]
    nsub = tm // FOX_TILE
    row = lambda n: pl.BlockSpec((tm, n), lambda i: (i, 0))
    hind = np.zeros((FOX_W, FOX_HEADS), np.float32)
    hind[np.arange(FOX_W), np.arange(FOX_W) // HEAD_DIM] = 1.0
    place = np.zeros((N_SPLIT, FOX_HEADS, LANES), np.float32)
    for k in range(N_SPLIT):
        for h in range(FOX_HEADS):
            place[k, h, N_SPLIT * h + k] = 1.0
    hind = jnp.asarray(hind, BF16)
    place = jnp.asarray(place, BF16)
    out_shape = [
        jax.ShapeDtypeStruct((s, SWA_Q), BF16),
        jax.ShapeDtypeStruct((s, 2 * LANES), BF16),
        jax.ShapeDtypeStruct((SWA_KV, s), BF16),
        jax.ShapeDtypeStruct((s, FOX_W), BF16),
        jax.ShapeDtypeStruct((s, FOX_W), BF16),
        jax.ShapeDtypeStruct((FOX_W, s), BF16),
        jax.ShapeDtypeStruct((s, D_MODEL), BF16),
        jax.ShapeDtypeStruct((s, D_MODEL), BF16),
        jax.ShapeDtypeStruct((s, LANES), BF16),
        jax.ShapeDtypeStruct((s // FOX_TILE, 1, FOX_HEADS), F32),
        jax.ShapeDtypeStruct((s // FOX_TILE, 1, FOX_HEADS), F32),
        jax.ShapeDtypeStruct((s // FOX_TILE, 1, FOX_HEADS), F32),
        jax.ShapeDtypeStruct((s // FOX_TILE, 1, FOX_HEADS), F32),
        jax.ShapeDtypeStruct((s // FOX_TILE, 1, FOX_HEADS), F32),
    ]
    stat = pl.BlockSpec((nsub, 1, FOX_HEADS), lambda i: (i, 0, 0))
    out_specs = [row(SWA_Q), row(2 * LANES), pl.BlockSpec((SWA_KV, tm), lambda i: (0, i)),
                 row(FOX_W), row(FOX_W), pl.BlockSpec((FOX_W, tm), lambda i: (0, i)),
                 row(D_MODEL), row(D_MODEL), row(LANES), stat, stat, stat, stat, stat]
    consts = [g, sh, sc, wa, wb, wvt, wg, bfor, hind, place]
    return pl.pallas_call(
        _inproj_kernel,
        grid=(s // tm,),
        in_specs=[row(D_MODEL)] + [_const_spec(a.shape) for a in consts],
        out_specs=out_specs,
        out_shape=out_shape,
        scratch_shapes=[pltpu.VMEM((1, FOX_HEADS), F32)],
        compiler_params=_params(("arbitrary",)),
        name="in_proj",
    )(x2, *consts)


def _t5_buckets_np():
    qi = np.arange(BLOCK)[:, None]
    kj = np.arange(2 * BLOCK)[None, :]
    dist = BLOCK + qi - kj
    n = np.maximum(dist, 0)
    max_exact = REL_BUCKETS // 2
    nf = np.maximum(n, 1).astype(np.float32)
    large = max_exact + (np.log(nf / np.float32(max_exact)) / np.float32(math.log(REL_MAX_DIST / max_exact))
                         * np.float32(REL_BUCKETS - max_exact)).astype(np.int32)
    large = np.minimum(large, REL_BUCKETS - 1)
    bucket = np.where(n < max_exact, n, large).astype(np.int32)
    band = (dist >= 0) & (dist < WINDOW)
    return np.where(band, bucket, -1).astype(np.int32)


def _swa_kernel(rel_ref, sink_ref, bkt_ref, q_ref, kc_ref, kp_ref, vc_ref, vp_ref, o_ref, bias_ref):
    n = pl.program_id(0)

    @pl.when(n == 0)
    def _():
        bkt = bkt_ref[...]
        prev = lax.broadcasted_iota(jnp.int32, bkt.shape, 0) < BLOCK
        for h in range(SWA_HEADS):
            acc = jnp.full(bkt.shape, NEG_BIG, F32)
            for b in range(REL_BUCKETS):
                acc = jnp.where(bkt == b, rel_ref[b * SWA_HEADS + h] * LOG2E, acc)
            cols = slice((h % 2) * BLOCK, (h % 2 + 1) * BLOCK)
            bias_ref[0, h // 2, :, cols] = acc
            bias_ref[1, h // 2, :, cols] = jnp.where(prev, NEG_BIG, acc)

    lane = lax.broadcasted_iota(jnp.int32, (BLOCK, LANES), 1)
    col2 = lax.broadcasted_iota(jnp.int32, (1, 2 * BLOCK), 1)
    k_all = jnp.concatenate([kp_ref[...], kc_ref[...]], axis=0)
    v_all = jnp.concatenate([vp_ref[...], vc_ref[...]], axis=1)
    work = [(b, p) for b in range(SWA_BLOCKS) for p in range(SWA_HEADS // 2)]
    scores = []
    for b, p in work:
        qp = q_ref[b * BLOCK:(b + 1) * BLOCK, p * LANES:(p + 1) * LANES]
        zero = jnp.zeros_like(qp)
        qs = jnp.concatenate([jnp.where(lane < HEAD_DIM, qp, zero), jnp.where(lane >= HEAD_DIM, qp, zero)], axis=0)
        g = p // 2
        scores.append(_dot_nt(k_all[b * BLOCK:(b + 2) * BLOCK, g * LANES:(g + 1) * LANES], qs))
    weights = []
    for (b, p), s in zip(work, scores):
        first = jnp.where(n == 0, 1, 0) if b == 0 else 0
        s = s + bias_ref[first, p]
        sink = jnp.where(col2 < BLOCK, sink_ref[2 * p], sink_ref[2 * p + 1]) * LOG2E
        m = jnp.maximum(jnp.max(s, axis=0, keepdims=True), sink)
        e = jnp.exp2(s - m)
        denom = jnp.sum(e, axis=0, keepdims=True) + jnp.exp2(sink - m)
        weights.append((e.astype(BF16), denom))
    outs = [[] for _ in range(SWA_BLOCKS)]
    for (b, p), (e, denom) in zip(work, weights):
        g = p // 2
        o = _dot(v_all[g * HEAD_DIM:(g + 1) * HEAD_DIM, b * BLOCK:(b + 2) * BLOCK], e) / denom
        outs[b] += [o[:, :BLOCK], o[:, BLOCK:]]
    for b in range(SWA_BLOCKS):
        o_ref[b * BLOCK:(b + 1) * BLOCK, :] = jnp.concatenate(outs[b], axis=0).T.astype(BF16)


def _swa(rel_flat, sinks, bkt_t, qa, ka2, vat):
    s = qa.shape[0]
    rows = SWA_BLOCKS * BLOCK
    smem = pl.BlockSpec(memory_space=pltpu.SMEM)
    prev = lambda n: jnp.maximum(n * SWA_BLOCKS - 1, 0)
    return pl.pallas_call(
        _swa_kernel,
        grid=(s // rows,),
        in_specs=[smem, smem, _const_spec(bkt_t.shape),
                  pl.BlockSpec((rows, SWA_Q), lambda n: (n, 0)),
                  pl.BlockSpec((rows, 2 * LANES), lambda n: (n, 0)),
                  pl.BlockSpec((BLOCK, 2 * LANES), lambda n: (prev(n), 0)),
                  pl.BlockSpec((SWA_KV, rows), lambda n: (0, n)),
                  pl.BlockSpec((SWA_KV, BLOCK), lambda n: (0, prev(n)))],
        out_specs=pl.BlockSpec((rows, SWA_Q), lambda n: (n, 0)),
        out_shape=jax.ShapeDtypeStruct((s, SWA_Q), BF16),
        scratch_shapes=[pltpu.VMEM((2, SWA_HEADS // 2, 2 * BLOCK, 2 * BLOCK), F32)],
        compiler_params=_params(("arbitrary",)),
        name="swa_attn",
    )(rel_flat, sinks, bkt_t, qa, ka2, ka2, vat, vat)


def _fox_kernel(jlo_ref, base_ref, q_ref, k_ref, ca_ref, vt_ref, o_ref):
    p = pl.program_id(0)
    i = pl.program_id(1)
    t = FOX_TILE
    tq = FOX_QSUB * t
    first_diag = i * FOX_QSUB
    lane = lax.broadcasted_iota(jnp.int32, (tq, LANES), 1)
    q = q_ref[...]
    wq = []
    for hh in range(2):
        in_head = (lane >= hh * HEAD_DIM) & (lane < (hh + 1) * HEAD_DIM)
        qm = jnp.where(in_head, q, jnp.zeros_like(q))
        first_term = N_SPLIT * (2 * p + hh)
        sel = (lane >= first_term) & (lane < first_term + N_SPLIT)
        aug = jnp.where(sel, -1.0, 0.0).astype(BF16)
        wq.append(jnp.concatenate([qm, aug], axis=1))
    key = lax.broadcasted_iota(jnp.int32, (t, tq), 0)
    qry = lax.broadcasted_iota(jnp.int32, (t, tq), 1)
    causal = [key + d * t <= qry for d in range(FOX_QSUB)]

    def step(j, carry, heads, nsub, diagonal=False):
        start = pl.multiple_of(j * t, t)
        rows = nsub * t
        lhs = jnp.concatenate([k_ref[pl.ds(start, rows), :], ca_ref[pl.ds(start, rows), :]], axis=1)
        scores = [_dot_nt(lhs, wq[hh]) for hh in heads]
        mid = []
        for hh, s, (m, l, acc) in zip(heads, scores, carry):
            h = 2 * p + hh
            parts = [s[k * t:(k + 1) * t] for k in range(nsub)]
            if diagonal:
                for d in range(FOX_QSUB):
                    k = nsub - FOX_QSUB + d
                    parts[k] = jnp.where(causal[d], parts[k], NEG_BIG)
            djs = [base_ref[h, first_diag] - base_ref[h, j + k] for k in range(nsub)]
            m_new = m
            for part, dj in zip(parts, djs):
                m_new = jnp.maximum(m_new, jnp.max(part, axis=0, keepdims=True) + dj)
            alpha = jnp.exp2(m - m_new)
            es = [jnp.exp2(part + (dj - m_new)) for part, dj in zip(parts, djs)]
            l = alpha * l
            for e in es:
                l = l + jnp.sum(e, axis=0, keepdims=True)
            e_all = es[0] if nsub == 1 else jnp.concatenate(es, axis=0)
            mid.append((m_new, l, alpha, acc, e_all.astype(BF16)))
        new = []
        for hh, (m_new, l, alpha, acc, e_all) in zip(heads, mid):
            vt = vt_ref[hh * HEAD_DIM:(hh + 1) * HEAD_DIM, pl.ds(start, rows)]
            new.append((m_new, l, alpha * acc + _dot(vt, e_all)))
        return tuple(new)

    def run_alone(lo, hi, carry, heads):
        n_long = (hi - lo) // FOX_CHUNK_ALONE
        carry = lax.fori_loop(0, n_long, lambda n, c: step(lo + FOX_CHUNK_ALONE * n, c, heads, FOX_CHUNK_ALONE), carry)
        mid = lo + FOX_CHUNK_ALONE * n_long
        n_full = (hi - mid) // FOX_CHUNK
        carry = lax.fori_loop(0, n_full, lambda n, c: step(mid + FOX_CHUNK * n, c, heads, FOX_CHUNK), carry)
        rest = mid + FOX_CHUNK * n_full
        tails = [lambda c: c] + [functools.partial(lambda c, k: step(rest, c, heads, k), k=k)
                                 for k in range(1, FOX_CHUNK)]
        return lax.switch(hi - rest, tails, carry)

    def run_to_diagonal(lo, carry, heads):
        end = first_diag + FOX_QSUB
        count = end - lo
        last = (count - FOX_QSUB) % FOX_CHUNK + FOX_QSUB
        n_full = (count - last) // FOX_CHUNK
        carry = lax.fori_loop(0, n_full, lambda n, c: step(lo + FOX_CHUNK * n, c, heads, FOX_CHUNK), carry)
        tails = [functools.partial(lambda c, k: step(end - k, c, heads, k, diagonal=True), k=k)
                 for k in range(FOX_QSUB, FOX_QSUB + FOX_CHUNK)]
        return lax.switch(last - FOX_QSUB, tails, carry)

    lo0 = jlo_ref[2 * p, i]
    lo1 = jlo_ref[2 * p + 1, i]
    lo_both = jnp.maximum(lo0, lo1)
    init = (jnp.full((1, tq), NEG_BIG, F32), jnp.zeros((1, tq), F32), jnp.zeros((HEAD_DIM, tq), F32))
    (c0,) = run_alone(lo0, lo_both, (init,), (0,))
    (c1,) = run_alone(lo1, lo_both, (init,), (1,))
    carry = run_to_diagonal(lo_both, (c0, c1), (0, 1))
    ot = jnp.concatenate([carry[0][2] / carry[0][1], carry[1][2] / carry[1][1]], axis=0)
    o_ref[...] = ot.T.astype(BF16)


def _fox_schedule(c_first, c_last, qn2, kn2, diag_min):
    nt = qn2.shape[0]
    bq = jnp.sqrt(qn2.reshape(nt, FOX_HEADS)) * NORM_SLACK
    bk = jnp.sqrt(kn2.reshape(nt, FOX_HEADS)) * NORM_SLACK
    c_first = c_first.reshape(nt, FOX_HEADS)
    c_last = c_last.reshape(nt, FOX_HEADS)
    diag_lo = diag_min.reshape(nt, FOX_HEADS) - (NORM_SLACK - 1.0) * bq * bk
    upper = bq[:, None, :] * bk[None, :, :] - diag_lo[:, None, :] + c_first[:, None, :] - c_last[None, :, :]
    ii = jnp.arange(nt)[:, None, None]
    jj = jnp.arange(nt)[None, :, None]
    needed = (jj <= ii) & ((upper >= -SKIP_LOG2) | (jj == ii))
    jlo = jnp.min(jnp.where(needed, jj, nt), axis=1)
    jlo = jnp.min(jlo.reshape(nt // FOX_QSUB, FOX_QSUB, FOX_HEADS), axis=1).T.astype(jnp.int32)
    base = jnp.concatenate([jnp.zeros((1, FOX_HEADS), F32), c_last[:-1]], axis=0).T
    return jlo, base


def _fox(jlo, base, qb, kb, ca, vbt):
    s = qb.shape[0]
    t = FOX_QSUB * FOX_TILE
    grid_spec = pltpu.PrefetchScalarGridSpec(
        num_scalar_prefetch=2,
        grid=(N_PAIRS, s // t),
        in_specs=[pl.BlockSpec((t, LANES), lambda p, i, *_: (i, p)),
                  pl.BlockSpec((s, LANES), lambda p, i, *_: (0, p)),
                  pl.BlockSpec((s, LANES), lambda p, i, *_: (0, 0)),
                  pl.BlockSpec((LANES, s), lambda p, i, *_: (p, 0))],
        out_specs=pl.BlockSpec((t, LANES), lambda p, i, *_: (i, p)),
    )
    return pl.pallas_call(
        _fox_kernel,
        grid_spec=grid_spec,
        out_shape=jax.ShapeDtypeStruct((s, FOX_W), BF16),
        compiler_params=_params(("arbitrary", "arbitrary")),
        name="fox_attn",
    )(jlo, base, qb, kb, ca, vbt)


def _rms(x, g):
    return x * lax.rsqrt(jnp.mean(x * x, axis=-1, keepdims=True) + RMS_EPS) * g


def _pack_rows(v):
    halves = []
    for j in range(2):
        lo = v[:, (2 * j) * PACK_W:(2 * j + 1) * PACK_W].astype(BF16).astype(F32)
        hi = v[:, (2 * j + 1) * PACK_W:(2 * j + 2) * PACK_W].astype(BF16).astype(F32)
        lo_bits = lax.bitcast_convert_type(lo, jnp.uint32)
        hi_bits = lax.bitcast_convert_type(hi, jnp.uint32)
        halves.append(hi_bits | (lo_bits >> 16))
    return halves


def _unpack_words(w):
    lo = lax.bitcast_convert_type(w << 16, F32)
    hi = lax.bitcast_convert_type(w & jnp.uint32(0xFFFF0000), F32)
    return lo, hi


def _outproj_kernel(x_ref, ya_ref, yb_ref, sga_ref, sgb_ref, pa_ref, pb_ref, wo_ref, gt_ref, g_ref, sh_ref, sc_ref,
                    wr_ref, br_ref, x1_ref, u2p_ref, e4_ref, r4_ref, w4_ref, cnt_ref, carry_ref):
    i = pl.program_id(0)
    tm = x_ref.shape[0]

    @pl.when(i == 0)
    def _():
        carry_ref[...] = jnp.zeros_like(carry_ref)

    merged = (sga_ref[...].astype(F32) * _dot(ya_ref[...], pa_ref[...])
              + sgb_ref[...].astype(F32) * _dot(yb_ref[...], pb_ref[...]))
    x1 = x_ref[...] + gt_ref[...] * _dot(merged.astype(BF16), wo_ref[...])
    x1_ref[...] = x1
    u2 = _rms(x1, g_ref[...]) * (1.0 + sc_ref[...]) + sh_ref[...]
    halves = _pack_rows(u2)
    u2p_ref[0] = halves[0]
    u2p_ref[1] = halves[1]

    logits = _dot(u2.astype(BF16), wr_ref[...]) + br_ref[...]
    eidx = lax.broadcasted_iota(jnp.int32, logits.shape, 1)
    work = logits
    sel = jnp.zeros(logits.shape, jnp.bool_)
    picks, vals = [], []
    for k in range(TOP_K):
        m = jnp.max(work, axis=-1, keepdims=True)
        first = jnp.min(jnp.where(work == m, eidx, N_EXPERTS), axis=-1, keepdims=True)
        hit = eidx == first
        sel = sel | hit
        work = jnp.where(hit, -jnp.inf, work)
        picks.append(first)
        vals.append(m)
    exps = [jnp.exp(v - vals[0]) for v in vals]
    denom = exps[0] + exps[1] + exps[2] + exps[3]

    r = lax.broadcasted_iota(jnp.int32, (tm, tm), 0)
    cc = lax.broadcasted_iota(jnp.int32, (tm, tm), 1)
    before = (cc < r).astype(BF16)
    chosen = sel.astype(BF16)
    rank = _dot(before, chosen) + carry_ref[...]
    cnt = carry_ref[...] + jnp.sum(chosen.astype(F32), axis=0, keepdims=True)
    carry_ref[...] = cnt
    cnt_ref[...] = cnt
    slot = lax.broadcasted_iota(jnp.int32, (tm, 4 * TOP_K), 1)
    cols = jnp.zeros((tm, 4 * TOP_K), F32)
    for k in range(TOP_K):
        rk = jnp.sum(jnp.where(eidx == picks[k], rank, 0.0), axis=-1, keepdims=True)
        hi = jnp.floor(rk * (1.0 / RANK_RADIX))
        for j, col in ((k, picks[k].astype(F32)), (TOP_K + k, hi), (2 * TOP_K + k, rk - hi * RANK_RADIX)):
            cols = jnp.where(slot == j, col, cols)
        w4_ref[:, k:k + 1] = exps[k] / denom
    eye = (lax.broadcasted_iota(jnp.int32, (4 * TOP_K, 4 * TOP_K), 0)
           == lax.broadcasted_iota(jnp.int32, (4 * TOP_K, 4 * TOP_K), 1)).astype(BF16)
    flipped = _dot_nt(eye, cols.astype(BF16))
    e4_ref[...] = flipped[:TOP_K].astype(jnp.int32)
    r4_ref[...] = (flipped[TOP_K:2 * TOP_K] * RANK_RADIX + flipped[2 * TOP_K:3 * TOP_K]).astype(jnp.int32)


def _outproj(x2, ya, yb, sga, sgb, pa, pb, wo, gt, g, sh, sc, wr, br, tm):
    s = x2.shape[0]
    row = lambda n: pl.BlockSpec((tm, n), lambda i: (i, 0))
    consts = [pa, pb, wo, gt, g, sh, sc, wr, br]
    return pl.pallas_call(
        _outproj_kernel,
        grid=(s // tm,),
        in_specs=[row(D_MODEL), row(SWA_Q), row(FOX_W), row(D_MODEL), row(D_MODEL)] + [_const_spec(a.shape) for a in consts],
        out_specs=[row(D_MODEL), pl.BlockSpec((2, tm, PACK_W), lambda i: (0, i, 0)),
                   pl.BlockSpec((TOP_K, tm), lambda i: (0, i)), pl.BlockSpec((TOP_K, tm), lambda i: (0, i)), row(TOP_K),
                   _const_spec((1, N_EXPERTS))],
        out_shape=[jax.ShapeDtypeStruct((s, D_MODEL), F32),
                   jax.ShapeDtypeStruct((2, s, PACK_W), jnp.uint32),
                   jax.ShapeDtypeStruct((TOP_K, s), jnp.int32),
                   jax.ShapeDtypeStruct((TOP_K, s), jnp.int32),
                   jax.ShapeDtypeStruct((s, TOP_K), F32),
                   jax.ShapeDtypeStruct((1, N_EXPERTS), F32)],
        scratch_shapes=[pltpu.VMEM((1, N_EXPERTS), F32)],
        compiler_params=_params(("arbitrary",)),
        name="out_proj_router",
    )(x2, ya, yb, sga, sgb, *consts)


def _sc_mesh():
    return plsc.VectorSubcoreMesh(core_axis_name="core", subcore_axis_name="subcore")


def _sc_dispatch(rows, idx, n_out):
    n, width = rows.shape

    @functools.partial(pl.kernel, out_type=jax.ShapeDtypeStruct((n_out, width), rows.dtype), mesh=_sc_mesh(),
                       scratch_types=[])
    def dispatch(x_hbm, i_hbm, o_hbm):
        def body(x_vmem, i_vmem):
            for k in range(TOP_K):
                pltpu.sync_copy(x_vmem, o_hbm.at[i_vmem.at[k]])

        pltpu.emit_pipeline(
            body, grid=(n // SC_WINDOW,),
            in_specs=[pl.BlockSpec((SC_WINDOW, width), lambda i: (i, 0)),
                      pl.BlockSpec((TOP_K, SC_WINDOW), lambda i: (0, i))],
            out_specs=[], core_axis_name=("core", "subcore"), dimension_semantics=(pltpu.PARALLEL,),
        )(x_hbm, i_hbm)

    return dispatch(rows, idx)


def _sc_gather(table, idx):
    n = idx.shape[1]
    width = table.shape[1]

    @functools.partial(pl.kernel, out_type=jax.ShapeDtypeStruct((n, width), table.dtype), mesh=_sc_mesh(),
                       scratch_types=[])
    def gather(t_hbm, i_hbm, o_hbm):
        def body(i_vmem, o_vmem):
            pltpu.sync_copy(t_hbm.at[i_vmem.at[0]], o_vmem)

        pltpu.emit_pipeline(
            body, grid=(n // SC_WINDOW,),
            in_specs=[pl.BlockSpec((1, SC_WINDOW), lambda i: (0, i))],
            out_specs=[pl.BlockSpec((SC_WINDOW, width), lambda i: (i, 0))],
            core_axis_name=("core", "subcore"), dimension_semantics=(pltpu.PARALLEL,),
        )(i_hbm, o_hbm)

    return gather(table, idx)


def _sc_pack_weights(w1, w2):
    r1, r2 = w1.shape[0], w2.shape[0]
    half = D_FF // 2

    def bf16_bits(v):
        u = plsc.bitcast(v, jnp.uint32)
        return (u + jnp.uint32(0x7FFF) + ((u >> 16) & jnp.uint32(1))) >> 16

    cp = pltpu.CompilerParams()
    if "needs_layout_passes" in pltpu.CompilerParams.__dataclass_fields__:
        cp = dataclasses.replace(cp, needs_layout_passes=False)

    @functools.partial(
        pl.kernel, mesh=_sc_mesh(), scratch_types=[], compiler_params=cp,
        out_type=(jax.ShapeDtypeStruct((r1, D_FF), jnp.uint32), jax.ShapeDtypeStruct((r2, D_MODEL // 2), jnp.uint32)))
    def pack(w1_hbm, w2_hbm, o1_hbm, o2_hbm):
        lanes = lax.iota(jnp.int32, SC_LANES)

        def body1(x_vmem, o_vmem):
            @pl.loop(0, SC_PACK_ROWS)
            def _(r):
                rr = jnp.full((SC_LANES,), r, jnp.int32)

                @pl.loop(0, half, step=SC_LANES)
                def _(c):
                    col = 2 * (c + lanes)
                    g0 = plsc.load_gather(x_vmem, [rr, col])
                    g1 = plsc.load_gather(x_vmem, [rr, col + 2 * half])
                    l0 = plsc.load_gather(x_vmem, [rr, col + 1])
                    l1 = plsc.load_gather(x_vmem, [rr, col + 2 * half + 1])
                    o_vmem[r, pl.ds(c, SC_LANES)] = bf16_bits(g0) | (bf16_bits(g1) << 16)
                    o_vmem[r, pl.ds(half + c, SC_LANES)] = bf16_bits(l0) | (bf16_bits(l1) << 16)

        def body2(x_vmem, o_vmem):
            @pl.loop(0, SC_PACK_ROWS)
            def _(r):
                @pl.loop(0, D_MODEL // 2, step=SC_LANES)
                def _(c):
                    lo = x_vmem[r, pl.ds(c, SC_LANES)]
                    hi = x_vmem[r, pl.ds(D_MODEL // 2 + c, SC_LANES)]
                    o_vmem[r, pl.ds(c, SC_LANES)] = bf16_bits(lo) | (bf16_bits(hi) << 16)

        for body, x_hbm, o_hbm, rows in ((body1, w1_hbm, o1_hbm, r1), (body2, w2_hbm, o2_hbm, r2)):
            pltpu.emit_pipeline(
                body, grid=(rows // SC_PACK_ROWS,),
                in_specs=[pl.BlockSpec((SC_PACK_ROWS, x_hbm.shape[1]), lambda i: (i, 0))],
                out_specs=[pl.BlockSpec((SC_PACK_ROWS, o_hbm.shape[1]), lambda i: (i, 0))],
                core_axis_name=("core", "subcore"), dimension_semantics=(pltpu.PARALLEL,),
            )(x_hbm, o_hbm)

    return pack(w1, w2)


def _moe_kernel(te_ref, first_ref, nact_ref, xs_ref, w1_ref, b1g_ref, b1l_ref, w2_ref, b2_ref, y_ref,
                wg_ref, wl_ref, w2b_ref):
    i = pl.program_id(0)
    live = i < nact_ref[0]

    @pl.when(jnp.logical_and(live, first_ref[i] == 1))
    def _():
        half = D_FF // 2
        for dst, words in ((wg_ref, w1_ref[0, :, :half]), (wl_ref, w1_ref[0, :, half:]), (w2b_ref, w2_ref[0])):
            lo, hi = _unpack_words(words)
            dst[:, :half] = lo.astype(BF16)
            dst[:, half:] = hi.astype(BF16)

    @pl.when(live)
    def _():
        chunks = []
        for j in range(2):
            lo, hi = _unpack_words(xs_ref[j])
            chunks += [lo.astype(BF16), hi.astype(BF16)]
        x = jnp.concatenate(chunks, axis=1)
        hg = _dot(x, wg_ref[...]) + b1g_ref[0]
        hl = _dot(x, wl_ref[...]) + b1l_ref[0]
        glu = jnp.minimum(hg, SWIGLU_LIMIT)
        lin = jnp.clip(hl, -SWIGLU_LIMIT, SWIGLU_LIMIT)
        a = glu * jax.nn.sigmoid(SWIGLU_ALPHA * glu) * (lin + 1.0)
        y = _dot(a.astype(BF16), w2b_ref[...]) + b2_ref[0]
        halves = _pack_rows(y)
        y_ref[0] = halves[0]
        y_ref[1] = halves[1]


def _moe_routed(tile_expert, tile_first, n_active, xs, w1, b1g, b1l, w2, b2):
    n_rows = xs.shape[1]
    n_tiles = n_rows // MOE_TILE
    live = lambda i, na: jnp.minimum(i, na[0] - 1)
    rows_spec = pl.BlockSpec((2, MOE_TILE, PACK_W), lambda i, te, tf, na: (0, live(i, na), 0))
    wsp = lambda a: pl.BlockSpec((1,) + a.shape[1:], lambda i, te, tf, na: (te[live(i, na)], 0, 0))
    grid_spec = pltpu.PrefetchScalarGridSpec(
        num_scalar_prefetch=3,
        grid=(n_tiles,),
        in_specs=[rows_spec, wsp(w1), wsp(b1g), wsp(b1l), wsp(w2), wsp(b2)],
        out_specs=rows_spec,
        scratch_shapes=[pltpu.VMEM((D_MODEL, D_FF), BF16), pltpu.VMEM((D_MODEL, D_FF), BF16),
                        pltpu.VMEM((D_FF, D_MODEL), BF16)],
    )
    return pl.pallas_call(
        _moe_kernel,
        grid_spec=grid_spec,
        out_shape=jax.ShapeDtypeStruct(xs.shape, jnp.uint32),
        compiler_params=_params(("arbitrary",)),
        name="moe_routed",
    )(tile_expert, tile_first, n_active, xs, w1, b1g, b1l, w2, b2)


def _final_kernel(x1_ref, yg_ref, w4_ref, gt_ref, gf_ref, o_ref):
    w4 = w4_ref[...]
    cols = []
    for j in range(2):
        lo_acc = hi_acc = None
        for k in range(TOP_K):
            lo, hi = _unpack_words(yg_ref[j, k])
            wk = w4[:, k:k + 1]
            lo_acc = wk * lo if lo_acc is None else lo_acc + wk * lo
            hi_acc = wk * hi if hi_acc is None else hi_acc + wk * hi
        cols += [lo_acc, hi_acc]
    moe = jnp.concatenate(cols, axis=1)
    x2 = x1_ref[...] + gt_ref[...] * moe
    o_ref[...] = _rms(x2, gf_ref[...])


def _final(x1, yg, w4, gt, gf, tm):
    s = x1.shape[0]
    row = lambda n: pl.BlockSpec((tm, n), lambda i: (i, 0))
    return pl.pallas_call(
        _final_kernel,
        grid=(s // tm,),
        in_specs=[row(D_MODEL), pl.BlockSpec((2, TOP_K, tm, PACK_W), lambda i: (0, 0, i, 0)), row(TOP_K),
                  _const_spec(gt.shape), _const_spec(gf.shape)],
        out_specs=row(D_MODEL),
        out_shape=jax.ShapeDtypeStruct((s, D_MODEL), F32),
        compiler_params=_params(("arbitrary",)),
        name="combine_final_norm",
    )(x1, yg, w4, gt, gf)


def _routing_tables(e4, r4, counts, s):
    n_rows = TOP_K * s + N_EXPERTS * MOE_TILE
    cnt = counts.reshape(N_EXPERTS).astype(jnp.int32)
    padded = ((cnt + MOE_TILE - 1) // MOE_TILE) * MOE_TILE
    ends = jnp.cumsum(padded)
    starts = ends - padded
    pos = r4
    for e in range(N_EXPERTS):
        pos = pos + jnp.where(e4 == e, starts[e], 0)
    tile_start = jnp.arange(n_rows // MOE_TILE, dtype=jnp.int32) * MOE_TILE
    tile_expert = jnp.minimum(jnp.sum(tile_start[:, None] >= ends[None, :], axis=1), N_EXPERTS - 1).astype(jnp.int32)
    tile_first = jnp.concatenate([jnp.ones((1,), jnp.int32), (tile_expert[1:] != tile_expert[:-1]).astype(jnp.int32)])
    n_active = (ends[-1:] // MOE_TILE).astype(jnp.int32)
    return pos, tile_expert, tile_first, n_active, n_rows


def kernel(x, c, w_ada, b_ada, g_mix, w_in, b_forget, sinks, rel_bias, w_proj_a, w_proj_b, w_out, g_ffn,
           w_router, b_router, w_e1, b_e1, w_e2, b_e2, g_final):
    b, s, d = x.shape
    assert b == 1 and d == D_MODEL and w_ada.shape[0] == 1
    assert s <= 256 * RANK_RADIX
    x2 = x.reshape(s, d)
    tm = min(512, s)

    mod = _ada(c.reshape(d, 1), w_ada[0], b_ada)
    sh_m, sc_m, gt_m, sh_f, sc_f, gt_f = [mod[:, k * d:(k + 1) * d] for k in range(N_MOD)]

    w = w_in[0]
    o_ka, o_va, o_b = SWA_Q, SWA_Q + SWA_KV, SWA_Q + 2 * SWA_KV
    o_f = o_b + 3 * FOX_W
    o_g = o_f + FOX_HEADS
    dup = lambda m: jnp.concatenate([m[:, :HEAD_DIM], m[:, :HEAD_DIM], m[:, HEAD_DIM:], m[:, HEAD_DIM:]], axis=1)
    wa = jnp.concatenate([w[:, :SWA_Q], dup(w[:, o_ka:o_va])], axis=1).astype(BF16)
    pad = jnp.zeros((d, LANES - FOX_HEADS), F32)
    wb = jnp.concatenate([w[:, o_b:o_b + 2 * FOX_W], w[:, o_f:o_g], pad], axis=1).astype(BF16)
    wvt = jnp.concatenate([w[:, o_b + 2 * FOX_W:o_f], w[:, o_va:o_b]], axis=1).T.astype(BF16)
    wg = w[:, o_g:].astype(BF16)
    qa, ka2, vat, qb, kb, vbt, sga, sgb, ca, qn2, kn2, c_first, c_last, diag_min = _inproj(
        x2, g_mix, sh_m, sc_m, wa, wb, wvt, wg, b_forget, tm)

    ya = _swa(rel_bias.reshape(-1), sinks[0], jnp.asarray(_t5_buckets_np().T), qa, ka2, vat)
    jlo, base = _fox_schedule(c_first, c_last, qn2, kn2, diag_min)
    yb = _fox(jlo, base, qb, kb, ca, vbt)

    x1, u2p, e4, r4, w4, counts = _outproj(
        x2, ya, yb, sga, sgb, w_proj_a[0].astype(BF16), w_proj_b[0].astype(BF16), w_out[0].astype(BF16),
        gt_m, g_ffn, sh_f, sc_f, w_router[0].astype(BF16), b_router, tm)

    pos, tile_expert, tile_first, n_active, n_rows = _routing_tables(e4, r4, counts, s)
    pos2 = jnp.concatenate([pos, pos + n_rows], axis=1)
    w1p, w2p = _sc_pack_weights(w_e1[0].reshape(N_EXPERTS * d, 2 * D_FF), w_e2[0].reshape(N_EXPERTS * D_FF, d))
    pos2, w1p, w2p = lax.optimization_barrier((pos2, w1p, w2p))
    xs = _sc_dispatch(u2p.reshape(2 * s, PACK_W), pos2, 2 * n_rows).reshape(2, n_rows, PACK_W)

    b1 = b_e1[0].reshape(N_EXPERTS, D_FF, 2)
    b1g = b1[:, None, :, 0]
    b1l = b1[:, None, :, 1]
    ys = _moe_routed(tile_expert, tile_first, n_active, xs, w1p.reshape(N_EXPERTS, d, D_FF), b1g, b1l,
                     w2p.reshape(N_EXPERTS, D_FF, d // 2), b_e2[0][:, None, :])

    gather_idx = pos2.reshape(TOP_K, 2, s).transpose(1, 0, 2).reshape(1, -1)
    yg = _sc_gather(ys.reshape(2 * n_rows, PACK_W), gather_idx).reshape(2, TOP_K, s, PACK_W)
    out = _final(x1, yg, w4, gt_f, g_final.reshape(1, d), tm)
    return out.reshape(b, s, d)
```
